```python
import jax, jax.numpy as jnp
from jax import lax
import numpy as np

D_MODEL = 1024
BATCH = 4
SEQ = 4096
DEPTH = 2

CTX_LEN = 256
GRID_W = 64
RMS_EPS = 1e-6
D_FF = 4 * D_MODEL

RW_HEADS = 8
RW_HEAD_DIM = 64
RW_WIDTH = RW_HEADS * RW_HEAD_DIM
RW_DECAY_RANK = 64
RW_ICLR_RANK = 64
RW_GATE_RANK = 128
RW_VRES_RANK = 32
RW_GN_EPS = 64e-5
RW_SPLIT = (RW_WIDTH, 2 * RW_WIDTH, 3 * RW_WIDTH,
            3 * RW_WIDTH + RW_DECAY_RANK,
            3 * RW_WIDTH + 2 * RW_DECAY_RANK,
            3 * RW_WIDTH + 2 * RW_DECAY_RANK + RW_ICLR_RANK,
            3 * RW_WIDTH + 2 * RW_DECAY_RANK + 2 * RW_ICLR_RANK)
RW_COLS = 3 * RW_WIDTH + 2 * RW_DECAY_RANK + 2 * RW_ICLR_RANK + RW_GATE_RANK

GLA_HEADS = 4
GLA_DK = 64
GLA_DV = 128
GLA_KW = GLA_HEADS * GLA_DK
GLA_VW = GLA_HEADS * GLA_DV
GLA_QKV_W = 2 * GLA_KW + GLA_VW
GLA_GATE_RANK = 16
GLA_TAU = 16.0
GLA_CHUNK = 64
GLA_SPLIT = (GLA_QKV_W, GLA_QKV_W + GLA_GATE_RANK, GLA_QKV_W + 2 * GLA_GATE_RANK)
GLA_COLS = GLA_QKV_W + 2 * GLA_GATE_RANK + GLA_VW

IN_SPLIT = (RW_COLS, RW_COLS + GLA_COLS)
IN_COLS = RW_COLS + GLA_COLS + 2 * D_MODEL

kernel_name = "hybrid_rwkv7_gla_dit_block"


def rms_norm(x, gain):
    xf = x.astype(jnp.float32)
    y = xf * lax.rsqrt(jnp.mean(xf * xf, axis=-1, keepdims=True) + RMS_EPS)
    return (y * gain).astype(x.dtype)


def modulate(x, gain, shift, scale):
    return rms_norm(x, gain) * (1.0 + scale) + shift


def shift_prev(t):
    return jnp.pad(t[:, :-1], ((0, 0), (1, 0), (0, 0)))


def shift_next(t):
    return jnp.pad(t[:, 1:], ((0, 0), (0, 1), (0, 0)))


def centred_conv3(t, w):
    return w[0] * shift_prev(t) + w[1] * t + w[2] * shift_next(t)


def to_scan_order(t, order, rows):
    if order == "row":
        return t
    b_, n_, f_ = t.shape
    return t.reshape(b_, rows, GRID_W, f_).transpose(0, 2, 1, 3).reshape(b_, n_, f_)


def from_scan_order(t, order, rows):
    if order == "row":
        return t
    b_, n_, f_ = t.shape
    return t.reshape(b_, GRID_W, rows, f_).transpose(0, 2, 1, 3).reshape(b_, n_, f_)


def rwkv7_scan(r, decay, k, v, kk, a, s0, reverse):
    def step(s, inp):
        r_t, w_t, k_t, v_t, kk_t, a_t = inp
        sa = jnp.einsum("bhvk,bhk->bhv", s, kk_t)
        s = (s * w_t[:, :, None, :]
             - sa[..., None] * (kk_t * a_t)[:, :, None, :]
             + v_t[..., None] * k_t[:, :, None, :])
        return s, jnp.einsum("bhvk,bhk->bhv", s, r_t)
    xs = tuple(jnp.moveaxis(t.astype(jnp.float32), 1, 0) for t in (r, decay, k, v, kk, a))
    s_final, ys = lax.scan(step, s0, xs, reverse=reverse)
    return jnp.moveaxis(ys, 0, 1), s_final


def rwkv7_branch(f, v_first, s0, lp):
    b_, t_, _ = f.shape
    heads = lambda t: t.reshape(b_, t_, RW_HEADS, RW_HEAD_DIM)
    f = f + lp["rw_mu"] * (0.5 * (shift_prev(f) + shift_next(f)) - f)
    r, k, v, wd_f, wd_b, ad_f, ad_b, gd = jnp.split(f, RW_SPLIT, axis=-1)
    if v_first is not None:
        v = v + (v_first - v) * jax.nn.sigmoid(
            lp["rw_vres_bias"] + (v @ lp["rw_vres_down"]) @ lp["rw_vres_up"])
    kk = heads(k * lp["rw_k_k"]).astype(jnp.float32)
    kk = kk * lax.rsqrt(jnp.sum(kk * kk, axis=-1, keepdims=True) + 1e-12)

    def direction(d, wd, ad, reverse):
        w = -jax.nn.softplus(-(lp["rw_w0"][d] + jnp.tanh(wd) @ lp["rw_w_up"][d])) - 0.5
        decay = jnp.exp(-jnp.exp(w.astype(jnp.float32)))
        a = jax.nn.sigmoid(lp["rw_a0"][d] + ad @ lp["rw_a_up"][d])
        k_rep = k * (1.0 + (a - 1.0) * lp["rw_k_a"])
        y, s = rwkv7_scan(heads(r), heads(decay), heads(k_rep), heads(v), kk, heads(a), s0[d], reverse)
        return y, s, k_rep

    y_f, s_f, k_f = direction(0, wd_f, ad_f, False)
    y_b, s_b, k_b = direction(1, wd_b, ad_b, True)
    y = y_f + y_b
    mean = jnp.mean(y, axis=-1, keepdims=True)
    var = jnp.mean(jnp.square(y - mean), axis=-1, keepdims=True)
    y = ((y - mean) * lax.rsqrt(var + RW_GN_EPS) * lp["rw_gn_w"].reshape(RW_HEADS, RW_HEAD_DIM)
         + lp["rw_gn_b"].reshape(RW_HEADS, RW_HEAD_DIM))
    bonus = jnp.sum(heads(r) * heads(k_f + k_b) * lp["rw_r_k"], axis=-1, keepdims=True) * heads(v)
    g = jax.nn.sigmoid(gd) @ lp["rw_g_up"]
    out = (y + bonus).reshape(b_, t_, RW_WIDTH) * g
    return out.astype(f.dtype), (s_f, s_b), v


def gla_chunked(q, k, v, log_alpha, s0):
    b_, t_, h_, _ = q.shape
    nc = t_ // GLA_CHUNK
    chunk = lambda t: t.reshape(b_, nc, GLA_CHUNK, h_, t.shape[-1])
    qc, kc, vc, gc = chunk(q), chunk(k), chunk(v), chunk(log_alpha)
    cum = jnp.cumsum(gc, axis=2)
    cum_last = cum[:, :, -1:]
    q_dec = qc * jnp.exp(cum)
    k_inv = kc * jnp.exp(-cum)
    k_to_end = kc * jnp.exp(cum_last - cum)
    causal = jnp.tril(jnp.ones((GLA_CHUNK, GLA_CHUNK), dtype=bool))
    scores = jnp.where(causal, jnp.einsum("bnthd,bnshd->bnhts", q_dec, k_inv), 0.0)
    o_intra = jnp.einsum("bnhts,bnshv->bnthv", scores, vc)
    chunk_kv = jnp.einsum("bnshd,bnshv->bnhdv", k_to_end, vc)
    chunk_decay = jnp.exp(cum_last[:, :, 0])

    def step(s, inp):
        dec, kv = inp
        return s * dec[..., None] + kv, s
    s_final, s_start = lax.scan(step, s0, (jnp.moveaxis(chunk_decay, 1, 0), jnp.moveaxis(chunk_kv, 1, 0)))
    o_inter = jnp.einsum("bnthd,bnhdv->bnthv", q_dec, jnp.moveaxis(s_start, 0, 1))
    return (o_intra + o_inter).reshape(b_, t_, h_, v.shape[-1]), s_final


def gla_branch(f, s0, lp):
    b_, t_, _ = f.shape
    qkv, ald_f, ald_b, gate = jnp.split(f, GLA_SPLIT, axis=-1)
    qkv = jax.nn.silu(centred_conv3(qkv, lp["gla_conv"]))
    q, k, v = jnp.split(qkv, (GLA_KW, 2 * GLA_KW), axis=-1)
    heads_k = lambda t: t.reshape(b_, t_, GLA_HEADS, GLA_DK).astype(jnp.float32)
    q = heads_k(q) * (GLA_DK ** -0.5)
    k = heads_k(k)
    v = v.reshape(b_, t_, GLA_HEADS, GLA_DV).astype(jnp.float32)
    log_alpha = lambda d, ad: heads_k(
        jax.nn.log_sigmoid(ad @ lp["gla_alpha_up"][d] + lp["gla_alpha_bias"][d])) / GLA_TAU
    flip = lambda t: t[:, ::-1]
    o_f, s_f = gla_chunked(q, k, v, log_alpha(0, ald_f), s0[0])
    o_b, s_b = gla_chunked(flip(q), flip(k), flip(v), flip(log_alpha(1, ald_b)), s0[1])
    o = o_f + flip(o_b)
    o = o * lax.rsqrt(jnp.mean(o * o, axis=-1, keepdims=True) + RMS_EPS) * lp["gla_norm_w"]
    out = o.reshape(b_, t_, GLA_VW) * jax.nn.silu(gate)
    return out.astype(f.dtype), (s_f, s_b)


def merge_branches(y_rw, y_gla, gate_cols, lp):
    g_rw, g_gla = jnp.split(jax.nn.sigmoid(gate_cols), 2, axis=-1)
    return (g_rw * (y_rw @ lp["rw_out"]) + g_gla * (y_gla @ lp["gla_out"])) @ lp["merge_out"]


def token_mixer(h_lat, h_ctx, v_first, lp, layer, rows, need_ctx_out):
    rw_order, gla_order = ("row", "col") if layer % 2 == 0 else ("col", "row")
    rw_lat, gla_lat, gate_lat = jnp.split(h_lat @ lp["w_in"], IN_SPLIT, axis=-1)
    rw_ctx, gla_ctx, gate_ctx = jnp.split(h_ctx @ lp["w_in"], IN_SPLIT, axis=-1)
    vf_lat, vf_ctx = v_first
    b_ = h_lat.shape[0]
    z_rw = jnp.zeros((b_, RW_HEADS, RW_HEAD_DIM, RW_HEAD_DIM), jnp.float32)
    z_gla = jnp.zeros((b_, GLA_HEADS, GLA_DK, GLA_DV), jnp.float32)
    y_rw_ctx, s_rw, v_ctx = rwkv7_branch(rw_ctx, vf_ctx, (z_rw, z_rw), lp)
    y_gla_ctx, s_gla = gla_branch(gla_ctx, (z_gla, z_gla), lp)
    vf_lat_o = None if vf_lat is None else to_scan_order(vf_lat, rw_order, rows)
    y_rw_lat, _, v_lat = rwkv7_branch(to_scan_order(rw_lat, rw_order, rows), vf_lat_o, s_rw, lp)
    y_rw_lat = from_scan_order(y_rw_lat, rw_order, rows)
    v_lat = from_scan_order(v_lat, rw_order, rows)
    y_gla_lat, _ = gla_branch(to_scan_order(gla_lat, gla_order, rows), s_gla, lp)
    y_gla_lat = from_scan_order(y_gla_lat, gla_order, rows)
    m_lat = merge_branches(y_rw_lat, y_gla_lat, gate_lat, lp)
    m_ctx = merge_branches(y_rw_ctx, y_gla_ctx, gate_ctx, lp) if need_ctx_out else None
    return m_lat, m_ctx, (v_lat, v_ctx)


def sq_relu_mlp(h, w1, w2):
    return jnp.square(jax.nn.relu(h @ w1)) @ w2


def setup_inputs(seed: int = 0) -> dict:
    key = jax.random.key(seed)
    ks = list(jax.random.split(key, 40))
    nrm = lambda shape, s: jax.random.normal(ks.pop(), shape, jnp.float32) * s
    uni = lambda shape, lo, hi: jax.random.uniform(ks.pop(), shape, jnp.float32, lo, hi)
    L, D = DEPTH, D_MODEL
    return {
        "x": nrm((BATCH, SEQ, D), 1.0),
        "c": nrm((BATCH, D), 1.0),
        "ctx": nrm((BATCH, CTX_LEN, D), 1.0),
        "c_ctx": nrm((D,), 1.0),
        "w_in": nrm((L, D, IN_COLS), D ** -0.5),
        "rw_mu": uni((L, RW_COLS), 0.0, 1.0),
        "rw_w0": uni((L, 2, RW_WIDTH), -6.0, 1.0),
        "rw_w_up": nrm((L, 2, RW_DECAY_RANK, RW_WIDTH), 0.1),
        "rw_a0": nrm((L, 2, RW_WIDTH), 0.3),
        "rw_a_up": nrm((L, 2, RW_ICLR_RANK, RW_WIDTH), 0.1),
        "rw_g_up": nrm((L, RW_GATE_RANK, RW_WIDTH), RW_GATE_RANK ** -0.5),
        "rw_k_k": 0.85 + nrm((L, RW_WIDTH), 0.05),
        "rw_k_a": 1.0 + nrm((L, RW_WIDTH), 0.05),
        "rw_r_k": nrm((L, RW_HEADS, RW_HEAD_DIM), 0.1),
        "rw_gn_w": 1.0 + nrm((L, RW_WIDTH), 0.02),
        "rw_gn_b": nrm((L, RW_WIDTH), 0.02),
        "rw_vres_down": nrm((L - 1, RW_WIDTH, RW_VRES_RANK), RW_WIDTH ** -0.5),
        "rw_vres_up": nrm((L - 1, RW_VRES_RANK, RW_WIDTH), RW_VRES_RANK ** -0.5),
        "rw_vres_bias": nrm((L - 1, RW_WIDTH), 0.1),
        "rw_out": nrm((L, RW_WIDTH, D), RW_WIDTH ** -0.5),
        "gla_conv": nrm((L, 3, GLA_QKV_W), 0.6),
        "gla_alpha_up": nrm((L, 2, GLA_GATE_RANK, GLA_KW), GLA_GATE_RANK ** -0.5),
        "gla_alpha_bias": uni((L, 2, GLA_KW), 0.0, 4.0),
        "gla_norm_w": 1.0 + nrm((L, GLA_DV), 0.02),
        "gla_out": nrm((L, GLA_VW, D), GLA_VW ** -0.5),
        "merge_out": nrm((L, D, D), D ** -0.5),
        "mlp_w1": nrm((L, D, D_FF), D ** -0.5),
        "mlp_w2": nrm((L, D_FF, D), D_FF ** -0.5),
        "ada_w": nrm((L, D, 6 * D), 0.5 * D ** -0.5),
        "ada_b": nrm((L, 6 * D), 0.02),
        "norm_mix_pre": 1.0 + nrm((L, D), 0.02),
        "norm_mix_post": 1.0 + nrm((L, D), 0.02),
        "norm_ffn_pre": 1.0 + nrm((L, D), 0.02),
        "norm_ffn_post": 1.0 + nrm((L, D), 0.02),
    }


def reference(x, c, ctx, c_ctx, w_in, rw_mu, rw_w0, rw_w_up, rw_a0, rw_a_up, rw_g_up,
              rw_k_k, rw_k_a, rw_r_k, rw_gn_w, rw_gn_b, rw_vres_down, rw_vres_up, rw_vres_bias,
              rw_out, gla_conv, gla_alpha_up, gla_alpha_bias, gla_norm_w, gla_out, merge_out,
              mlp_w1, mlp_w2, ada_w, ada_b, norm_mix_pre, norm_mix_post, norm_ffn_pre,
              norm_ffn_post):
    rows = x.shape[1] // GRID_W
    x_lat, x_ctx = x, ctx
    v_first = (None, None)
    for l in range(DEPTH):
        last = l == DEPTH - 1
        lp = {"w_in": w_in[l], "rw_mu": rw_mu[l], "rw_w0": rw_w0[l], "rw_w_up": rw_w_up[l],
              "rw_a0": rw_a0[l], "rw_a_up": rw_a_up[l], "rw_g_up": rw_g_up[l],
              "rw_k_k": rw_k_k[l], "rw_k_a": rw_k_a[l], "rw_r_k": rw_r_k[l],
              "rw_gn_w": rw_gn_w[l], "rw_gn_b": rw_gn_b[l], "rw_out": rw_out[l],
              "gla_conv": gla_conv[l], "gla_alpha_up": gla_alpha_up[l],
              "gla_alpha_bias": gla_alpha_bias[l], "gla_norm_w": gla_norm_w[l],
              "gla_out": gla_out[l], "merge_out": merge_out[l]}
        if l > 0:
            lp["rw_vres_down"] = rw_vres_down[l - 1]
            lp["rw_vres_up"] = rw_vres_up[l - 1]
            lp["rw_vres_bias"] = rw_vres_bias[l - 1]
        mod_lat = (jax.nn.silu(c) @ ada_w[l] + ada_b[l])[:, None, :]
        mod_ctx = jax.nn.silu(c_ctx) @ ada_w[l] + ada_b[l]
        sh_m, sc_m, g_m, sh_f, sc_f, g_f = jnp.split(mod_lat, 6, axis=-1)
        csh_m, csc_m, cg_m, csh_f, csc_f, cg_f = jnp.split(mod_ctx, 6, axis=-1)

        h_lat = modulate(x_lat, norm_mix_pre[l], sh_m, sc_m)
        h_ctx = modulate(x_ctx, norm_mix_pre[l], csh_m, csc_m)
        m_lat, m_ctx, v_new = token_mixer(h_lat, h_ctx, v_first, lp, l, rows, not last)
        if l == 0:
            v_first = v_new
        x_lat = x_lat + g_m * rms_norm(m_lat, norm_mix_post[l])
        f_lat = sq_relu_mlp(modulate(x_lat, norm_ffn_pre[l], sh_f, sc_f), mlp_w1[l], mlp_w2[l])
        x_lat = x_lat + g_f * rms_norm(f_lat, norm_ffn_post[l])
        if not last:
            x_ctx = x_ctx + cg_m * rms_norm(m_ctx, norm_mix_post[l])
            f_ctx = sq_relu_mlp(modulate(x_ctx, norm_ffn_pre[l], csh_f, csc_f), mlp_w1[l], mlp_w2[l])
            x_ctx = x_ctx + cg_f * rms_norm(f_ctx, norm_ffn_post[l])
    return x_lat
```

```python
import functools

import jax
import jax.numpy as jnp
from jax import lax
from jax.experimental import pallas as pl
from jax.experimental.pallas import tpu as pltpu

F32 = jnp.float32
BF16 = jnp.bfloat16

GRID_W = 64
RMS_EPS = 1e-6
RW_HEADS = 8
RW_HEAD_DIM = 64
RW_WIDTH = RW_HEADS * RW_HEAD_DIM
RW_GN_EPS = 64e-5
RW_COLS = 1920
GLA_HEADS = 4
GLA_DK = 64
GLA_DV = 128
GLA_KW = GLA_HEADS * GLA_DK
GLA_VW = GLA_HEADS * GLA_DV
GLA_QKV_W = 2 * GLA_KW + GLA_VW
GLA_GATE_RANK = 16
GLA_TAU = 16.0
CHUNK = 64
LANES = 128
SUBLANES = 8
VMEM_LIMIT = 56 * 1024 * 1024
TOKEN_TILE = 256


def _dot(a, b):
    return jnp.dot(a.astype(BF16), b.astype(BF16), preferred_element_type=F32)


def _dot_nt(a, b):
    return lax.dot_general(a.astype(BF16), b.astype(BF16), (((1,), (1,)), ((), ())),
                           preferred_element_type=F32)


def _dot_tn(a, b):
    return lax.dot_general(a.astype(BF16), b.astype(BF16), (((0,), (0,)), ((), ())),
                           preferred_element_type=F32)


def _split2(x):
    hi = x.astype(BF16)
    lo = (x - hi.astype(F32)).astype(BF16)
    return hi, lo


def _dot_ind_rhs(x, ind):
    hi, lo = _split2(x)
    return (jnp.dot(hi, ind, preferred_element_type=F32)
            + jnp.dot(lo, ind, preferred_element_type=F32))


def _dot_ind_lhs(ind, x):
    hi, lo = _split2(x)
    return (jnp.dot(ind, hi, preferred_element_type=F32)
            + jnp.dot(ind, lo, preferred_element_type=F32))


def _sigmoid(x):
    return jax.nn.sigmoid(x)


def _silu(x):
    return x * jax.nn.sigmoid(x)


def _softplus(z):
    return jnp.maximum(z, 0.0) + jnp.log1p(jnp.exp(-jnp.abs(z)))


def _log_sigmoid(z):
    return -_softplus(-z)


def _rms(x, gain):
    return x * lax.rsqrt(jnp.mean(x * x, axis=-1, keepdims=True) + RMS_EPS) * gain


def _load_tok(ref, transposed, a, k, f):
    if not transposed:
        return ref[...]
    return jnp.concatenate([ref[:, i * f:(i + 1) * f] for i in range(k)], axis=0)


def _store_tok(ref, val, transposed, a, k, f):
    if not transposed:
        ref[...] = val
    else:
        for i in range(k):
            ref[:, i * f:(i + 1) * f] = val[i * a:(i + 1) * a, :]


def _tok_view(arr, transposed, a):
    if not transposed:
        return arr
    b, t, f = arr.shape
    return arr.reshape(b, a, (t // a) * f)


def _tok_unview(arr, transposed, t, f):
    if not transposed:
        return arr
    return arr.reshape(arr.shape[0], t, f)


def _tok_spec(f, tm, transposed, a):
    if not transposed:
        return pl.BlockSpec((None, tm, f), lambda b, j: (b, j, 0))
    return pl.BlockSpec((None, a, (tm // a) * f), lambda b, j: (b, 0, j))


def _tok_shape(bsz, t, f, transposed, a):
    if not transposed:
        return jax.ShapeDtypeStruct((bsz, t, f), F32)
    return jax.ShapeDtypeStruct((bsz, a, (t // a) * f), F32)


def _const_spec(shape):
    nd = len(shape)
    return pl.BlockSpec(shape, lambda *_: (0,) * nd, pipeline_mode=pl.Buffered(1))


def _params(ndim):
    return pltpu.CompilerParams(dimension_semantics=("arbitrary",) * ndim,
                                vmem_limit_bytes=VMEM_LIMIT)


def _halo_specs(f, tm, t):
    sub = tm // SUBLANES
    last = t // SUBLANES - 1
    prev = pl.BlockSpec((None, SUBLANES, f), lambda b, j: (b, jnp.maximum(j * sub - 1, 0), 0))
    nxt = pl.BlockSpec((None, SUBLANES, f), lambda b, j: (b, jnp.minimum((j + 1) * sub, last), 0))
    return prev, nxt


def _shifted(cur, prev8, next8, first, last):
    tm = cur.shape[0]
    row = lax.broadcasted_iota(jnp.int32, cur.shape, 0)
    p_row = jnp.where(first, 0.0, prev8[SUBLANES - 1:SUBLANES, :])
    n_row = jnp.where(last, 0.0, next8[0:1, :])
    prev = jnp.where(row == 0, p_row, pltpu.roll(cur, 1, 0))
    nxt = jnp.where(row == tm - 1, n_row, pltpu.roll(cur, tm - 1, 0))
    return prev, nxt


def _ada_kernel(c_ref, w_ref, b_ref, o_ref):
    cc = c_ref[...]
    o_ref[...] = _dot(_silu(cc), w_ref[...]) + b_ref[...]


def _ada_mods(cc, ada_w, ada_b):
    nl, d, n6 = ada_w.shape
    tn = d
    return pl.pallas_call(
        _ada_kernel,
        grid=(nl, n6 // tn),
        in_specs=[pl.BlockSpec((SUBLANES, d), lambda l, n: (0, 0)),
                  pl.BlockSpec((None, d, tn), lambda l, n: (l, 0, n)),
                  pl.BlockSpec((None, 1, tn), lambda l, n: (l, 0, n))],
        out_specs=pl.BlockSpec((None, SUBLANES, tn), lambda l, n: (l, 0, n)),
        out_shape=jax.ShapeDtypeStruct((nl, SUBLANES, n6), F32),
        compiler_params=_params(2),
        name="ada",
    )(cc, ada_w, ada_b.reshape(nl, 1, n6))


def _inproj_kernel(x_ref, mod_ref, gain_ref, w_rw, w_q, w_ald, w_gg, w_mg,
                   o_rw, o_q, o_ald, o_gg, o_mg, *, x_tr, g_tr, a, k):
    d = gain_ref.shape[-1]
    x = _load_tok(x_ref, x_tr, a, k, d)
    h = _rms(x, gain_ref[...]) * (1.0 + mod_ref[1:2, :]) + mod_ref[0:1, :]
    hb = h.astype(BF16)
    o_rw[...] = jnp.dot(hb, w_rw[...], preferred_element_type=F32)
    o_mg[...] = jnp.dot(hb, w_mg[...], preferred_element_type=F32)
    _store_tok(o_q, jnp.dot(hb, w_q[...], preferred_element_type=F32), g_tr, a, k, GLA_QKV_W)
    _store_tok(o_ald, jnp.dot(hb, w_ald[...], preferred_element_type=F32), g_tr, a, k, LANES)
    _store_tok(o_gg, jnp.dot(hb, w_gg[...], preferred_element_type=F32), g_tr, a, k, GLA_VW)


def _inproj(x, mods, mod_row, gain, ws, *, x_tr, g_tr, a, tm):
    bsz, t, d = x.shape
    k = tm // a
    kern = functools.partial(_inproj_kernel, x_tr=x_tr, g_tr=g_tr, a=a, k=k)
    widths = (RW_COLS, GLA_QKV_W, LANES, GLA_VW, 2 * d)
    out_tr = (False, g_tr, g_tr, g_tr, False)
    outs = pl.pallas_call(
        kern,
        grid=(bsz, t // tm),
        in_specs=[_tok_spec(d, tm, x_tr, a),
                  pl.BlockSpec((None, 6, d), lambda b, j: (mod_row(b), 0, 0)),
                  _const_spec((1, d))] + [_const_spec(w.shape) for w in ws],
        out_specs=[_tok_spec(f, tm, tr, a) for f, tr in zip(widths, out_tr)],
        out_shape=[_tok_shape(bsz, t, f, tr, a) for f, tr in zip(widths, out_tr)],
        compiler_params=_params(2),
        name="inproj",
    )(_tok_view(x, x_tr, a), mods, gain, *ws)
    return [_tok_unview(o, tr, t, f) for o, f, tr in zip(outs, widths, out_tr)]


def _rwprep_kernel(*refs, has_vres, vf_tr, a, k):
    it = iter(refs)
    f_cur, f_prev, f_next = next(it), next(it), next(it)
    vf_ref = next(it) if has_vres else None
    mu, w0, w_up, a0, a_up, g_up, k_k, k_a, r_k = (next(it) for _ in range(9))
    if has_vres:
        vdown, vup, vbias = next(it), next(it), next(it)
    hind = next(it)
    o_r, o_v, o_kk, o_g, o_gb, o_lw, o_kd, o_bd = (next(it) for _ in range(8))

    j = pl.program_id(1)
    first = j == 0
    last = j == pl.num_programs(1) - 1
    f = f_cur[...]
    prev, nxt = _shifted(f, f_prev[...], f_next[...], first, last)
    fs = f + mu[...] * (0.5 * (prev + nxt) - f)
    w = RW_WIDTH
    r = fs[:, 0:w]
    kx = fs[:, w:2 * w]
    v = fs[:, 2 * w:3 * w]
    wd = fs[:, 3 * w:3 * w + LANES]
    ad = fs[:, 3 * w + LANES:3 * w + 2 * LANES]
    gd = fs[:, 3 * w + 2 * LANES:3 * w + 3 * LANES]
    if has_vres:
        vf = _load_tok(vf_ref, vf_tr, a, k, w)
        mix = _sigmoid(vbias[...] + _dot(_dot(v, vdown[...]), vup[...]))
        v = v + (vf - v) * mix
    kk = kx * k_k[...]
    kk = kk * lax.rsqrt(_dot_ind_rhs(kk * kk, hind[...]) + 1e-12)
    twd = jnp.tanh(wd)
    ksum = None
    for di in range(2):
        wlog = -_softplus(-(w0[di:di + 1, :] + _dot(twd, w_up[di]))) - 0.5
        o_lw[di] = -jnp.exp(wlog)
        lr = _sigmoid(a0[di:di + 1, :] + _dot(ad, a_up[di]))
        kd = kx * (1.0 + (lr - 1.0) * k_a[...])
        o_kd[di] = kd
        o_bd[di] = lr * kk
        ksum = kd if ksum is None else ksum + kd
    g = _dot(_sigmoid(gd), g_up[...])
    bonus = _dot_ind_rhs(r * ksum * r_k[...], hind[...]) * v
    o_r[...] = r
    o_v[...] = v
    o_kk[...] = kk
    o_g[...] = g
    o_gb[...] = bonus * g


def _rwprep(f, vfirst, lp, hind, *, vf_tr, a, tm):
    bsz, t, fc = f.shape
    w = RW_WIDTH
    has_vres = vfirst is not None
    k = tm // a
    kern = functools.partial(_rwprep_kernel, has_vres=has_vres, vf_tr=vf_tr, a=a, k=k)
    prev_spec, next_spec = _halo_specs(fc, tm, t)
    args = [f, f, f]
    in_specs = [_tok_spec(fc, tm, False, a), prev_spec, next_spec]
    if has_vres:
        args.append(_tok_view(vfirst, vf_tr, a))
        in_specs.append(_tok_spec(w, tm, vf_tr, a))
    small = [lp["rw_mu"], lp["rw_w0"], lp["rw_w_up_pad"], lp["rw_a0"], lp["rw_a_up_pad"],
             lp["rw_g_up"], lp["rw_k_k"], lp["rw_k_a"], lp["rw_r_k"]]
    if has_vres:
        small += [lp["rw_vres_down"], lp["rw_vres_up"], lp["rw_vres_bias"]]
    small.append(hind)
    args += small
    in_specs += [_const_spec(s.shape) for s in small]
    tok = pl.BlockSpec((None, tm, w), lambda b, j: (b, j, 0))
    tok2 = pl.BlockSpec((2, None, tm, w), lambda b, j: (0, b, j, 0))
    s1 = jax.ShapeDtypeStruct((bsz, t, w), F32)
    s2 = jax.ShapeDtypeStruct((2, bsz, t, w), F32)
    return pl.pallas_call(
        kern,
        grid=(bsz, t // tm),
        in_specs=in_specs,
        out_specs=[tok] * 5 + [tok2] * 3,
        out_shape=[s1] * 5 + [s2] * 3,
        compiler_params=_params(2),
        name="rwprep",
    )(*args)


def _scan_masks(d):
    row = lax.broadcasted_iota(jnp.int32, (CHUNK, CHUNK), 0)
    col = lax.broadcasted_iota(jnp.int32, (CHUNK, CHUNK), 1)
    diff = (row - col) * (1 - 2 * d)
    return row, col, diff >= 0, diff > 0


def _rwscan_kernel(r_ref, v_ref, kk_ref, lw_ref, kd_ref, bd_ref, s0_ref, y_ref, sfin_ref, s_scr):
    d = pl.program_id(1)
    c = pl.program_id(2)

    @pl.when(c == 0)
    def _():
        s_scr[...] = s0_ref[...]

    row, col, m_incl, m_strict = _scan_masks(d)
    tri = jnp.where(m_incl, 1.0, 0.0).astype(BF16)
    eye = jnp.where(row == col, 1.0, 0.0)
    lw = lw_ref[...]
    cum = _dot_ind_lhs(tri, lw)
    tot = jnp.where(d == 0, cum[CHUNK - 1:CHUNK, :], cum[0:1, :])
    e_neg = jnp.exp(-cum)
    e_end = jnp.exp(tot - cum)
    g_tot = jnp.exp(tot)
    kd = kd_ref[...]
    bd = bd_ref[...]
    rt = r_ref[...] * jnp.exp(cum)
    at = kk_ref[...] * jnp.exp(cum - lw)
    kt = kd * e_neg
    bt = bd * e_neg
    kh = kd * e_end
    bh = bd * e_end
    v = v_ref[...]
    n = RW_HEAD_DIM
    for h in range(RW_HEADS):
        sl = slice(h * n, (h + 1) * n)
        ar = jnp.concatenate([at[:, sl], rt[:, sl]], axis=0)
        gk = _dot_nt(ar, kt[:, sl])
        gb = _dot_nt(ar, bt[:, sl])
        a_ak = jnp.where(m_strict, gk[:CHUNK], 0.0)
        a_rk = jnp.where(m_incl, gk[CHUNK:], 0.0)
        a_ab = jnp.where(m_strict, gb[:CHUNK], 0.0)
        a_rb = jnp.where(m_incl, gb[CHUNK:], 0.0)
        tinv = eye - a_ab
        pw = a_ab
        for _ in range(5):
            pw = _dot(pw, pw)
            tinv = tinv + _dot(tinv, pw)
        vh = v[:, sl]
        av = _dot(a_ak, vh)
        y0 = _dot(a_rk, vh)
        wm = _dot(tinv, at[:, sl])
        u0 = -_dot(tinv, av)
        s = s_scr[h]
        pr = _dot_nt(jnp.concatenate([wm, rt[:, sl]], axis=0), s)
        u = u0 - pr[:CHUNK]
        y_ref[:, sl] = pr[CHUNK:] + y0 + _dot(a_rb, u)
        s_scr[h] = s * g_tot[:, sl] + _dot_tn(jnp.concatenate([vh, u], axis=0),
                                              jnp.concatenate([kh[:, sl], bh[:, sl]], axis=0))

    @pl.when(c == pl.num_programs(2) - 1)
    def _():
        sfin_ref[...] = s_scr[...]


def _rwscan(r, v, kk, lw, kd, bd, s0):
    bsz, t, w = r.shape
    nc = t // CHUNK
    cidx = lambda d, c: c + d * (nc - 1 - 2 * c)
    tok = pl.BlockSpec((None, CHUNK, w), lambda b, d, c: (b, cidx(d, c), 0))
    tok2 = pl.BlockSpec((None, None, CHUNK, w), lambda b, d, c: (d, b, cidx(d, c), 0))
    st = pl.BlockSpec((None, None, RW_HEADS, RW_HEAD_DIM, RW_HEAD_DIM),
                      lambda b, d, c: (b, d, 0, 0, 0))
    return pl.pallas_call(
        _rwscan_kernel,
        grid=(bsz, 2, nc),
        in_specs=[tok, tok, tok, tok2, tok2, tok2, st],
        out_specs=[tok2, st],
        out_shape=[jax.ShapeDtypeStruct((2, bsz, t, w), F32),
                   jax.ShapeDtypeStruct(s0.shape, F32)],
        scratch_shapes=[pltpu.VMEM((RW_HEADS, RW_HEAD_DIM, RW_HEAD_DIM), F32)],
        compiler_params=_params(3),
        name="rwscan",
    )(r, v, kk, lw, kd, bd, s0)


def _rwpost_kernel(y_ref, g_ref, gb_ref, gnw, gnb, hind, o_ref):
    y = y_ref[0] + y_ref[1]
    inv_n = 1.0 / RW_HEAD_DIM
    mean = _dot_ind_rhs(y, hind[...]) * inv_n
    yc = y - mean
    var = _dot_ind_rhs(yc * yc, hind[...]) * inv_n
    yn = yc * lax.rsqrt(var + RW_GN_EPS) * gnw[...] + gnb[...]
    o_ref[...] = yn * g_ref[...] + gb_ref[...]


def _rwpost(y2, g, gb, lp, hind, *, tm):
    _, bsz, t, w = y2.shape
    tok = pl.BlockSpec((None, tm, w), lambda b, j: (b, j, 0))
    tok2 = pl.BlockSpec((2, None, tm, w), lambda b, j: (0, b, j, 0))
    return pl.pallas_call(
        _rwpost_kernel,
        grid=(bsz, t // tm),
        in_specs=[tok2, tok, tok, _const_spec((1, w)), _const_spec((1, w)), _const_spec(hind.shape)],
        out_specs=tok,
        out_shape=jax.ShapeDtypeStruct((bsz, t, w), F32),
        compiler_params=_params(2),
        name="rwpost",
    )(y2, g, gb, lp["rw_gn_w"], lp["rw_gn_b"], hind)


def _glaprep_kernel(q_cur, q_prev, q_next, ald_ref, conv, up, bias, o_q, o_k, o_v, o_la):
    j = pl.program_id(1)
    x = q_cur[...]
    prev, nxt = _shifted(x, q_prev[...], q_next[...], j == 0, j == pl.num_programs(1) - 1)
    y = _silu(conv[0:1, :] * prev + conv[1:2, :] * x + conv[2:3, :] * nxt)
    o_q[...] = y[:, 0:GLA_KW] * (GLA_DK ** -0.5)
    o_k[...] = y[:, GLA_KW:2 * GLA_KW]
    o_v[...] = y[:, 2 * GLA_KW:]
    ald = ald_ref[...]
    for di in range(2):
        o_la[di] = _log_sigmoid(_dot(ald, up[di]) + bias[di:di + 1, :]) / GLA_TAU


def _glaprep(qkv, ald, lp, *, tm):
    bsz, t, fq = qkv.shape
    prev_spec, next_spec = _halo_specs(fq, tm, t)
    tok = lambda f: pl.BlockSpec((None, tm, f), lambda b, j: (b, j, 0))
    small = [lp["gla_conv"], lp["gla_alpha_up_pad"], lp["gla_alpha_bias"]]
    return pl.pallas_call(
        _glaprep_kernel,
        grid=(bsz, t // tm),
        in_specs=[tok(fq), prev_spec, next_spec, tok(LANES)] + [_const_spec(s.shape) for s in small],
        out_specs=[tok(GLA_KW), tok(GLA_KW), tok(GLA_VW),
                   pl.BlockSpec((2, None, tm, GLA_KW), lambda b, j: (0, b, j, 0))],
        out_shape=[jax.ShapeDtypeStruct((bsz, t, GLA_KW), F32),
                   jax.ShapeDtypeStruct((bsz, t, GLA_KW), F32),
                   jax.ShapeDtypeStruct((bsz, t, GLA_VW), F32),
                   jax.ShapeDtypeStruct((2, bsz, t, GLA_KW), F32)],
        compiler_params=_params(2),
        name="glaprep",
    )(qkv, qkv, qkv, ald, *small)


def _glascan_kernel(q_ref, k_ref, v_ref, la_ref, s0_ref, o_ref, sfin_ref, s_scr):
    d = pl.program_id(1)
    c = pl.program_id(2)

    @pl.when(c == 0)
    def _():
        s_scr[...] = s0_ref[...]

    _, _, m_incl, _ = _scan_masks(d)
    tri = jnp.where(m_incl, 1.0, 0.0).astype(BF16)
    la = la_ref[...]
    cum = _dot_ind_lhs(tri, la)
    tot = jnp.where(d == 0, cum[CHUNK - 1:CHUNK, :], cum[0:1, :])
    k = k_ref[...]
    q_dec = q_ref[...] * jnp.exp(cum)
    k_inv = k * jnp.exp(-cum)
    k_end = k * jnp.exp(tot - cum)
    dec = jnp.exp(tot)
    v = v_ref[...]
    for h in range(GLA_HEADS):
        sk = slice(h * GLA_DK, (h + 1) * GLA_DK)
        sv = slice(h * GLA_DV, (h + 1) * GLA_DV)
        scores = jnp.where(m_incl, _dot_nt(q_dec[:, sk], k_inv[:, sk]), 0.0)
        s = s_scr[h]
        o_ref[:, sv] = _dot(scores, v[:, sv]) + _dot_nt(q_dec[:, sk], s)
        s_scr[h] = s * dec[:, sk] + _dot_tn(v[:, sv], k_end[:, sk])

    @pl.when(c == pl.num_programs(2) - 1)
    def _():
        sfin_ref[...] = s_scr[...]


def _glascan(q, k, v, la, s0):
    bsz, t, _ = q.shape
    nc = t // CHUNK
    cidx = lambda d, c: c + d * (nc - 1 - 2 * c)
    tok = lambda f: pl.BlockSpec((None, CHUNK, f), lambda b, d, c: (b, cidx(d, c), 0))
    tok2 = lambda f: pl.BlockSpec((None, None, CHUNK, f), lambda b, d, c: (d, b, cidx(d, c), 0))
    st = pl.BlockSpec((None, None, GLA_HEADS, GLA_DV, GLA_DK), lambda b, d, c: (b, d, 0, 0, 0))
    return pl.pallas_call(
        _glascan_kernel,
        grid=(bsz, 2, nc),
        in_specs=[tok(GLA_KW), tok(GLA_KW), tok(GLA_VW), tok2(GLA_KW), st],
        out_specs=[tok2(GLA_VW), st],
        out_shape=[jax.ShapeDtypeStruct((2, bsz, t, GLA_VW), F32),
                   jax.ShapeDtypeStruct(s0.shape, F32)],
        scratch_shapes=[pltpu.VMEM((GLA_HEADS, GLA_DV, GLA_DK), F32)],
        compiler_params=_params(3),
        name="glascan",
    )(q, k, v, la, s0)


def _glapost_kernel(o_ref, gate_ref, nw, y_ref):
    o = o_ref[0] + o_ref[1]
    gate = gate_ref[...]
    for h in range(GLA_HEADS):
        sv = slice(h * GLA_DV, (h + 1) * GLA_DV)
        oh = o[:, sv]
        y_ref[:, sv] = _rms(oh, nw[...]) * _silu(gate[:, sv])


def _glapost(o2, gate, lp, *, tm):
    _, bsz, t, w = o2.shape
    tok = pl.BlockSpec((None, tm, w), lambda b, j: (b, j, 0))
    return pl.pallas_call(
        _glapost_kernel,
        grid=(bsz, t // tm),
        in_specs=[pl.BlockSpec((2, None, tm, w), lambda b, j: (0, b, j, 0)), tok,
                  _const_spec((1, GLA_DV))],
        out_specs=tok,
        out_shape=jax.ShapeDtypeStruct((bsz, t, w), F32),
        compiler_params=_params(2),
        name="glapost",
    )(o2, gate, lp["gla_norm_w"])


def _merge_kernel(x_ref, yrw_ref, ygla_ref, mg_ref, mod_ref, gains, w_rwo, w_glao, w_mo, w1, w2,
                  o_ref, *, x_tr, g_tr, a, k):
    d = x_ref.shape[-1] // (k if x_tr else 1)
    x = _load_tok(x_ref, x_tr, a, k, d)
    ygla = _load_tok(ygla_ref, g_tr, a, k, GLA_VW)
    mg = mg_ref[...]
    br = (_sigmoid(mg[:, :d]) * _dot(yrw_ref[...], w_rwo[...])
          + _sigmoid(mg[:, d:]) * _dot(ygla, w_glao[...]))
    m = _dot(br, w_mo[...])
    x1 = x + mod_ref[2:3, :] * _rms(m, gains[1:2, :])
    h2 = _rms(x1, gains[2:3, :]) * (1.0 + mod_ref[4:5, :]) + mod_ref[3:4, :]
    hid = jnp.maximum(_dot(h2, w1[...]), 0.0)
    f = _dot(hid * hid, w2[...])
    x2 = x1 + mod_ref[5:6, :] * _rms(f, gains[3:4, :])
    _store_tok(o_ref, x2, x_tr, a, k, d)


def _merge(x, yrw, ygla, mg, mods, mod_row, gains, ws, *, x_tr, g_tr, a, tm):
    bsz, t, d = x.shape
    k = tm // a
    kern = functools.partial(_merge_kernel, x_tr=x_tr, g_tr=g_tr, a=a, k=k)
    out = pl.pallas_call(
        kern,
        grid=(bsz, t // tm),
        in_specs=[_tok_spec(d, tm, x_tr, a), _tok_spec(RW_WIDTH, tm, False, a),
                  _tok_spec(GLA_VW, tm, g_tr, a), _tok_spec(2 * d, tm, False, a),
                  pl.BlockSpec((None, 6, d), lambda b, j: (mod_row(b), 0, 0)),
                  _const_spec(gains.shape)] + [_const_spec(w.shape) for w in ws],
        out_specs=_tok_spec(d, tm, x_tr, a),
        out_shape=_tok_shape(bsz, t, d, x_tr, a),
        compiler_params=_params(2),
        name="merge",
    )(_tok_view(x, x_tr, a), yrw, _tok_view(ygla, g_tr, a), mg, mods, gains, *ws)
    return _tok_unview(out, x_tr, t, d)


def _layer_params(l, p):
    d = p["w_in"].shape[1]
    w_in = p["w_in"][l]
    g0 = RW_COLS
    bf = lambda w: w.astype(BF16)
    row = lambda w: w.reshape(1, -1)
    lp = {
        "w_in_parts": [
            bf(w_in[:, :g0]),
            bf(w_in[:, g0:g0 + GLA_QKV_W]),
            bf(jnp.pad(w_in[:, g0 + GLA_QKV_W:g0 + GLA_QKV_W + 2 * GLA_GATE_RANK],
                       ((0, 0), (0, LANES - 2 * GLA_GATE_RANK)))),
            bf(w_in[:, g0 + GLA_QKV_W + 2 * GLA_GATE_RANK:g0 + GLA_QKV_W + 2 * GLA_GATE_RANK + GLA_VW]),
            bf(w_in[:, g0 + GLA_QKV_W + 2 * GLA_GATE_RANK + GLA_VW:]),
        ],
        "rw_mu": row(p["rw_mu"][l]),
        "rw_w0": p["rw_w0"][l],
        "rw_a0": p["rw_a0"][l],
        "rw_g_up": bf(p["rw_g_up"][l]),
        "rw_k_k": row(p["rw_k_k"][l]),
        "rw_k_a": row(p["rw_k_a"][l]),
        "rw_r_k": row(p["rw_r_k"][l]),
        "rw_gn_w": row(p["rw_gn_w"][l]),
        "rw_gn_b": row(p["rw_gn_b"][l]),
        "gla_conv": p["gla_conv"][l],
        "gla_alpha_bias": p["gla_alpha_bias"][l],
        "gla_norm_w": row(p["gla_norm_w"][l]),
        "merge_ws": [bf(p["rw_out"][l]), bf(p["gla_out"][l]), bf(p["merge_out"][l]),
                     bf(p["mlp_w1"][l]), bf(p["mlp_w2"][l])],
        "gains": jnp.stack([p["norm_mix_pre"][l], p["norm_mix_post"][l],
                            p["norm_ffn_pre"][l], p["norm_ffn_post"][l]]),
    }
    rank = p["rw_w_up"].shape[2]
    pad_dir = lambda w, r: jnp.stack([jnp.pad(w[di], ((di * r, LANES - (di + 1) * r), (0, 0)))
                                      for di in range(2)])
    lp["rw_w_up_pad"] = bf(pad_dir(p["rw_w_up"][l], rank))
    lp["rw_a_up_pad"] = bf(pad_dir(p["rw_a_up"][l], p["rw_a_up"].shape[2]))
    lp["gla_alpha_up_pad"] = bf(pad_dir(p["gla_alpha_up"][l], GLA_GATE_RANK))
    if l > 0:
        lp["rw_vres_down"] = bf(p["rw_vres_down"][l - 1])
        lp["rw_vres_up"] = bf(p["rw_vres_up"][l - 1])
        lp["rw_vres_bias"] = row(p["rw_vres_bias"][l - 1])
    del d
    return lp


def _head_indicator():
    h = jnp.arange(RW_WIDTH) // RW_HEAD_DIM
    return (h[:, None] == h[None, :]).astype(BF16)


def _mixer(x, mods, mod_row, vfirst, s_rw, s_gla, lp, hind, *, p_col, tm, need_out):
    bsz, t, d = x.shape
    if p_col is None:
        x_tr, g_tr, a = False, False, SUBLANES
    else:
        x_tr, g_tr = p_col, True
        a = (t // GRID_W) if p_col else GRID_W
    rw, gq, gald, gg, mg = _inproj(x, mods, mod_row, lp["gains"][0:1], lp["w_in_parts"],
                                   x_tr=x_tr, g_tr=g_tr, a=a, tm=tm)
    r, v, kk, g, gb, lw, kd, bd = _rwprep(rw, vfirst, lp, hind, vf_tr=x_tr, a=a, tm=tm)
    y2, s_rw_out = _rwscan(r, v, kk, lw, kd, bd, s_rw)
    q, k, gv, la = _glaprep(gq, gald, lp, tm=tm)
    o2, s_gla_out = _glascan(q, k, gv, la, s_gla)
    if not need_out:
        return None, v, s_rw_out, s_gla_out
    yrw = _rwpost(y2, g, gb, lp, hind, tm=tm)
    ygla = _glapost(o2, gg, lp, tm=tm)
    x_new = _merge(x, yrw, ygla, mg, mods, mod_row, lp["gains"], lp["merge_ws"],
                   x_tr=x_tr, g_tr=g_tr, a=a, tm=tm)
    return x_new, v, s_rw_out, s_gla_out


def kernel(x, c, ctx, c_ctx, w_in, rw_mu, rw_w0, rw_w_up, rw_a0, rw_a_up, rw_g_up, rw_k_k, rw_k_a,
           rw_r_k, rw_gn_w, rw_gn_b, rw_vres_down, rw_vres_up, rw_vres_bias, rw_out, gla_conv,
           gla_alpha_up, gla_alpha_bias, gla_norm_w, gla_out, merge_out, mlp_w1, mlp_w2, ada_w,
           ada_b, norm_mix_pre, norm_mix_post, norm_ffn_pre, norm_ffn_post):
    p = dict(w_in=w_in, rw_mu=rw_mu, rw_w0=rw_w0, rw_w_up=rw_w_up, rw_a0=rw_a0, rw_a_up=rw_a_up,
             rw_g_up=rw_g_up, rw_k_k=rw_k_k, rw_k_a=rw_k_a, rw_r_k=rw_r_k.reshape(rw_r_k.shape[0], -1),
             rw_gn_w=rw_gn_w, rw_gn_b=rw_gn_b, rw_vres_down=rw_vres_down, rw_vres_up=rw_vres_up,
             rw_vres_bias=rw_vres_bias, rw_out=rw_out, gla_conv=gla_conv, gla_alpha_up=gla_alpha_up,
             gla_alpha_bias=gla_alpha_bias, gla_norm_w=gla_norm_w, gla_out=gla_out,
             merge_out=merge_out, mlp_w1=mlp_w1, mlp_w2=mlp_w2, norm_mix_pre=norm_mix_pre,
             norm_mix_post=norm_mix_post, norm_ffn_pre=norm_ffn_pre, norm_ffn_post=norm_ffn_post)
    bsz, t, d = x.shape
    depth = w_in.shape[0]
    assert bsz < SUBLANES and d % LANES == 0
    assert t % (GRID_W * SUBLANES) == 0 and ctx.shape[1] % CHUNK == 0
    tm = min(TOKEN_TILE, t)
    tm_ctx = min(TOKEN_TILE, ctx.shape[1])
    assert t % tm == 0 and ctx.shape[1] % tm_ctx == 0 and tm % GRID_W == 0

    cc = jnp.concatenate([c, c_ctx[None, :], jnp.zeros((SUBLANES - 1 - bsz, d), F32)], axis=0)
    mods = _ada_mods(cc, ada_w, ada_b).reshape(depth, SUBLANES, 6, d)
    hind = _head_indicator()
    lat_row = lambda b: b
    ctx_row = lambda b: bsz

    x_lat, x_ctx = x, ctx
    vf_lat = vf_ctx = None
    for l in range(depth):
        last = l == depth - 1
        lp = _layer_params(l, p)
        z_rw = jnp.zeros((bsz, 2, RW_HEADS, RW_HEAD_DIM, RW_HEAD_DIM), F32)
        z_gla = jnp.zeros((bsz, 2, GLA_HEADS, GLA_DV, GLA_DK), F32)
        x_ctx_new, v_ctx, s_rw, s_gla = _mixer(
            x_ctx, mods[l], ctx_row, vf_ctx, z_rw, z_gla, lp, hind,
            p_col=None, tm=tm_ctx, need_out=not last)
        x_lat, v_lat, _, _ = _mixer(
            x_lat, mods[l], lat_row, vf_lat, s_rw, s_gla, lp, hind,
            p_col=(l % 2 == 1), tm=tm, need_out=True)
        if l == 0:
            vf_lat, vf_ctx = v_lat, v_ctx
        if not last:
            x_ctx = x_ctx_new
    return x_lat
```

```python
import functools

import jax
import jax.numpy as jnp
from jax import lax
from jax.experimental import pallas as pl
from jax.experimental.pallas import tpu as pltpu

F32 = jnp.float32
BF16 = jnp.bfloat16

GRID_W = 64
RMS_EPS = 1e-6
RW_HEADS = 8
RW_HEAD_DIM = 64
RW_WIDTH = RW_HEADS * RW_HEAD_DIM
RW_GN_EPS = 64e-5
RW_COLS = 1920
GLA_HEADS = 4
GLA_DK = 64
GLA_DV = 128
GLA_KW = GLA_HEADS * GLA_DK
GLA_VW = GLA_HEADS * GLA_DV
GLA_QKV_W = 2 * GLA_KW + GLA_VW
GLA_GATE_RANK = 16
GLA_TAU = 16.0
CHUNK = 64
LANES = 128
SUBLANES = 8
VMEM_LIMIT = 56 * 1024 * 1024
TOKEN_TILE = 256


def _dot(a, b):
    return jnp.dot(a.astype(BF16), b.astype(BF16), preferred_element_type=F32)


def _dot_nt(a, b):
    return lax.dot_general(a.astype(BF16), b.astype(BF16), (((1,), (1,)), ((), ())),
                           preferred_element_type=F32)


def _dot_tn(a, b):
    return lax.dot_general(a.astype(BF16), b.astype(BF16), (((0,), (0,)), ((), ())),
                           preferred_element_type=F32)


def _split2(x):
    hi = x.astype(BF16)
    lo = (x - hi.astype(F32)).astype(BF16)
    return hi, lo


def _dot_ind_rhs(x, ind):
    hi, lo = _split2(x)
    return (jnp.dot(hi, ind, preferred_element_type=F32)
            + jnp.dot(lo, ind, preferred_element_type=F32))


def _dot_ind_lhs(ind, x):
    hi, lo = _split2(x)
    return (jnp.dot(ind, hi, preferred_element_type=F32)
            + jnp.dot(ind, lo, preferred_element_type=F32))


def _sigmoid(x):
    return jax.nn.sigmoid(x)


def _silu(x):
    return x * jax.nn.sigmoid(x)


def _softplus(z):
    return jnp.maximum(z, 0.0) + jnp.log1p(jnp.exp(-jnp.abs(z)))


def _log_sigmoid(z):
    return -_softplus(-z)


def _rms(x, gain):
    return x * lax.rsqrt(jnp.mean(x * x, axis=-1, keepdims=True) + RMS_EPS) * gain


def _load_tok(ref, transposed, a, k, f):
    if not transposed:
        return ref[...]
    return jnp.concatenate([ref[:, i * f:(i + 1) * f] for i in range(k)], axis=0)


def _store_tok(ref, val, transposed, a, k, f):
    if not transposed:
        ref[...] = val
    else:
        for i in range(k):
            ref[:, i * f:(i + 1) * f] = val[i * a:(i + 1) * a, :]


def _tok_view(arr, transposed, a):
    if not transposed:
        return arr
    b, t, f = arr.shape
    return arr.reshape(b, a, (t // a) * f)


def _tok_unview(arr, transposed, t, f):
    if not transposed:
        return arr
    return arr.reshape(arr.shape[0], t, f)


def _tok_spec(f, tm, transposed, a):
    if not transposed:
        return pl.BlockSpec((None, tm, f), lambda b, j: (b, j, 0))
    return pl.BlockSpec((None, a, (tm // a) * f), lambda b, j: (b, 0, j))


def _tok_shape(bsz, t, f, transposed, a):
    if not transposed:
        return jax.ShapeDtypeStruct((bsz, t, f), F32)
    return jax.ShapeDtypeStruct((bsz, a, (t // a) * f), F32)


def _const_spec(shape):
    nd = len(shape)
    return pl.BlockSpec(shape, lambda *_: (0,) * nd, pipeline_mode=pl.Buffered(1))


def _params(ndim):
    return pltpu.CompilerParams(dimension_semantics=("arbitrary",) * ndim,
                                vmem_limit_bytes=VMEM_LIMIT)


def _halo_specs(f, tm, t):
    sub = tm // SUBLANES
    last = t // SUBLANES - 1
    prev = pl.BlockSpec((None, SUBLANES, f), lambda b, j: (b, jnp.maximum(j * sub - 1, 0), 0))
    nxt = pl.BlockSpec((None, SUBLANES, f), lambda b, j: (b, jnp.minimum((j + 1) * sub, last), 0))
    return prev, nxt


def _shifted(cur, prev8, next8, first, last):
    tm = cur.shape[0]
    row = lax.broadcasted_iota(jnp.int32, cur.shape, 0)
    p_row = jnp.where(first, 0.0, prev8[SUBLANES - 1:SUBLANES, :])
    n_row = jnp.where(last, 0.0, next8[0:1, :])
    prev = jnp.where(row == 0, p_row, pltpu.roll(cur, 1, 0))
    nxt = jnp.where(row == tm - 1, n_row, pltpu.roll(cur, tm - 1, 0))
    return prev, nxt


def _ada_kernel(c_ref, w_ref, b_ref, o_ref):
    cc = c_ref[...]
    o_ref[...] = _dot(_silu(cc), w_ref[...]) + b_ref[...]


def _ada_mods(cc, ada_w, ada_b):
    nl, d, n6 = ada_w.shape
    tn = d
    return pl.pallas_call(
        _ada_kernel,
        grid=(nl, n6 // tn),
        in_specs=[pl.BlockSpec((SUBLANES, d), lambda l, n: (0, 0)),
                  pl.BlockSpec((None, d, tn), lambda l, n: (l, 0, n)),
                  pl.BlockSpec((None, 1, tn), lambda l, n: (l, 0, n))],
        out_specs=pl.BlockSpec((None, SUBLANES, tn), lambda l, n: (l, 0, n)),
        out_shape=jax.ShapeDtypeStruct((nl, SUBLANES, n6), F32),
        compiler_params=_params(2),
        name="ada",
    )(cc, ada_w, ada_b.reshape(nl, 1, n6))


def _inproj_kernel(x_ref, mod_ref, gain_ref, w_rw, w_q, w_ald, w_gg, w_mg,
                   o_rw, o_q, o_ald, o_gg, o_mg, *, x_tr, g_tr, a, k):
    d = gain_ref.shape[-1]
    x = _load_tok(x_ref, x_tr, a, k, d)
    h = _rms(x, gain_ref[...]) * (1.0 + mod_ref[1:2, :]) + mod_ref[0:1, :]
    hb = h.astype(BF16)
    o_rw[...] = jnp.dot(hb, w_rw[...], preferred_element_type=F32)
    o_mg[...] = jnp.dot(hb, w_mg[...], preferred_element_type=F32)
    _store_tok(o_q, jnp.dot(hb, w_q[...], preferred_element_type=F32), g_tr, a, k, GLA_QKV_W)
    _store_tok(o_ald, jnp.dot(hb, w_ald[...], preferred_element_type=F32), g_tr, a, k, LANES)
    _store_tok(o_gg, jnp.dot(hb, w_gg[...], preferred_element_type=F32), g_tr, a, k, GLA_VW)


def _inproj(x, mods, mod_row, gain, ws, *, x_tr, g_tr, a, tm):
    bsz, t, d = x.shape
    k = tm // a
    kern = functools.partial(_inproj_kernel, x_tr=x_tr, g_tr=g_tr, a=a, k=k)
    widths = (RW_COLS, GLA_QKV_W, LANES, GLA_VW, 2 * d)
    out_tr = (False, g_tr, g_tr, g_tr, False)
    outs = pl.pallas_call(
        kern,
        grid=(bsz, t // tm),
        in_specs=[_tok_spec(d, tm, x_tr, a),
                  pl.BlockSpec((None, 6, d), lambda b, j: (mod_row(b), 0, 0)),
                  _const_spec((1, d))] + [_const_spec(w.shape) for w in ws],
        out_specs=[_tok_spec(f, tm, tr, a) for f, tr in zip(widths, out_tr)],
        out_shape=[_tok_shape(bsz, t, f, tr, a) for f, tr in zip(widths, out_tr)],
        compiler_params=_params(2),
        name="inproj",
    )(_tok_view(x, x_tr, a), mods, gain, *ws)
    return [_tok_unview(o, tr, t, f) for o, f, tr in zip(outs, widths, out_tr)]


def _rwprep_kernel(*refs, has_vres, vf_tr, a, k):
    it = iter(refs)
    f_cur, f_prev, f_next = next(it), next(it), next(it)
    vf_ref = next(it) if has_vres else None
    mu, w0, w_up, a0, a_up, g_up, k_k, k_a, r_k = (next(it) for _ in range(9))
    if has_vres:
        vdown, vup, vbias = next(it), next(it), next(it)
    hind = next(it)
    o_r, o_v, o_kk, o_g, o_gb, o_lw, o_kd, o_bd = (next(it) for _ in range(8))

    j = pl.program_id(1)
    first = j == 0
    last = j == pl.num_programs(1) - 1
    f = f_cur[...]
    prev, nxt = _shifted(f, f_prev[...], f_next[...], first, last)
    fs = f + mu[...] * (0.5 * (prev + nxt) - f)
    w = RW_WIDTH
    r = fs[:, 0:w]
    kx = fs[:, w:2 * w]
    v = fs[:, 2 * w:3 * w]
    wd = fs[:, 3 * w:3 * w + LANES]
    ad = fs[:, 3 * w + LANES:3 * w + 2 * LANES]
    gd = fs[:, 3 * w + 2 * LANES:3 * w + 3 * LANES]
    if has_vres:
        vf = _load_tok(vf_ref, vf_tr, a, k, w)
        mix = _sigmoid(vbias[...] + _dot(_dot(v, vdown[...]), vup[...]))
        v = v + (vf - v) * mix
    kk = kx * k_k[...]
    kk = kk * lax.rsqrt(_dot_ind_rhs(kk * kk, hind[...]) + 1e-12)
    twd = jnp.tanh(wd)
    ksum = None
    for di in range(2):
        wlog = -_softplus(-(w0[di:di + 1, :] + _dot(twd, w_up[di]))) - 0.5
        o_lw[di] = -jnp.exp(wlog)
        lr = _sigmoid(a0[di:di + 1, :] + _dot(ad, a_up[di]))
        kd = kx * (1.0 + (lr - 1.0) * k_a[...])
        o_kd[di] = kd
        o_bd[di] = lr * kk
        ksum = kd if ksum is None else ksum + kd
    g = _dot(_sigmoid(gd), g_up[...])
    bonus = _dot_ind_rhs(r * ksum * r_k[...], hind[...]) * v
    o_r[...] = r
    o_v[...] = v
    o_kk[...] = kk
    o_g[...] = g
    o_gb[...] = bonus * g


def _rwprep(f, vfirst, lp, hind, *, vf_tr, a, tm):
    bsz, t, fc = f.shape
    w = RW_WIDTH
    has_vres = vfirst is not None
    k = tm // a
    kern = functools.partial(_rwprep_kernel, has_vres=has_vres, vf_tr=vf_tr, a=a, k=k)
    prev_spec, next_spec = _halo_specs(fc, tm, t)
    args = [f, f, f]
    in_specs = [_tok_spec(fc, tm, False, a), prev_spec, next_spec]
    if has_vres:
        args.append(_tok_view(vfirst, vf_tr, a))
        in_specs.append(_tok_spec(w, tm, vf_tr, a))
    small = [lp["rw_mu"], lp["rw_w0"], lp["rw_w_up_pad"], lp["rw_a0"], lp["rw_a_up_pad"],
             lp["rw_g_up"], lp["rw_k_k"], lp["rw_k_a"], lp["rw_r_k"]]
    if has_vres:
        small += [lp["rw_vres_down"], lp["rw_vres_up"], lp["rw_vres_bias"]]
    small.append(hind)
    args += small
    in_specs += [_const_spec(s.shape) for s in small]
    tok = pl.BlockSpec((None, tm, w), lambda b, j: (b, j, 0))
    tok2 = pl.BlockSpec((2, None, tm, w), lambda b, j: (0, b, j, 0))
    s1 = jax.ShapeDtypeStruct((bsz, t, w), F32)
    s2 = jax.ShapeDtypeStruct((2, bsz, t, w), F32)
    return pl.pallas_call(
        kern,
        grid=(bsz, t // tm),
        in_specs=in_specs,
        out_specs=[tok] * 5 + [tok2] * 3,
        out_shape=[s1] * 5 + [s2] * 3,
        compiler_params=_params(2),
        name="rwprep",
    )(*args)


def _before(row, col, reverse, inclusive):
    if reverse:
        return (col >= row) if inclusive else (col > row)
    return (col <= row) if inclusive else (col < row)


def _cum_parts(lw, reverse):
    row = lax.broadcasted_iota(jnp.int32, (CHUNK, CHUNK), 0)
    col = lax.broadcasted_iota(jnp.int32, (CHUNK, CHUNK), 1)
    tri = jnp.where(_before(row, col, reverse, True), 1.0, 0.0).astype(BF16)
    cum = _dot_ind_lhs(tri, lw)
    tot = cum[0:1, :] if reverse else cum[CHUNK - 1:CHUNK, :]
    return cum, tot


def _rwscan_kernel(rf, vf, kkf, lwf, kdf, bdf, rb, vb, kkb, lwb, kdb, bdb, s0_ref,
                   yf_ref, yb_ref, sfin_ref, s_scr):
    c = pl.program_id(1)

    @pl.when(c == 0)
    def _():
        s_scr[...] = s0_ref[...]

    n = RW_HEAD_DIM
    c2 = 2 * CHUNK
    row2 = lax.broadcasted_iota(jnp.int32, (c2, c2), 0)
    col2 = lax.broadcasted_iota(jnp.int32, (c2, c2), 1)
    rp = row2 % CHUNK
    cp = col2 % CHUNK
    bottom = row2 // CHUNK
    row = lax.broadcasted_iota(jnp.int32, (CHUNK, CHUNK), 0)
    col = lax.broadcasted_iota(jnp.int32, (CHUNK, CHUNK), 1)
    eye = jnp.where(row == col, 1.0, 0.0)
    zeros = jnp.zeros((CHUNK, n), BF16)

    chains = []
    for di, (r_ref, v_ref, kk_ref, lw_ref, kd_ref, bd_ref, y_ref) in enumerate(
            ((rf, vf, kkf, lwf, kdf, bdf, yf_ref), (rb, vb, kkb, lwb, kdb, bdb, yb_ref))):
        reverse = di == 1
        lw = lw_ref[...]
        cum, tot = _cum_parts(lw, reverse)
        e_neg = jnp.exp(-cum)
        e_end = jnp.exp(tot - cum)
        kd = kd_ref[...]
        bd = bd_ref[...]
        ops = dict(
            rt=(r_ref[...] * jnp.exp(cum)).astype(BF16),
            at=(kk_ref[...] * jnp.exp(cum - lw)).astype(BF16),
            kt=(kd * e_neg).astype(BF16),
            bt=(bd * e_neg).astype(BF16),
            kh=(kd * e_end).astype(BF16),
            bh=(bd * e_end).astype(BF16),
            v=v_ref[...].astype(BF16),
            g_tot=jnp.exp(tot),
        )
        keep = (cp > rp - bottom) if reverse else (cp < rp + bottom)
        for h in range(RW_HEADS):
            sl = slice(h * n, (h + 1) * n)
            ch = {k: val[:, sl] for k, val in ops.items()}
            ch.update(di=di, h=h, sl=sl, keep=keep, y_ref=y_ref)
            chains.append(ch)

    for ch in chains:
        g = _dot_nt(jnp.concatenate([ch["at"], ch["rt"]], axis=0),
                    jnp.concatenate([ch["bt"], ch["kt"]], axis=0))
        g = jnp.where(ch["keep"], g, 0.0).astype(BF16)
        ch["g_top"], ch["g_bot"] = g[:CHUNK], g[CHUNK:]
        ch["l"] = g[:CHUNK, :CHUNK]
    for ch in chains:
        ch["p"] = _dot(ch["l"], ch["l"])
        ch["av"] = _dot(ch["g_top"], jnp.concatenate([zeros, ch["v"]], axis=0))
        ch["kv"] = _dot_tn(ch["v"], ch["kh"])
        ch["x"] = eye - ch["l"].astype(F32)
    for _ in range(4):
        for ch in chains:
            z = _dot(jnp.concatenate([ch["x"], ch["p"]], axis=0), ch["p"])
            ch["x"] = ch["x"] + z[:CHUNK]
            ch["p"] = z[CHUNK:]
    for ch in chains:
        ch["x"] = ch["x"] + _dot(ch["x"], ch["p"])
    for ch in chains:
        ch["wm"] = _dot(ch["x"], ch["at"]).astype(BF16)
        ch["u0"] = -_dot(ch["x"], ch["av"])
    for ch in chains:
        ch["u0t"] = ch["u0"].T
    for ch in chains:
        s = s_scr[ch["di"], ch["h"]]
        ch["s"] = s
        sb = s.astype(BF16)
        pr = _dot_nt(jnp.concatenate([ch["wm"], ch["rt"]], axis=0), sb)
        ch["u"] = ch["u0"] - pr[:CHUNK]
        ch["rs"] = pr[CHUNK:]
        ch["ut"] = ch["u0t"] - _dot_nt(sb, ch["wm"])
    for ch in chains:
        ch["y_ref"][:, ch["sl"]] = ch["rs"] + _dot(
            ch["g_bot"], jnp.concatenate([ch["u"].astype(BF16), ch["v"]], axis=0))
        s_scr[ch["di"], ch["h"]] = ch["s"] * ch["g_tot"] + ch["kv"] + _dot(ch["ut"], ch["bh"])

    @pl.when(c == pl.num_programs(1) - 1)
    def _():
        sfin_ref[...] = s_scr[...]


def _rwscan(r, v, kk, lw, kd, bd, s0):
    bsz, t, w = r.shape
    nc = t // CHUNK
    tok_f = pl.BlockSpec((None, CHUNK, w), lambda b, c: (b, c, 0))
    tok_b = pl.BlockSpec((None, CHUNK, w), lambda b, c: (b, nc - 1 - c, 0))
    dir_f = pl.BlockSpec((None, None, CHUNK, w), lambda b, c: (0, b, c, 0))
    dir_b = pl.BlockSpec((None, None, CHUNK, w), lambda b, c: (1, b, nc - 1 - c, 0))
    st = pl.BlockSpec((None, 2, RW_HEADS, RW_HEAD_DIM, RW_HEAD_DIM), lambda b, c: (b, 0, 0, 0, 0))
    y_shape = jax.ShapeDtypeStruct((bsz, t, w), F32)
    yf, yb, sfin = pl.pallas_call(
        _rwscan_kernel,
        grid=(bsz, nc),
        in_specs=[tok_f, tok_f, tok_f, dir_f, dir_f, dir_f,
                  tok_b, tok_b, tok_b, dir_b, dir_b, dir_b, st],
        out_specs=[tok_f, tok_b, st],
        out_shape=[y_shape, y_shape, jax.ShapeDtypeStruct(s0.shape, F32)],
        scratch_shapes=[pltpu.VMEM((2, RW_HEADS, RW_HEAD_DIM, RW_HEAD_DIM), F32)],
        compiler_params=_params(2),
        name="rwscan",
    )(r, v, kk, lw, kd, bd, r, v, kk, lw, kd, bd, s0)
    return yf, yb, sfin


def _rwpost_kernel(yf_ref, yb_ref, g_ref, gb_ref, gnw, gnb, hind, o_ref):
    y = yf_ref[...] + yb_ref[...]
    inv_n = 1.0 / RW_HEAD_DIM
    mean = _dot_ind_rhs(y, hind[...]) * inv_n
    yc = y - mean
    var = _dot_ind_rhs(yc * yc, hind[...]) * inv_n
    yn = yc * lax.rsqrt(var + RW_GN_EPS) * gnw[...] + gnb[...]
    o_ref[...] = yn * g_ref[...] + gb_ref[...]


def _rwpost(yf, yb, g, gb, lp, hind, *, tm):
    bsz, t, w = yf.shape
    tok = pl.BlockSpec((None, tm, w), lambda b, j: (b, j, 0))
    return pl.pallas_call(
        _rwpost_kernel,
        grid=(bsz, t // tm),
        in_specs=[tok, tok, tok, tok, _const_spec((1, w)), _const_spec((1, w)),
                  _const_spec(hind.shape)],
        out_specs=tok,
        out_shape=jax.ShapeDtypeStruct((bsz, t, w), F32),
        compiler_params=_params(2),
        name="rwpost",
    )(yf, yb, g, gb, lp["rw_gn_w"], lp["rw_gn_b"], hind)


def _glaprep_kernel(q_cur, q_prev, q_next, ald_ref, conv, up, bias, o_q, o_k, o_v, o_la):
    j = pl.program_id(1)
    x = q_cur[...]
    prev, nxt = _shifted(x, q_prev[...], q_next[...], j == 0, j == pl.num_programs(1) - 1)
    y = _silu(conv[0:1, :] * prev + conv[1:2, :] * x + conv[2:3, :] * nxt)
    o_q[...] = y[:, 0:GLA_KW] * (GLA_DK ** -0.5)
    o_k[...] = y[:, GLA_KW:2 * GLA_KW]
    o_v[...] = y[:, 2 * GLA_KW:]
    ald = ald_ref[...]
    for di in range(2):
        o_la[di] = _log_sigmoid(_dot(ald, up[di]) + bias[di:di + 1, :]) / GLA_TAU


def _glaprep(qkv, ald, lp, *, tm):
    bsz, t, fq = qkv.shape
    prev_spec, next_spec = _halo_specs(fq, tm, t)
    tok = lambda f: pl.BlockSpec((None, tm, f), lambda b, j: (b, j, 0))
    small = [lp["gla_conv"], lp["gla_alpha_up_pad"], lp["gla_alpha_bias"]]
    return pl.pallas_call(
        _glaprep_kernel,
        grid=(bsz, t // tm),
        in_specs=[tok(fq), prev_spec, next_spec, tok(LANES)] + [_const_spec(s.shape) for s in small],
        out_specs=[tok(GLA_KW), tok(GLA_KW), tok(GLA_VW),
                   pl.BlockSpec((2, None, tm, GLA_KW), lambda b, j: (0, b, j, 0))],
        out_shape=[jax.ShapeDtypeStruct((bsz, t, GLA_KW), F32),
                   jax.ShapeDtypeStruct((bsz, t, GLA_KW), F32),
                   jax.ShapeDtypeStruct((bsz, t, GLA_VW), F32),
                   jax.ShapeDtypeStruct((2, bsz, t, GLA_KW), F32)],
        compiler_params=_params(2),
        name="glaprep",
    )(qkv, qkv, qkv, ald, *small)


def _glascan_kernel(qf, kf, vf, laf, qb, kb, vb, lab, s0_ref, of_ref, ob_ref, sfin_ref, s_scr):
    c = pl.program_id(1)

    @pl.when(c == 0)
    def _():
        s_scr[...] = s0_ref[...]

    row = lax.broadcasted_iota(jnp.int32, (CHUNK, CHUNK), 0)
    col = lax.broadcasted_iota(jnp.int32, (CHUNK, CHUNK), 1)
    chains = []
    for di, (q_ref, k_ref, v_ref, la_ref, o_ref) in enumerate(
            ((qf, kf, vf, laf, of_ref), (qb, kb, vb, lab, ob_ref))):
        reverse = di == 1
        cum, tot = _cum_parts(la_ref[...], reverse)
        k = k_ref[...]
        q_dec = (q_ref[...] * jnp.exp(cum)).astype(BF16)
        k_inv = (k * jnp.exp(-cum)).astype(BF16)
        k_end = (k * jnp.exp(tot - cum)).astype(BF16)
        dec = jnp.exp(tot)
        v = v_ref[...].astype(BF16)
        keep = _before(row, col, reverse, True)
        for h in range(GLA_HEADS):
            sk = slice(h * GLA_DK, (h + 1) * GLA_DK)
            sv = slice(h * GLA_DV, (h + 1) * GLA_DV)
            chains.append(dict(di=di, h=h, sv=sv, keep=keep, o_ref=o_ref, q=q_dec[:, sk],
                               ki=k_inv[:, sk], ke=k_end[:, sk], dec=dec[:, sk], v=v[:, sv]))
    for ch in chains:
        ch["sc"] = jnp.where(ch["keep"], _dot_nt(ch["q"], ch["ki"]), 0.0)
        ch["kv"] = _dot_tn(ch["v"], ch["ke"])
        ch["s"] = s_scr[ch["di"], ch["h"]]
        ch["inter"] = _dot_nt(ch["q"], ch["s"])
    for ch in chains:
        ch["o_ref"][:, ch["sv"]] = _dot(ch["sc"], ch["v"]) + ch["inter"]
        s_scr[ch["di"], ch["h"]] = ch["s"] * ch["dec"] + ch["kv"]

    @pl.when(c == pl.num_programs(1) - 1)
    def _():
        sfin_ref[...] = s_scr[...]


def _glascan(q, k, v, la, s0):
    bsz, t, _ = q.shape
    nc = t // CHUNK
    tok_f = lambda f: pl.BlockSpec((None, CHUNK, f), lambda b, c: (b, c, 0))
    tok_b = lambda f: pl.BlockSpec((None, CHUNK, f), lambda b, c: (b, nc - 1 - c, 0))
    dir_f = pl.BlockSpec((None, None, CHUNK, GLA_KW), lambda b, c: (0, b, c, 0))
    dir_b = pl.BlockSpec((None, None, CHUNK, GLA_KW), lambda b, c: (1, b, nc - 1 - c, 0))
    st = pl.BlockSpec((None, 2, GLA_HEADS, GLA_DV, GLA_DK), lambda b, c: (b, 0, 0, 0, 0))
    o_shape = jax.ShapeDtypeStruct((bsz, t, GLA_VW), F32)
    return pl.pallas_call(
        _glascan_kernel,
        grid=(bsz, nc),
        in_specs=[tok_f(GLA_KW), tok_f(GLA_KW), tok_f(GLA_VW), dir_f,
                  tok_b(GLA_KW), tok_b(GLA_KW), tok_b(GLA_VW), dir_b, st],
        out_specs=[tok_f(GLA_VW), tok_b(GLA_VW), st],
        out_shape=[o_shape, o_shape, jax.ShapeDtypeStruct(s0.shape, F32)],
        scratch_shapes=[pltpu.VMEM((2, GLA_HEADS, GLA_DV, GLA_DK), F32)],
        compiler_params=_params(2),
        name="glascan",
    )(q, k, v, la, q, k, v, la, s0)


def _glapost_kernel(of_ref, ob_ref, gate_ref, nw, y_ref):
    o = of_ref[...] + ob_ref[...]
    gate = gate_ref[...]
    for h in range(GLA_HEADS):
        sv = slice(h * GLA_DV, (h + 1) * GLA_DV)
        oh = o[:, sv]
        y_ref[:, sv] = _rms(oh, nw[...]) * _silu(gate[:, sv])


def _glapost(of, ob, gate, lp, *, tm):
    bsz, t, w = of.shape
    tok = pl.BlockSpec((None, tm, w), lambda b, j: (b, j, 0))
    return pl.pallas_call(
        _glapost_kernel,
        grid=(bsz, t // tm),
        in_specs=[tok, tok, tok, _const_spec((1, GLA_DV))],
        out_specs=tok,
        out_shape=jax.ShapeDtypeStruct((bsz, t, w), F32),
        compiler_params=_params(2),
        name="glapost",
    )(of, ob, gate, lp["gla_norm_w"])


def _merge_kernel(x_ref, yrw_ref, ygla_ref, mg_ref, mod_ref, gains, w_rwo, w_glao, w_mo, w1, w2,
                  o_ref, *, x_tr, g_tr, a, k):
    d = x_ref.shape[-1] // (k if x_tr else 1)
    x = _load_tok(x_ref, x_tr, a, k, d)
    ygla = _load_tok(ygla_ref, g_tr, a, k, GLA_VW)
    mg = mg_ref[...]
    br = (_sigmoid(mg[:, :d]) * _dot(yrw_ref[...], w_rwo[...])
          + _sigmoid(mg[:, d:]) * _dot(ygla, w_glao[...]))
    m = _dot(br, w_mo[...])
    x1 = x + mod_ref[2:3, :] * _rms(m, gains[1:2, :])
    h2 = _rms(x1, gains[2:3, :]) * (1.0 + mod_ref[4:5, :]) + mod_ref[3:4, :]
    hid = jnp.maximum(_dot(h2, w1[...]), 0.0)
    f = _dot(hid * hid, w2[...])
    x2 = x1 + mod_ref[5:6, :] * _rms(f, gains[3:4, :])
    _store_tok(o_ref, x2, x_tr, a, k, d)


def _merge(x, yrw, ygla, mg, mods, mod_row, gains, ws, *, x_tr, g_tr, a, tm):
    bsz, t, d = x.shape
    k = tm // a
    kern = functools.partial(_merge_kernel, x_tr=x_tr, g_tr=g_tr, a=a, k=k)
    out = pl.pallas_call(
        kern,
        grid=(bsz, t // tm),
        in_specs=[_tok_spec(d, tm, x_tr, a), _tok_spec(RW_WIDTH, tm, False, a),
                  _tok_spec(GLA_VW, tm, g_tr, a), _tok_spec(2 * d, tm, False, a),
                  pl.BlockSpec((None, 6, d), lambda b, j: (mod_row(b), 0, 0)),
                  _const_spec(gains.shape)] + [_const_spec(w.shape) for w in ws],
        out_specs=_tok_spec(d, tm, x_tr, a),
        out_shape=_tok_shape(bsz, t, d, x_tr, a),
        compiler_params=_params(2),
        name="merge",
    )(_tok_view(x, x_tr, a), yrw, _tok_view(ygla, g_tr, a), mg, mods, gains, *ws)
    return _tok_unview(out, x_tr, t, d)


def _layer_params(l, p):
    d = p["w_in"].shape[1]
    w_in = p["w_in"][l]
    g0 = RW_COLS
    bf = lambda w: w.astype(BF16)
    row = lambda w: w.reshape(1, -1)
    lp = {
        "w_in_parts": [
            bf(w_in[:, :g0]),
            bf(w_in[:, g0:g0 + GLA_QKV_W]),
            bf(jnp.pad(w_in[:, g0 + GLA_QKV_W:g0 + GLA_QKV_W + 2 * GLA_GATE_RANK],
                       ((0, 0), (0, LANES - 2 * GLA_GATE_RANK)))),
            bf(w_in[:, g0 + GLA_QKV_W + 2 * GLA_GATE_RANK:g0 + GLA_QKV_W + 2 * GLA_GATE_RANK + GLA_VW]),
            bf(w_in[:, g0 + GLA_QKV_W + 2 * GLA_GATE_RANK + GLA_VW:]),
        ],
        "rw_mu": row(p["rw_mu"][l]),
        "rw_w0": p["rw_w0"][l],
        "rw_a0": p["rw_a0"][l],
        "rw_g_up": bf(p["rw_g_up"][l]),
        "rw_k_k": row(p["rw_k_k"][l]),
        "rw_k_a": row(p["rw_k_a"][l]),
        "rw_r_k": row(p["rw_r_k"][l]),
        "rw_gn_w": row(p["rw_gn_w"][l]),
        "rw_gn_b": row(p["rw_gn_b"][l]),
        "gla_conv": p["gla_conv"][l],
        "gla_alpha_bias": p["gla_alpha_bias"][l],
        "gla_norm_w": row(p["gla_norm_w"][l]),
        "merge_ws": [bf(p["rw_out"][l]), bf(p["gla_out"][l]), bf(p["merge_out"][l]),
                     bf(p["mlp_w1"][l]), bf(p["mlp_w2"][l])],
        "gains": jnp.stack([p["norm_mix_pre"][l], p["norm_mix_post"][l],
                            p["norm_ffn_pre"][l], p["norm_ffn_post"][l]]),
    }
    rank = p["rw_w_up"].shape[2]
    pad_dir = lambda w, r: jnp.stack([jnp.pad(w[di], ((di * r, LANES - (di + 1) * r), (0, 0)))
                                      for di in range(2)])
    lp["rw_w_up_pad"] = bf(pad_dir(p["rw_w_up"][l], rank))
    lp["rw_a_up_pad"] = bf(pad_dir(p["rw_a_up"][l], p["rw_a_up"].shape[2]))
    lp["gla_alpha_up_pad"] = bf(pad_dir(p["gla_alpha_up"][l], GLA_GATE_RANK))
    if l > 0:
        lp["rw_vres_down"] = bf(p["rw_vres_down"][l - 1])
        lp["rw_vres_up"] = bf(p["rw_vres_up"][l - 1])
        lp["rw_vres_bias"] = row(p["rw_vres_bias"][l - 1])
    del d
    return lp


def _head_indicator():
    h = jnp.arange(RW_WIDTH) // RW_HEAD_DIM
    return (h[:, None] == h[None, :]).astype(BF16)


def _mixer(x, mods, mod_row, vfirst, s_rw, s_gla, lp, hind, *, p_col, tm, need_out):
    bsz, t, d = x.shape
    if p_col is None:
        x_tr, g_tr, a = False, False, SUBLANES
    else:
        x_tr, g_tr = p_col, True
        a = (t // GRID_W) if p_col else GRID_W
    rw, gq, gald, gg, mg = _inproj(x, mods, mod_row, lp["gains"][0:1], lp["w_in_parts"],
                                   x_tr=x_tr, g_tr=g_tr, a=a, tm=tm)
    r, v, kk, g, gb, lw, kd, bd = _rwprep(rw, vfirst, lp, hind, vf_tr=x_tr, a=a, tm=tm)
    yf, yb, s_rw_out = _rwscan(r, v, kk, lw, kd, bd, s_rw)
    q, k, gv, la = _glaprep(gq, gald, lp, tm=tm)
    of, ob, s_gla_out = _glascan(q, k, gv, la, s_gla)
    if not need_out:
        return None, v, s_rw_out, s_gla_out
    yrw = _rwpost(yf, yb, g, gb, lp, hind, tm=tm)
    ygla = _glapost(of, ob, gg, lp, tm=tm)
    x_new = _merge(x, yrw, ygla, mg, mods, mod_row, lp["gains"], lp["merge_ws"],
                   x_tr=x_tr, g_tr=g_tr, a=a, tm=tm)
    return x_new, v, s_rw_out, s_gla_out


def kernel(x, c, ctx, c_ctx, w_in, rw_mu, rw_w0, rw_w_up, rw_a0, rw_a_up, rw_g_up, rw_k_k, rw_k_a,
           rw_r_k, rw_gn_w, rw_gn_b, rw_vres_down, rw_vres_up, rw_vres_bias, rw_out, gla_conv,
           gla_alpha_up, gla_alpha_bias, gla_norm_w, gla_out, merge_out, mlp_w1, mlp_w2, ada_w,
           ada_b, norm_mix_pre, norm_mix_post, norm_ffn_pre, norm_ffn_post):
    p = dict(w_in=w_in, rw_mu=rw_mu, rw_w0=rw_w0, rw_w_up=rw_w_up, rw_a0=rw_a0, rw_a_up=rw_a_up,
             rw_g_up=rw_g_up, rw_k_k=rw_k_k, rw_k_a=rw_k_a, rw_r_k=rw_r_k.reshape(rw_r_k.shape[0], -1),
             rw_gn_w=rw_gn_w, rw_gn_b=rw_gn_b, rw_vres_down=rw_vres_down, rw_vres_up=rw_vres_up,
             rw_vres_bias=rw_vres_bias, rw_out=rw_out, gla_conv=gla_conv, gla_alpha_up=gla_alpha_up,
             gla_alpha_bias=gla_alpha_bias, gla_norm_w=gla_norm_w, gla_out=gla_out,
             merge_out=merge_out, mlp_w1=mlp_w1, mlp_w2=mlp_w2, norm_mix_pre=norm_mix_pre,
             norm_mix_post=norm_mix_post, norm_ffn_pre=norm_ffn_pre, norm_ffn_post=norm_ffn_post)
    bsz, t, d = x.shape
    depth = w_in.shape[0]
    assert bsz < SUBLANES and d % LANES == 0
    assert t % (GRID_W * SUBLANES) == 0 and ctx.shape[1] % CHUNK == 0
    tm = min(TOKEN_TILE, t)
    tm_ctx = min(TOKEN_TILE, ctx.shape[1])
    assert t % tm == 0 and ctx.shape[1] % tm_ctx == 0 and tm % GRID_W == 0

    cc = jnp.concatenate([c, c_ctx[None, :], jnp.zeros((SUBLANES - 1 - bsz, d), F32)], axis=0)
    mods = _ada_mods(cc, ada_w, ada_b).reshape(depth, SUBLANES, 6, d)
    hind = _head_indicator()
    lat_row = lambda b: b
    ctx_row = lambda b: bsz

    x_lat, x_ctx = x, ctx
    vf_lat = vf_ctx = None
    for l in range(depth):
        last = l == depth - 1
        lp = _layer_params(l, p)
        z_rw = jnp.zeros((bsz, 2, RW_HEADS, RW_HEAD_DIM, RW_HEAD_DIM), F32)
        z_gla = jnp.zeros((bsz, 2, GLA_HEADS, GLA_DV, GLA_DK), F32)
        x_ctx_new, v_ctx, s_rw, s_gla = _mixer(
            x_ctx, mods[l], ctx_row, vf_ctx, z_rw, z_gla, lp, hind,
            p_col=None, tm=tm_ctx, need_out=not last)
        x_lat, v_lat, _, _ = _mixer(
            x_lat, mods[l], lat_row, vf_lat, s_rw, s_gla, lp, hind,
            p_col=(l % 2 == 1), tm=tm, need_out=True)
        if l == 0:
            vf_lat, vf_ctx = v_lat, v_ctx
        if not last:
            x_ctx = x_ctx_new
    return x_lat
```

```python
import functools

import jax
import jax.numpy as jnp
from jax import lax
from jax.experimental import pallas as pl
from jax.experimental.pallas import tpu as pltpu

F32 = jnp.float32
BF16 = jnp.bfloat16

GRID_W = 64
RMS_EPS = 1e-6
RW_HEADS = 8
RW_HEAD_DIM = 64
RW_WIDTH = RW_HEADS * RW_HEAD_DIM
RW_GN_EPS = 64e-5
RW_COLS = 1920
GLA_HEADS = 4
GLA_DK = 64
GLA_DV = 128
GLA_KW = GLA_HEADS * GLA_DK
GLA_VW = GLA_HEADS * GLA_DV
GLA_QKV_W = 2 * GLA_KW + GLA_VW
GLA_GATE_RANK = 16
GLA_TAU = 16.0
CHUNK = 64
LANES = 128
SUBLANES = 8
VMEM_LIMIT = 56 * 1024 * 1024
TOKEN_TILE = 512


def _dot(a, b):
    return jnp.dot(a.astype(BF16), b.astype(BF16), preferred_element_type=F32)


def _dot_nt(a, b):
    return lax.dot_general(a.astype(BF16), b.astype(BF16), (((1,), (1,)), ((), ())),
                           preferred_element_type=F32)


def _dot_tn(a, b):
    return lax.dot_general(a.astype(BF16), b.astype(BF16), (((0,), (0,)), ((), ())),
                           preferred_element_type=F32)


def _split2(x):
    hi = x.astype(BF16)
    lo = (x - hi.astype(F32)).astype(BF16)
    return hi, lo


def _dot_ind_rhs(x, ind):
    hi, lo = _split2(x)
    return (jnp.dot(hi, ind, preferred_element_type=F32)
            + jnp.dot(lo, ind, preferred_element_type=F32))


def _dot_ind_lhs(ind, x):
    hi, lo = _split2(x)
    return (jnp.dot(ind, hi, preferred_element_type=F32)
            + jnp.dot(ind, lo, preferred_element_type=F32))


def _sigmoid(x):
    return jax.nn.sigmoid(x)


def _silu(x):
    return x * jax.nn.sigmoid(x)


def _softplus(z):
    return jnp.maximum(z, 0.0) + jnp.log1p(jnp.exp(-jnp.abs(z)))


def _log_sigmoid(z):
    return -_softplus(-z)


def _rms(x, gain):
    return x * lax.rsqrt(jnp.mean(x * x, axis=-1, keepdims=True) + RMS_EPS) * gain


def _load_tok(ref, transposed, a, k):
    if not transposed:
        return ref[...]
    return jnp.concatenate([ref[:, i, :] for i in range(k)], axis=0)


def _store_tok(ref, val, transposed, a, k):
    if not transposed:
        ref[...] = val
    else:
        for i in range(k):
            ref[:, i, :] = val[i * a:(i + 1) * a, :]


def _tok_view(arr, transposed, a):
    if not transposed:
        return arr
    b, t, f = arr.shape
    return arr.reshape(b, a, t // a, f)


def _tok_unview(arr, transposed, t, f):
    if not transposed:
        return arr
    return arr.reshape(arr.shape[0], t, f)


def _tok_spec(f, tm, transposed, a):
    if not transposed:
        return pl.BlockSpec((None, tm, f), lambda b, j: (b, j, 0))
    return pl.BlockSpec((None, a, tm // a, f), lambda b, j: (b, 0, j, 0))


def _tok_shape(bsz, t, f, transposed, a):
    if not transposed:
        return jax.ShapeDtypeStruct((bsz, t, f), F32)
    return jax.ShapeDtypeStruct((bsz, a, t // a, f), F32)


def _const_spec(shape):
    nd = len(shape)
    return pl.BlockSpec(shape, lambda *_: (0,) * nd, pipeline_mode=pl.Buffered(1))


def _params(ndim):
    return pltpu.CompilerParams(dimension_semantics=("arbitrary",) * ndim,
                                vmem_limit_bytes=VMEM_LIMIT)


def _halo_specs(f, tm, t):
    sub = tm // SUBLANES
    last = t // SUBLANES - 1
    prev = pl.BlockSpec((None, SUBLANES, f), lambda b, j: (b, jnp.maximum(j * sub - 1, 0), 0))
    nxt = pl.BlockSpec((None, SUBLANES, f), lambda b, j: (b, jnp.minimum((j + 1) * sub, last), 0))
    return prev, nxt


def _shifted(cur, prev8, next8, first, last):
    tm = cur.shape[0]
    row = lax.broadcasted_iota(jnp.int32, cur.shape, 0)
    p_row = jnp.where(first, 0.0, prev8[SUBLANES - 1:SUBLANES, :])
    n_row = jnp.where(last, 0.0, next8[0:1, :])
    prev = jnp.where(row == 0, p_row, pltpu.roll(cur, 1, 0))
    nxt = jnp.where(row == tm - 1, n_row, pltpu.roll(cur, tm - 1, 0))
    return prev, nxt


def _ada_kernel(c_ref, w_ref, b_ref, o_ref):
    cc = c_ref[...]
    o_ref[...] = _dot(_silu(cc), w_ref[...]) + b_ref[...]


def _ada_mods(cc, ada_w, ada_b):
    nl, d, n6 = ada_w.shape
    tn = d
    return pl.pallas_call(
        _ada_kernel,
        grid=(nl, n6 // tn),
        in_specs=[pl.BlockSpec((SUBLANES, d), lambda l, n: (0, 0)),
                  pl.BlockSpec((None, d, tn), lambda l, n: (l, 0, n)),
                  pl.BlockSpec((None, 1, tn), lambda l, n: (l, 0, n))],
        out_specs=pl.BlockSpec((None, SUBLANES, tn), lambda l, n: (l, 0, n)),
        out_shape=jax.ShapeDtypeStruct((nl, SUBLANES, n6), F32),
        compiler_params=_params(2),
        name="ada",
    )(cc, ada_w, ada_b.reshape(nl, 1, n6))


def _inproj_kernel(x_ref, mod_ref, gain_ref, w_rw, w_q, w_ald, w_gg, w_mg,
                   o_rw, o_q, o_ald, o_gg, o_mg, *, x_tr, g_tr, a, k):
    d = gain_ref.shape[-1]
    x = _load_tok(x_ref, x_tr, a, k)
    h = _rms(x, gain_ref[...]) * (1.0 + mod_ref[1:2, :]) + mod_ref[0:1, :]
    hb = h.astype(BF16)
    o_rw[...] = jnp.dot(hb, w_rw[...], preferred_element_type=F32)
    o_mg[...] = jnp.dot(hb, w_mg[...], preferred_element_type=F32)
    _store_tok(o_q, jnp.dot(hb, w_q[...], preferred_element_type=F32), g_tr, a, k)
    _store_tok(o_ald, jnp.dot(hb, w_ald[...], preferred_element_type=F32), g_tr, a, k)
    _store_tok(o_gg, jnp.dot(hb, w_gg[...], preferred_element_type=F32), g_tr, a, k)


def _inproj(x, mods, mod_row, gain, ws, *, x_tr, g_tr, a, tm):
    bsz, t, d = x.shape
    k = tm // a
    kern = functools.partial(_inproj_kernel, x_tr=x_tr, g_tr=g_tr, a=a, k=k)
    widths = (RW_COLS, GLA_QKV_W, LANES, GLA_VW, 2 * d)
    out_tr = (False, g_tr, g_tr, g_tr, False)
    outs = pl.pallas_call(
        kern,
        grid=(bsz, t // tm),
        in_specs=[_tok_spec(d, tm, x_tr, a),
                  pl.BlockSpec((None, 6, d), lambda b, j: (mod_row(b), 0, 0)),
                  _const_spec((1, d))] + [_const_spec(w.shape) for w in ws],
        out_specs=[_tok_spec(f, tm, tr, a) for f, tr in zip(widths, out_tr)],
        out_shape=[_tok_shape(bsz, t, f, tr, a) for f, tr in zip(widths, out_tr)],
        compiler_params=_params(2),
        name="inproj",
    )(_tok_view(x, x_tr, a), mods, gain, *ws)
    return [_tok_unview(o, tr, t, f) for o, f, tr in zip(outs, widths, out_tr)]


def _rwprep_kernel(*refs, has_vres, vf_tr, a, k):
    it = iter(refs)
    f_cur, f_prev, f_next = next(it), next(it), next(it)
    vf_ref = next(it) if has_vres else None
    mu, w0, w_up, a0, a_up, g_up, k_k, k_a, r_k = (next(it) for _ in range(9))
    if has_vres:
        vdown, vup, vbias = next(it), next(it), next(it)
    hind = next(it)
    o_r, o_v, o_kk, o_g, o_gb, o_lw, o_kd, o_bd = (next(it) for _ in range(8))

    j = pl.program_id(1)
    first = j == 0
    last = j == pl.num_programs(1) - 1
    f = f_cur[...]
    prev, nxt = _shifted(f, f_prev[...], f_next[...], first, last)
    fs = f + mu[...] * (0.5 * (prev + nxt) - f)
    w = RW_WIDTH
    r = fs[:, 0:w]
    kx = fs[:, w:2 * w]
    v = fs[:, 2 * w:3 * w]
    wd = fs[:, 3 * w:3 * w + LANES]
    ad = fs[:, 3 * w + LANES:3 * w + 2 * LANES]
    gd = fs[:, 3 * w + 2 * LANES:3 * w + 3 * LANES]
    if has_vres:
        vf = _load_tok(vf_ref, vf_tr, a, k)
        mix = _sigmoid(vbias[...] + _dot(_dot(v, vdown[...]), vup[...]))
        v = v + (vf - v) * mix
    kk = kx * k_k[...]
    kk = kk * lax.rsqrt(_dot_ind_rhs(kk * kk, hind[...]) + 1e-12)
    twd = jnp.tanh(wd)
    ksum = None
    for di in range(2):
        wlog = -_softplus(-(w0[di:di + 1, :] + _dot(twd, w_up[di]))) - 0.5
        o_lw[di] = -jnp.exp(wlog)
        lr = _sigmoid(a0[di:di + 1, :] + _dot(ad, a_up[di]))
        kd = kx * (1.0 + (lr - 1.0) * k_a[...])
        o_kd[di] = kd
        o_bd[di] = lr * kk
        ksum = kd if ksum is None else ksum + kd
    g = _dot(_sigmoid(gd), g_up[...])
    bonus = _dot_ind_rhs(r * ksum * r_k[...], hind[...]) * v
    o_r[...] = r
    o_v[...] = v
    o_kk[...] = kk
    o_g[...] = g
    o_gb[...] = bonus * g


def _rwprep(f, vfirst, lp, hind, *, vf_tr, a, tm):
    bsz, t, fc = f.shape
    w = RW_WIDTH
    has_vres = vfirst is not None
    k = tm // a
    kern = functools.partial(_rwprep_kernel, has_vres=has_vres, vf_tr=vf_tr, a=a, k=k)
    prev_spec, next_spec = _halo_specs(fc, tm, t)
    args = [f, f, f]
    in_specs = [_tok_spec(fc, tm, False, a), prev_spec, next_spec]
    if has_vres:
        args.append(_tok_view(vfirst, vf_tr, a))
        in_specs.append(_tok_spec(w, tm, vf_tr, a))
    small = [lp["rw_mu"], lp["rw_w0"], lp["rw_w_up_pad"], lp["rw_a0"], lp["rw_a_up_pad"],
             lp["rw_g_up"], lp["rw_k_k"], lp["rw_k_a"], lp["rw_r_k"]]
    if has_vres:
        small += [lp["rw_vres_down"], lp["rw_vres_up"], lp["rw_vres_bias"]]
    small.append(hind)
    args += small
    in_specs += [_const_spec(s.shape) for s in small]
    tok = pl.BlockSpec((None, tm, w), lambda b, j: (b, j, 0))
    tok2 = pl.BlockSpec((2, None, tm, w), lambda b, j: (0, b, j, 0))
    s1 = jax.ShapeDtypeStruct((bsz, t, w), F32)
    s2 = jax.ShapeDtypeStruct((2, bsz, t, w), F32)
    return pl.pallas_call(
        kern,
        grid=(bsz, t // tm),
        in_specs=in_specs,
        out_specs=[tok] * 5 + [tok2] * 3,
        out_shape=[s1] * 5 + [s2] * 3,
        compiler_params=_params(2),
        name="rwprep",
    )(*args)


def _before(row, col, reverse, inclusive):
    if reverse:
        return (col >= row) if inclusive else (col > row)
    return (col <= row) if inclusive else (col < row)


def _cum_parts(lw, reverse):
    row = lax.broadcasted_iota(jnp.int32, (CHUNK, CHUNK), 0)
    col = lax.broadcasted_iota(jnp.int32, (CHUNK, CHUNK), 1)
    tri = jnp.where(_before(row, col, reverse, True), 1.0, 0.0).astype(BF16)
    cum = _dot_ind_lhs(tri, lw)
    tot = cum[0:1, :] if reverse else cum[CHUNK - 1:CHUNK, :]
    return cum, tot


def _rwscan_kernel(rf, vf, kkf, lwf, kdf, bdf, rb, vb, kkb, lwb, kdb, bdb, s0_ref,
                   yf_ref, yb_ref, sfin_ref, s_scr):
    c = pl.program_id(1)

    @pl.when(c == 0)
    def _():
        s_scr[...] = s0_ref[...]

    n = RW_HEAD_DIM
    c2 = 2 * CHUNK
    row2 = lax.broadcasted_iota(jnp.int32, (c2, c2), 0)
    col2 = lax.broadcasted_iota(jnp.int32, (c2, c2), 1)
    rp = row2 % CHUNK
    cp = col2 % CHUNK
    bottom = row2 // CHUNK
    row = lax.broadcasted_iota(jnp.int32, (CHUNK, CHUNK), 0)
    col = lax.broadcasted_iota(jnp.int32, (CHUNK, CHUNK), 1)
    eye = jnp.where(row == col, 1.0, 0.0)
    zeros = jnp.zeros((CHUNK, n), BF16)

    chains = []
    for di, (r_ref, v_ref, kk_ref, lw_ref, kd_ref, bd_ref, y_ref) in enumerate(
            ((rf, vf, kkf, lwf, kdf, bdf, yf_ref), (rb, vb, kkb, lwb, kdb, bdb, yb_ref))):
        reverse = di == 1
        lw = lw_ref[...]
        cum, tot = _cum_parts(lw, reverse)
        e_neg = jnp.exp(-cum)
        e_end = jnp.exp(tot - cum)
        kd = kd_ref[...]
        bd = bd_ref[...]
        ops = dict(
            rt=(r_ref[...] * jnp.exp(cum)).astype(BF16),
            at=(kk_ref[...] * jnp.exp(cum - lw)).astype(BF16),
            kt=(kd * e_neg).astype(BF16),
            bt=(bd * e_neg).astype(BF16),
            kh=(kd * e_end).astype(BF16),
            bh=(bd * e_end).astype(BF16),
            v=v_ref[...].astype(BF16),
            g_tot=jnp.exp(tot),
        )
        keep = (cp > rp - bottom) if reverse else (cp < rp + bottom)
        for h in range(RW_HEADS):
            sl = slice(h * n, (h + 1) * n)
            ch = {k: val[:, sl] for k, val in ops.items()}
            ch.update(di=di, h=h, sl=sl, keep=keep, y_ref=y_ref)
            chains.append(ch)

    for ch in chains:
        g = _dot_nt(jnp.concatenate([ch["at"], ch["rt"]], axis=0),
                    jnp.concatenate([ch["bt"], ch["kt"]], axis=0))
        g = jnp.where(ch["keep"], g, 0.0).astype(BF16)
        ch["g_top"], ch["g_bot"] = g[:CHUNK], g[CHUNK:]
        ch["l"] = g[:CHUNK, :CHUNK]
    for ch in chains:
        ch["p"] = _dot(ch["l"], ch["l"])
        ch["av"] = _dot(ch["g_top"], jnp.concatenate([zeros, ch["v"]], axis=0))
        ch["kv"] = _dot_tn(ch["v"], ch["kh"])
        ch["x"] = eye - ch["l"].astype(F32)
    for _ in range(4):
        for ch in chains:
            z = _dot(jnp.concatenate([ch["x"], ch["p"]], axis=0), ch["p"])
            ch["x"] = ch["x"] + z[:CHUNK]
            ch["p"] = z[CHUNK:]
    for ch in chains:
        ch["x"] = ch["x"] + _dot(ch["x"], ch["p"])
    for ch in chains:
        ch["wm"] = _dot(ch["x"], ch["at"]).astype(BF16)
        ch["u0"] = -_dot(ch["x"], ch["av"])
    for ch in chains:
        ch["u0t"] = ch["u0"].T
    for ch in chains:
        s = s_scr[ch["di"], ch["h"]]
        ch["s"] = s
        sb = s.astype(BF16)
        pr = _dot_nt(jnp.concatenate([ch["wm"], ch["rt"]], axis=0), sb)
        ch["u"] = ch["u0"] - pr[:CHUNK]
        ch["rs"] = pr[CHUNK:]
        ch["ut"] = ch["u0t"] - _dot_nt(sb, ch["wm"])
    for ch in chains:
        ch["y_ref"][:, ch["sl"]] = ch["rs"] + _dot(
            ch["g_bot"], jnp.concatenate([ch["u"].astype(BF16), ch["v"]], axis=0))
        s_scr[ch["di"], ch["h"]] = ch["s"] * ch["g_tot"] + ch["kv"] + _dot(ch["ut"], ch["bh"])

    @pl.when(c == pl.num_programs(1) - 1)
    def _():
        sfin_ref[...] = s_scr[...]


def _rwscan(r, v, kk, lw, kd, bd, s0):
    bsz, t, w = r.shape
    nc = t // CHUNK
    tok_f = pl.BlockSpec((None, CHUNK, w), lambda b, c: (b, c, 0))
    tok_b = pl.BlockSpec((None, CHUNK, w), lambda b, c: (b, nc - 1 - c, 0))
    dir_f = pl.BlockSpec((None, None, CHUNK, w), lambda b, c: (0, b, c, 0))
    dir_b = pl.BlockSpec((None, None, CHUNK, w), lambda b, c: (1, b, nc - 1 - c, 0))
    st = pl.BlockSpec((None, 2, RW_HEADS, RW_HEAD_DIM, RW_HEAD_DIM), lambda b, c: (b, 0, 0, 0, 0))
    y_shape = jax.ShapeDtypeStruct((bsz, t, w), F32)
    yf, yb, sfin = pl.pallas_call(
        _rwscan_kernel,
        grid=(bsz, nc),
        in_specs=[tok_f, tok_f, tok_f, dir_f, dir_f, dir_f,
                  tok_b, tok_b, tok_b, dir_b, dir_b, dir_b, st],
        out_specs=[tok_f, tok_b, st],
        out_shape=[y_shape, y_shape, jax.ShapeDtypeStruct(s0.shape, F32)],
        scratch_shapes=[pltpu.VMEM((2, RW_HEADS, RW_HEAD_DIM, RW_HEAD_DIM), F32)],
        compiler_params=_params(2),
        name="rwscan",
    )(r, v, kk, lw, kd, bd, r, v, kk, lw, kd, bd, s0)
    return yf, yb, sfin


def _rwpost_kernel(yf_ref, yb_ref, g_ref, gb_ref, gnw, gnb, hind, o_ref):
    y = yf_ref[...] + yb_ref[...]
    inv_n = 1.0 / RW_HEAD_DIM
    mean = _dot_ind_rhs(y, hind[...]) * inv_n
    yc = y - mean
    var = _dot_ind_rhs(yc * yc, hind[...]) * inv_n
    yn = yc * lax.rsqrt(var + RW_GN_EPS) * gnw[...] + gnb[...]
    o_ref[...] = yn * g_ref[...] + gb_ref[...]


def _rwpost(yf, yb, g, gb, lp, hind, *, tm):
    bsz, t, w = yf.shape
    tok = pl.BlockSpec((None, tm, w), lambda b, j: (b, j, 0))
    return pl.pallas_call(
        _rwpost_kernel,
        grid=(bsz, t // tm),
        in_specs=[tok, tok, tok, tok, _const_spec((1, w)), _const_spec((1, w)),
                  _const_spec(hind.shape)],
        out_specs=tok,
        out_shape=jax.ShapeDtypeStruct((bsz, t, w), F32),
        compiler_params=_params(2),
        name="rwpost",
    )(yf, yb, g, gb, lp["rw_gn_w"], lp["rw_gn_b"], hind)


def _glaprep_kernel(q_cur, q_prev, q_next, ald_ref, conv, up, bias, o_q, o_k, o_v, o_la):
    j = pl.program_id(1)
    x = q_cur[...]
    prev, nxt = _shifted(x, q_prev[...], q_next[...], j == 0, j == pl.num_programs(1) - 1)
    y = _silu(conv[0:1, :] * prev + conv[1:2, :] * x + conv[2:3, :] * nxt)
    o_q[...] = y[:, 0:GLA_KW] * (GLA_DK ** -0.5)
    o_k[...] = y[:, GLA_KW:2 * GLA_KW]
    o_v[...] = y[:, 2 * GLA_KW:]
    ald = ald_ref[...]
    for di in range(2):
        o_la[di] = _log_sigmoid(_dot(ald, up[di]) + bias[di:di + 1, :]) / GLA_TAU


def _glaprep(qkv, ald, lp, *, tm):
    bsz, t, fq = qkv.shape
    prev_spec, next_spec = _halo_specs(fq, tm, t)
    tok = lambda f: pl.BlockSpec((None, tm, f), lambda b, j: (b, j, 0))
    small = [lp["gla_conv"], lp["gla_alpha_up_pad"], lp["gla_alpha_bias"]]
    return pl.pallas_call(
        _glaprep_kernel,
        grid=(bsz, t // tm),
        in_specs=[tok(fq), prev_spec, next_spec, tok(LANES)] + [_const_spec(s.shape) for s in small],
        out_specs=[tok(GLA_KW), tok(GLA_KW), tok(GLA_VW),
                   pl.BlockSpec((2, None, tm, GLA_KW), lambda b, j: (0, b, j, 0))],
        out_shape=[jax.ShapeDtypeStruct((bsz, t, GLA_KW), F32),
                   jax.ShapeDtypeStruct((bsz, t, GLA_KW), F32),
                   jax.ShapeDtypeStruct((bsz, t, GLA_VW), F32),
                   jax.ShapeDtypeStruct((2, bsz, t, GLA_KW), F32)],
        compiler_params=_params(2),
        name="glaprep",
    )(qkv, qkv, qkv, ald, *small)


def _glascan_kernel(qf, kf, vf, laf, qb, kb, vb, lab, s0_ref, of_ref, ob_ref, sfin_ref, s_scr):
    c = pl.program_id(1)

    @pl.when(c == 0)
    def _():
        s_scr[...] = s0_ref[...]

    row = lax.broadcasted_iota(jnp.int32, (CHUNK, CHUNK), 0)
    col = lax.broadcasted_iota(jnp.int32, (CHUNK, CHUNK), 1)
    chains = []
    for di, (q_ref, k_ref, v_ref, la_ref, o_ref) in enumerate(
            ((qf, kf, vf, laf, of_ref), (qb, kb, vb, lab, ob_ref))):
        reverse = di == 1
        cum, tot = _cum_parts(la_ref[...], reverse)
        k = k_ref[...]
        q_dec = (q_ref[...] * jnp.exp(cum)).astype(BF16)
        k_inv = (k * jnp.exp(-cum)).astype(BF16)
        k_end = (k * jnp.exp(tot - cum)).astype(BF16)
        dec = jnp.exp(tot)
        v = v_ref[...].astype(BF16)
        keep = _before(row, col, reverse, True)
        for h in range(GLA_HEADS):
            sk = slice(h * GLA_DK, (h + 1) * GLA_DK)
            sv = slice(h * GLA_DV, (h + 1) * GLA_DV)
            chains.append(dict(di=di, h=h, sv=sv, keep=keep, o_ref=o_ref, q=q_dec[:, sk],
                               ki=k_inv[:, sk], ke=k_end[:, sk], dec=dec[:, sk], v=v[:, sv]))
    for ch in chains:
        ch["sc"] = jnp.where(ch["keep"], _dot_nt(ch["q"], ch["ki"]), 0.0)
        ch["kv"] = _dot_tn(ch["v"], ch["ke"])
        ch["s"] = s_scr[ch["di"], ch["h"]]
        ch["inter"] = _dot_nt(ch["q"], ch["s"])
    for ch in chains:
        ch["o_ref"][:, ch["sv"]] = _dot(ch["sc"], ch["v"]) + ch["inter"]
        s_scr[ch["di"], ch["h"]] = ch["s"] * ch["dec"] + ch["kv"]

    @pl.when(c == pl.num_programs(1) - 1)
    def _():
        sfin_ref[...] = s_scr[...]


def _glascan(q, k, v, la, s0):
    bsz, t, _ = q.shape
    nc = t // CHUNK
    tok_f = lambda f: pl.BlockSpec((None, CHUNK, f), lambda b, c: (b, c, 0))
    tok_b = lambda f: pl.BlockSpec((None, CHUNK, f), lambda b, c: (b, nc - 1 - c, 0))
    dir_f = pl.BlockSpec((None, None, CHUNK, GLA_KW), lambda b, c: (0, b, c, 0))
    dir_b = pl.BlockSpec((None, None, CHUNK, GLA_KW), lambda b, c: (1, b, nc - 1 - c, 0))
    st = pl.BlockSpec((None, 2, GLA_HEADS, GLA_DV, GLA_DK), lambda b, c: (b, 0, 0, 0, 0))
    o_shape = jax.ShapeDtypeStruct((bsz, t, GLA_VW), F32)
    return pl.pallas_call(
        _glascan_kernel,
        grid=(bsz, nc),
        in_specs=[tok_f(GLA_KW), tok_f(GLA_KW), tok_f(GLA_VW), dir_f,
                  tok_b(GLA_KW), tok_b(GLA_KW), tok_b(GLA_VW), dir_b, st],
        out_specs=[tok_f(GLA_VW), tok_b(GLA_VW), st],
        out_shape=[o_shape, o_shape, jax.ShapeDtypeStruct(s0.shape, F32)],
        scratch_shapes=[pltpu.VMEM((2, GLA_HEADS, GLA_DV, GLA_DK), F32)],
        compiler_params=_params(2),
        name="glascan",
    )(q, k, v, la, q, k, v, la, s0)


def _glapost_kernel(of_ref, ob_ref, gate_ref, nw, y_ref):
    o = of_ref[...] + ob_ref[...]
    gate = gate_ref[...]
    for h in range(GLA_HEADS):
        sv = slice(h * GLA_DV, (h + 1) * GLA_DV)
        oh = o[:, sv]
        y_ref[:, sv] = _rms(oh, nw[...]) * _silu(gate[:, sv])


def _glapost(of, ob, gate, lp, *, tm):
    bsz, t, w = of.shape
    tok = pl.BlockSpec((None, tm, w), lambda b, j: (b, j, 0))
    return pl.pallas_call(
        _glapost_kernel,
        grid=(bsz, t // tm),
        in_specs=[tok, tok, tok, _const_spec((1, GLA_DV))],
        out_specs=tok,
        out_shape=jax.ShapeDtypeStruct((bsz, t, w), F32),
        compiler_params=_params(2),
        name="glapost",
    )(of, ob, gate, lp["gla_norm_w"])


def _merge_kernel(x_ref, yrw_ref, ygla_ref, mg_ref, mod_ref, gains, w_rwo, w_glao, w_mo, w1, w2,
                  o_ref, *, x_tr, g_tr, a, k):
    d = x_ref.shape[-1]
    x = _load_tok(x_ref, x_tr, a, k)
    ygla = _load_tok(ygla_ref, g_tr, a, k)
    mg = mg_ref[...]
    br = (_sigmoid(mg[:, :d]) * _dot(yrw_ref[...], w_rwo[...])
          + _sigmoid(mg[:, d:]) * _dot(ygla, w_glao[...]))
    m = _dot(br, w_mo[...])
    x1 = x + mod_ref[2:3, :] * _rms(m, gains[1:2, :])
    h2 = _rms(x1, gains[2:3, :]) * (1.0 + mod_ref[4:5, :]) + mod_ref[3:4, :]
    hid = jnp.maximum(_dot(h2, w1[...]), 0.0)
    f = _dot(hid * hid, w2[...])
    x2 = x1 + mod_ref[5:6, :] * _rms(f, gains[3:4, :])
    _store_tok(o_ref, x2, x_tr, a, k)


def _merge(x, yrw, ygla, mg, mods, mod_row, gains, ws, *, x_tr, g_tr, a, tm):
    bsz, t, d = x.shape
    k = tm // a
    kern = functools.partial(_merge_kernel, x_tr=x_tr, g_tr=g_tr, a=a, k=k)
    out = pl.pallas_call(
        kern,
        grid=(bsz, t // tm),
        in_specs=[_tok_spec(d, tm, x_tr, a), _tok_spec(RW_WIDTH, tm, False, a),
                  _tok_spec(GLA_VW, tm, g_tr, a), _tok_spec(2 * d, tm, False, a),
                  pl.BlockSpec((None, 6, d), lambda b, j: (mod_row(b), 0, 0)),
                  _const_spec(gains.shape)] + [_const_spec(w.shape) for w in ws],
        out_specs=_tok_spec(d, tm, x_tr, a),
        out_shape=_tok_shape(bsz, t, d, x_tr, a),
        compiler_params=_params(2),
        name="merge",
    )(_tok_view(x, x_tr, a), yrw, _tok_view(ygla, g_tr, a), mg, mods, gains, *ws)
    return _tok_unview(out, x_tr, t, d)


def _layer_params(l, p):
    d = p["w_in"].shape[1]
    w_in = p["w_in"][l]
    g0 = RW_COLS
    bf = lambda w: w.astype(BF16)
    row = lambda w: w.reshape(1, -1)
    lp = {
        "w_in_parts": [
            bf(w_in[:, :g0]),
            bf(w_in[:, g0:g0 + GLA_QKV_W]),
            bf(jnp.pad(w_in[:, g0 + GLA_QKV_W:g0 + GLA_QKV_W + 2 * GLA_GATE_RANK],
                       ((0, 0), (0, LANES - 2 * GLA_GATE_RANK)))),
            bf(w_in[:, g0 + GLA_QKV_W + 2 * GLA_GATE_RANK:g0 + GLA_QKV_W + 2 * GLA_GATE_RANK + GLA_VW]),
            bf(w_in[:, g0 + GLA_QKV_W + 2 * GLA_GATE_RANK + GLA_VW:]),
        ],
        "rw_mu": row(p["rw_mu"][l]),
        "rw_w0": p["rw_w0"][l],
        "rw_a0": p["rw_a0"][l],
        "rw_g_up": bf(p["rw_g_up"][l]),
        "rw_k_k": row(p["rw_k_k"][l]),
        "rw_k_a": row(p["rw_k_a"][l]),
        "rw_r_k": row(p["rw_r_k"][l]),
        "rw_gn_w": row(p["rw_gn_w"][l]),
        "rw_gn_b": row(p["rw_gn_b"][l]),
        "gla_conv": p["gla_conv"][l],
        "gla_alpha_bias": p["gla_alpha_bias"][l],
        "gla_norm_w": row(p["gla_norm_w"][l]),
        "merge_ws": [bf(p["rw_out"][l]), bf(p["gla_out"][l]), bf(p["merge_out"][l]),
                     bf(p["mlp_w1"][l]), bf(p["mlp_w2"][l])],
        "gains": jnp.stack([p["norm_mix_pre"][l], p["norm_mix_post"][l],
                            p["norm_ffn_pre"][l], p["norm_ffn_post"][l]]),
    }
    rank = p["rw_w_up"].shape[2]
    pad_dir = lambda w, r: jnp.stack([jnp.pad(w[di], ((di * r, LANES - (di + 1) * r), (0, 0)))
                                      for di in range(2)])
    lp["rw_w_up_pad"] = bf(pad_dir(p["rw_w_up"][l], rank))
    lp["rw_a_up_pad"] = bf(pad_dir(p["rw_a_up"][l], p["rw_a_up"].shape[2]))
    lp["gla_alpha_up_pad"] = bf(pad_dir(p["gla_alpha_up"][l], GLA_GATE_RANK))
    if l > 0:
        lp["rw_vres_down"] = bf(p["rw_vres_down"][l - 1])
        lp["rw_vres_up"] = bf(p["rw_vres_up"][l - 1])
        lp["rw_vres_bias"] = row(p["rw_vres_bias"][l - 1])
    del d
    return lp


def _head_indicator():
    h = jnp.arange(RW_WIDTH) // RW_HEAD_DIM
    return (h[:, None] == h[None, :]).astype(BF16)


def _mixer(x, mods, mod_row, vfirst, s_rw, s_gla, lp, hind, *, p_col, tm, need_out):
    bsz, t, d = x.shape
    if p_col is None:
        x_tr, g_tr, a = False, False, SUBLANES
    else:
        x_tr, g_tr = p_col, True
        a = (t // GRID_W) if p_col else GRID_W
    rw, gq, gald, gg, mg = _inproj(x, mods, mod_row, lp["gains"][0:1], lp["w_in_parts"],
                                   x_tr=x_tr, g_tr=g_tr, a=a, tm=tm)
    r, v, kk, g, gb, lw, kd, bd = _rwprep(rw, vfirst, lp, hind, vf_tr=x_tr, a=a, tm=tm)
    yf, yb, s_rw_out = _rwscan(r, v, kk, lw, kd, bd, s_rw)
    q, k, gv, la = _glaprep(gq, gald, lp, tm=tm)
    of, ob, s_gla_out = _glascan(q, k, gv, la, s_gla)
    if not need_out:
        return None, v, s_rw_out, s_gla_out
    yrw = _rwpost(yf, yb, g, gb, lp, hind, tm=tm)
    ygla = _glapost(of, ob, gg, lp, tm=tm)
    x_new = _merge(x, yrw, ygla, mg, mods, mod_row, lp["gains"], lp["merge_ws"],
                   x_tr=x_tr, g_tr=g_tr, a=a, tm=tm)
    return x_new, v, s_rw_out, s_gla_out


def kernel(x, c, ctx, c_ctx, w_in, rw_mu, rw_w0, rw_w_up, rw_a0, rw_a_up, rw_g_up, rw_k_k, rw_k_a,
           rw_r_k, rw_gn_w, rw_gn_b, rw_vres_down, rw_vres_up, rw_vres_bias, rw_out, gla_conv,
           gla_alpha_up, gla_alpha_bias, gla_norm_w, gla_out, merge_out, mlp_w1, mlp_w2, ada_w,
           ada_b, norm_mix_pre, norm_mix_post, norm_ffn_pre, norm_ffn_post):
    p = dict(w_in=w_in, rw_mu=rw_mu, rw_w0=rw_w0, rw_w_up=rw_w_up, rw_a0=rw_a0, rw_a_up=rw_a_up,
             rw_g_up=rw_g_up, rw_k_k=rw_k_k, rw_k_a=rw_k_a, rw_r_k=rw_r_k.reshape(rw_r_k.shape[0], -1),
             rw_gn_w=rw_gn_w, rw_gn_b=rw_gn_b, rw_vres_down=rw_vres_down, rw_vres_up=rw_vres_up,
             rw_vres_bias=rw_vres_bias, rw_out=rw_out, gla_conv=gla_conv, gla_alpha_up=gla_alpha_up,
             gla_alpha_bias=gla_alpha_bias, gla_norm_w=gla_norm_w, gla_out=gla_out,
             merge_out=merge_out, mlp_w1=mlp_w1, mlp_w2=mlp_w2, norm_mix_pre=norm_mix_pre,
             norm_mix_post=norm_mix_post, norm_ffn_pre=norm_ffn_pre, norm_ffn_post=norm_ffn_post)
    bsz, t, d = x.shape
    depth = w_in.shape[0]
    assert bsz < SUBLANES and d % LANES == 0
    assert t % (GRID_W * SUBLANES) == 0 and ctx.shape[1] % CHUNK == 0
    tm = min(TOKEN_TILE, t)
    tm_ctx = min(TOKEN_TILE, ctx.shape[1])
    assert t % tm == 0 and ctx.shape[1] % tm_ctx == 0 and tm % GRID_W == 0

    cc = jnp.concatenate([c, c_ctx[None, :], jnp.zeros((SUBLANES - 1 - bsz, d), F32)], axis=0)
    mods = _ada_mods(cc, ada_w, ada_b).reshape(depth, SUBLANES, 6, d)
    hind = _head_indicator()
    lat_row = lambda b: b
    ctx_row = lambda b: bsz

    x_lat, x_ctx = x, ctx
    vf_lat = vf_ctx = None
    for l in range(depth):
        last = l == depth - 1
        lp = _layer_params(l, p)
        z_rw = jnp.zeros((bsz, 2, RW_HEADS, RW_HEAD_DIM, RW_HEAD_DIM), F32)
        z_gla = jnp.zeros((bsz, 2, GLA_HEADS, GLA_DV, GLA_DK), F32)
        x_ctx_new, v_ctx, s_rw, s_gla = _mixer(
            x_ctx, mods[l], ctx_row, vf_ctx, z_rw, z_gla, lp, hind,
            p_col=None, tm=tm_ctx, need_out=not last)
        x_lat, v_lat, _, _ = _mixer(
            x_lat, mods[l], lat_row, vf_lat, s_rw, s_gla, lp, hind,
            p_col=(l % 2 == 1), tm=tm, need_out=True)
        if l == 0:
            vf_lat, vf_ctx = v_lat, v_ctx
        if not last:
            x_ctx = x_ctx_new
    return x_lat
```

```python
import functools

import jax
import jax.numpy as jnp
from jax import lax
from jax.experimental import pallas as pl
from jax.experimental.pallas import tpu as pltpu

F32 = jnp.float32
BF16 = jnp.bfloat16

GRID_W = 64
RMS_EPS = 1e-6
RW_HEADS = 8
RW_HEAD_DIM = 64
RW_WIDTH = RW_HEADS * RW_HEAD_DIM
RW_GN_EPS = 64e-5
RW_COLS = 1920
GLA_HEADS = 4
GLA_DK = 64
GLA_DV = 128
GLA_KW = GLA_HEADS * GLA_DK
GLA_VW = GLA_HEADS * GLA_DV
GLA_QKV_W = 2 * GLA_KW + GLA_VW
GLA_GATE_RANK = 16
GLA_TAU = 16.0
CHUNK = 64
GLA_BLOCK_CHUNKS = 4
LANES = 128
SUBLANES = 8
HALO = 16
VMEM_LIMIT = 56 * 1024 * 1024
TOKEN_TILE = 1024
MERGE_TILE = 512


def _dot(a, b):
    return jnp.dot(a.astype(BF16), b.astype(BF16), preferred_element_type=F32)


def _dot_nt(a, b):
    return lax.dot_general(a.astype(BF16), b.astype(BF16), (((1,), (1,)), ((), ())),
                           preferred_element_type=F32)


def _dot_tn(a, b):
    return lax.dot_general(a.astype(BF16), b.astype(BF16), (((0,), (0,)), ((), ())),
                           preferred_element_type=F32)


def _split2(x):
    hi = x.astype(BF16)
    lo = (x - hi.astype(F32)).astype(BF16)
    return hi, lo


def _dot_ind_rhs(x, ind):
    hi, lo = _split2(x)
    return (jnp.dot(hi, ind, preferred_element_type=F32)
            + jnp.dot(lo, ind, preferred_element_type=F32))


def _dot_ind_lhs(ind, x):
    hi, lo = _split2(x)
    return (jnp.dot(ind, hi, preferred_element_type=F32)
            + jnp.dot(ind, lo, preferred_element_type=F32))


def _sigmoid(x):
    return jax.nn.sigmoid(x)


def _silu(x):
    return x * jax.nn.sigmoid(x)


def _softplus(z):
    return jnp.maximum(z, 0.0) + jnp.log1p(jnp.exp(-jnp.abs(z)))


def _log_sigmoid(z):
    return -_softplus(-z)


def _rms(x, gain):
    return x * lax.rsqrt(jnp.mean(x * x, axis=-1, keepdims=True) + RMS_EPS) * gain


def _load_tok(ref, transposed, a, k):
    if not transposed:
        return ref[...]
    return jnp.concatenate([ref[:, i, :] for i in range(k)], axis=0)


def _store_tok(ref, val, transposed, a, k):
    val = val.astype(ref.dtype)
    if not transposed:
        ref[...] = val
    else:
        for i in range(k):
            ref[:, i, :] = val[i * a:(i + 1) * a, :]


def _tok_view(arr, transposed, a):
    if not transposed:
        return arr
    b, t, f = arr.shape
    return arr.reshape(b, a, t // a, f)


def _tok_unview(arr, transposed, t, f):
    if not transposed:
        return arr
    return arr.reshape(arr.shape[0], t, f)


def _tok_spec(f, tm, transposed, a):
    if not transposed:
        return pl.BlockSpec((None, tm, f), lambda b, j: (b, j, 0))
    return pl.BlockSpec((None, a, tm // a, f), lambda b, j: (b, 0, j, 0))


def _tok_shape(bsz, t, f, transposed, a, dtype):
    if not transposed:
        return jax.ShapeDtypeStruct((bsz, t, f), dtype)
    return jax.ShapeDtypeStruct((bsz, a, t // a, f), dtype)


def _const_spec(shape):
    nd = len(shape)
    return pl.BlockSpec(shape, lambda *_: (0,) * nd, pipeline_mode=pl.Buffered(1))


def _params(ndim):
    return pltpu.CompilerParams(dimension_semantics=("arbitrary",) * ndim,
                                vmem_limit_bytes=VMEM_LIMIT)


def _halo_specs(f, tm, t):
    sub = tm // HALO
    last = t // HALO - 1
    prev = pl.BlockSpec((None, HALO, f), lambda b, j: (b, jnp.maximum(j * sub - 1, 0), 0))
    nxt = pl.BlockSpec((None, HALO, f), lambda b, j: (b, jnp.minimum((j + 1) * sub, last), 0))
    return prev, nxt


def _shifted(cur, prev_blk, next_blk, first, last):
    tm = cur.shape[0]
    row = lax.broadcasted_iota(jnp.int32, cur.shape, 0)
    p_row = jnp.where(first, 0.0, prev_blk[HALO - 1:HALO, :].astype(F32))
    n_row = jnp.where(last, 0.0, next_blk[0:1, :].astype(F32))
    prev = jnp.where(row == 0, p_row, pltpu.roll(cur, 1, 0))
    nxt = jnp.where(row == tm - 1, n_row, pltpu.roll(cur, tm - 1, 0))
    return prev, nxt


def _ada_kernel(c_ref, w_ref, b_ref, o_ref):
    cc = c_ref[...]
    o_ref[...] = _dot(_silu(cc), w_ref[...]) + b_ref[...]


def _ada_mods(cc, ada_w, ada_b):
    nl, d, n6 = ada_w.shape
    tn = d
    return pl.pallas_call(
        _ada_kernel,
        grid=(nl, n6 // tn),
        in_specs=[pl.BlockSpec((SUBLANES, d), lambda l, n: (0, 0)),
                  pl.BlockSpec((None, d, tn), lambda l, n: (l, 0, n)),
                  pl.BlockSpec((None, 1, tn), lambda l, n: (l, 0, n))],
        out_specs=pl.BlockSpec((None, SUBLANES, tn), lambda l, n: (l, 0, n)),
        out_shape=jax.ShapeDtypeStruct((nl, SUBLANES, n6), F32),
        compiler_params=_params(2),
        name="ada",
    )(cc, ada_w, ada_b.reshape(nl, 1, n6))


def _inproj_kernel(x_ref, mod_ref, gain_ref, w_rw, w_q, w_ald, w_gg, w_mg,
                   o_rw, o_q, o_ald, o_gg, o_mg, *, x_tr, g_tr, a, k):
    x = _load_tok(x_ref, x_tr, a, k)
    h = _rms(x, gain_ref[...]) * (1.0 + mod_ref[1:2, :]) + mod_ref[0:1, :]
    hb = h.astype(BF16)
    proj = lambda w: jnp.dot(hb, w[...], preferred_element_type=F32)
    _store_tok(o_rw, proj(w_rw), False, a, k)
    _store_tok(o_mg, proj(w_mg), False, a, k)
    _store_tok(o_q, proj(w_q), g_tr, a, k)
    _store_tok(o_ald, proj(w_ald), g_tr, a, k)
    _store_tok(o_gg, proj(w_gg), g_tr, a, k)


def _inproj(x, mods, mod_row, gain, ws, *, x_tr, g_tr, a, tm):
    bsz, t, d = x.shape
    k = tm // a
    kern = functools.partial(_inproj_kernel, x_tr=x_tr, g_tr=g_tr, a=a, k=k)
    widths = (RW_COLS, GLA_QKV_W, LANES, GLA_VW, 2 * d)
    out_tr = (False, g_tr, g_tr, g_tr, False)
    outs = pl.pallas_call(
        kern,
        grid=(bsz, t // tm),
        in_specs=[_tok_spec(d, tm, x_tr, a),
                  pl.BlockSpec((None, 6, d), lambda b, j: (mod_row(b), 0, 0)),
                  _const_spec((1, d))] + [_const_spec(w.shape) for w in ws],
        out_specs=[_tok_spec(f, tm, tr, a) for f, tr in zip(widths, out_tr)],
        out_shape=[_tok_shape(bsz, t, f, tr, a, BF16) for f, tr in zip(widths, out_tr)],
        compiler_params=_params(2),
        name="inproj",
    )(_tok_view(x, x_tr, a), mods, gain, *ws)
    return [_tok_unview(o, tr, t, f) for o, f, tr in zip(outs, widths, out_tr)]


def _rwprep_kernel(*refs, has_vres, vf_tr, a, k):
    it = iter(refs)
    f_cur, f_prev, f_next = next(it), next(it), next(it)
    vf_ref = next(it) if has_vres else None
    mu, w0, w_up, a0, a_up, g_up, k_k, k_a, r_k = (next(it) for _ in range(9))
    if has_vres:
        vdown, vup, vbias = next(it), next(it), next(it)
    hind = next(it)
    o_r, o_v, o_kk, o_g, o_gb, o_lw, o_kd, o_bd = (next(it) for _ in range(8))

    j = pl.program_id(1)
    first = j == 0
    last = j == pl.num_programs(1) - 1
    f = f_cur[...].astype(F32)
    prev, nxt = _shifted(f, f_prev[...], f_next[...], first, last)
    fs = f + mu[...] * (0.5 * (prev + nxt) - f)
    w = RW_WIDTH
    r = fs[:, 0:w]
    kx = fs[:, w:2 * w]
    v = fs[:, 2 * w:3 * w]
    wd = fs[:, 3 * w:3 * w + LANES]
    ad = fs[:, 3 * w + LANES:3 * w + 2 * LANES]
    gd = fs[:, 3 * w + 2 * LANES:3 * w + 3 * LANES]
    if has_vres:
        vf = _load_tok(vf_ref, vf_tr, a, k).astype(F32)
        mix = _sigmoid(vbias[...] + _dot(_dot(v, vdown[...]), vup[...]))
        v = v + (vf - v) * mix
    kk = kx * k_k[...]
    kk = kk * lax.rsqrt(_dot_ind_rhs(kk * kk, hind[...]) + 1e-12)
    twd = jnp.tanh(wd)
    ksum = None
    for di in range(2):
        wlog = -_softplus(-(w0[di:di + 1, :] + _dot(twd, w_up[di]))) - 0.5
        o_lw[di] = -jnp.exp(wlog)
        lr = _sigmoid(a0[di:di + 1, :] + _dot(ad, a_up[di]))
        kd = kx * (1.0 + (lr - 1.0) * k_a[...])
        o_kd[di] = kd.astype(o_kd.dtype)
        o_bd[di] = (lr * kk).astype(o_bd.dtype)
        ksum = kd if ksum is None else ksum + kd
    g = _dot(_sigmoid(gd), g_up[...])
    bonus = _dot_ind_rhs(r * ksum * r_k[...], hind[...]) * v
    o_r[...] = r.astype(o_r.dtype)
    o_v[...] = v.astype(o_v.dtype)
    o_kk[...] = kk.astype(o_kk.dtype)
    o_g[...] = g.astype(o_g.dtype)
    o_gb[...] = (bonus * g).astype(o_gb.dtype)


def _rwprep(f, vfirst, lp, hind, *, vf_tr, a, tm):
    bsz, t, fc = f.shape
    w = RW_WIDTH
    has_vres = vfirst is not None
    k = tm // a
    kern = functools.partial(_rwprep_kernel, has_vres=has_vres, vf_tr=vf_tr, a=a, k=k)
    prev_spec, next_spec = _halo_specs(fc, tm, t)
    args = [f, f, f]
    in_specs = [_tok_spec(fc, tm, False, a), prev_spec, next_spec]
    if has_vres:
        args.append(_tok_view(vfirst, vf_tr, a))
        in_specs.append(_tok_spec(w, tm, vf_tr, a))
    small = [lp["rw_mu"], lp["rw_w0"], lp["rw_w_up_pad"], lp["rw_a0"], lp["rw_a_up_pad"],
             lp["rw_g_up"], lp["rw_k_k"], lp["rw_k_a"], lp["rw_r_k"]]
    if has_vres:
        small += [lp["rw_vres_down"], lp["rw_vres_up"], lp["rw_vres_bias"]]
    small.append(hind)
    args += small
    in_specs += [_const_spec(s.shape) for s in small]
    tok = pl.BlockSpec((None, tm, w), lambda b, j: (b, j, 0))
    tok2 = pl.BlockSpec((2, None, tm, w), lambda b, j: (0, b, j, 0))
    s1 = jax.ShapeDtypeStruct((bsz, t, w), BF16)
    s2 = lambda dt: jax.ShapeDtypeStruct((2, bsz, t, w), dt)
    return pl.pallas_call(
        kern,
        grid=(bsz, t // tm),
        in_specs=in_specs,
        out_specs=[tok] * 5 + [tok2] * 3,
        out_shape=[s1] * 5 + [s2(F32), s2(BF16), s2(BF16)],
        compiler_params=_params(2),
        name="rwprep",
    )(*args)


def _before(row, col, reverse, inclusive):
    if reverse:
        return (col >= row) if inclusive else (col > row)
    return (col <= row) if inclusive else (col < row)


def _cum_parts(lw, reverse):
    row = lax.broadcasted_iota(jnp.int32, (CHUNK, CHUNK), 0)
    col = lax.broadcasted_iota(jnp.int32, (CHUNK, CHUNK), 1)
    tri = jnp.where(_before(row, col, reverse, True), 1.0, 0.0).astype(BF16)
    cum = _dot_ind_lhs(tri, lw)
    tot = cum[0:1, :] if reverse else cum[CHUNK - 1:CHUNK, :]
    return cum, tot


def _rwscan_kernel(rf, vf, kkf, lwf, kdf, bdf, rb, vb, kkb, lwb, kdb, bdb, s0_ref,
                   yf_ref, yb_ref, sfin_ref, s_scr):
    c = pl.program_id(1)

    @pl.when(c == 0)
    def _():
        s_scr[...] = s0_ref[...]

    n = RW_HEAD_DIM
    c2 = 2 * CHUNK
    row2 = lax.broadcasted_iota(jnp.int32, (c2, c2), 0)
    col2 = lax.broadcasted_iota(jnp.int32, (c2, c2), 1)
    rp = row2 % CHUNK
    cp = col2 % CHUNK
    bottom = row2 // CHUNK
    row = lax.broadcasted_iota(jnp.int32, (CHUNK, CHUNK), 0)
    col = lax.broadcasted_iota(jnp.int32, (CHUNK, CHUNK), 1)
    eye = jnp.where(row == col, 1.0, 0.0)
    zeros = jnp.zeros((CHUNK, n), BF16)

    chains = []
    for di, (r_ref, v_ref, kk_ref, lw_ref, kd_ref, bd_ref, y_ref) in enumerate(
            ((rf, vf, kkf, lwf, kdf, bdf, yf_ref), (rb, vb, kkb, lwb, kdb, bdb, yb_ref))):
        reverse = di == 1
        lw = lw_ref[...]
        cum, tot = _cum_parts(lw, reverse)
        e_neg = jnp.exp(-cum)
        e_end = jnp.exp(tot - cum)
        kd = kd_ref[...].astype(F32)
        bd = bd_ref[...].astype(F32)
        ops = dict(
            rt=(r_ref[...].astype(F32) * jnp.exp(cum)).astype(BF16),
            at=(kk_ref[...].astype(F32) * jnp.exp(cum - lw)).astype(BF16),
            kt=(kd * e_neg).astype(BF16),
            bt=(bd * e_neg).astype(BF16),
            kh=(kd * e_end).astype(BF16),
            bh=(bd * e_end).astype(BF16),
            v=v_ref[...],
            g_tot=jnp.exp(tot),
        )
        keep = (cp > rp - bottom) if reverse else (cp < rp + bottom)
        for h in range(RW_HEADS):
            sl = slice(h * n, (h + 1) * n)
            ch = {k: val[:, sl] for k, val in ops.items()}
            ch.update(di=di, h=h, sl=sl, keep=keep, y_ref=y_ref)
            chains.append(ch)

    for ch in chains:
        g = _dot_nt(jnp.concatenate([ch["at"], ch["rt"]], axis=0),
                    jnp.concatenate([ch["bt"], ch["kt"]], axis=0))
        g = jnp.where(ch["keep"], g, 0.0).astype(BF16)
        ch["g_top"], ch["g_bot"] = g[:CHUNK], g[CHUNK:]
        ch["l"] = g[:CHUNK, :CHUNK]
    for ch in chains:
        ch["p"] = _dot(ch["l"], ch["l"])
        ch["av"] = _dot(ch["g_top"], jnp.concatenate([zeros, ch["v"]], axis=0))
        ch["kv"] = _dot_tn(ch["v"], ch["kh"])
        ch["x"] = eye - ch["l"].astype(F32)
    for _ in range(4):
        for ch in chains:
            z = _dot(jnp.concatenate([ch["x"], ch["p"]], axis=0), ch["p"])
            ch["x"] = ch["x"] + z[:CHUNK]
            ch["p"] = z[CHUNK:]
    for ch in chains:
        ch["x"] = ch["x"] + _dot(ch["x"], ch["p"])
    for ch in chains:
        ch["wm"] = _dot(ch["x"], ch["at"]).astype(BF16)
        ch["u0"] = -_dot(ch["x"], ch["av"])
    for ch in chains:
        ch["u0t"] = ch["u0"].T
    for ch in chains:
        s = s_scr[ch["di"], ch["h"]]
        ch["s"] = s
        sb = s.astype(BF16)
        pr = _dot_nt(jnp.concatenate([ch["wm"], ch["rt"]], axis=0), sb)
        ch["u"] = ch["u0"] - pr[:CHUNK]
        ch["rs"] = pr[CHUNK:]
        ch["ut"] = ch["u0t"] - _dot_nt(sb, ch["wm"])
    for ch in chains:
        y = ch["rs"] + _dot(ch["g_bot"],
                            jnp.concatenate([ch["u"].astype(BF16), ch["v"]], axis=0))
        ch["y_ref"][:, ch["sl"]] = y.astype(ch["y_ref"].dtype)
        s_scr[ch["di"], ch["h"]] = ch["s"] * ch["g_tot"] + ch["kv"] + _dot(ch["ut"], ch["bh"])

    @pl.when(c == pl.num_programs(1) - 1)
    def _():
        sfin_ref[...] = s_scr[...]


def _rwscan(r, v, kk, lw, kd, bd, s0):
    bsz, t, w = r.shape
    nc = t // CHUNK
    tok_f = pl.BlockSpec((None, CHUNK, w), lambda b, c: (b, c, 0))
    tok_b = pl.BlockSpec((None, CHUNK, w), lambda b, c: (b, nc - 1 - c, 0))
    dir_f = pl.BlockSpec((None, None, CHUNK, w), lambda b, c: (0, b, c, 0))
    dir_b = pl.BlockSpec((None, None, CHUNK, w), lambda b, c: (1, b, nc - 1 - c, 0))
    st = pl.BlockSpec((None, 2, RW_HEADS, RW_HEAD_DIM, RW_HEAD_DIM), lambda b, c: (b, 0, 0, 0, 0))
    y_shape = jax.ShapeDtypeStruct((bsz, t, w), BF16)
    yf, yb, sfin = pl.pallas_call(
        _rwscan_kernel,
        grid=(bsz, nc),
        in_specs=[tok_f, tok_f, tok_f, dir_f, dir_f, dir_f,
                  tok_b, tok_b, tok_b, dir_b, dir_b, dir_b, st],
        out_specs=[tok_f, tok_b, st],
        out_shape=[y_shape, y_shape, jax.ShapeDtypeStruct(s0.shape, F32)],
        scratch_shapes=[pltpu.VMEM((2, RW_HEADS, RW_HEAD_DIM, RW_HEAD_DIM), F32)],
        compiler_params=_params(2),
        name="rwscan",
    )(r, v, kk, lw, kd, bd, r, v, kk, lw, kd, bd, s0)
    return yf, yb, sfin


def _rwpost_kernel(yf_ref, yb_ref, g_ref, gb_ref, gnw, gnb, hind, o_ref):
    y = yf_ref[...].astype(F32) + yb_ref[...].astype(F32)
    inv_n = 1.0 / RW_HEAD_DIM
    mean = _dot_ind_rhs(y, hind[...]) * inv_n
    yc = y - mean
    var = _dot_ind_rhs(yc * yc, hind[...]) * inv_n
    yn = yc * lax.rsqrt(var + RW_GN_EPS) * gnw[...] + gnb[...]
    o_ref[...] = (yn * g_ref[...].astype(F32) + gb_ref[...].astype(F32)).astype(o_ref.dtype)


def _rwpost(yf, yb, g, gb, lp, hind, *, tm):
    bsz, t, w = yf.shape
    tok = pl.BlockSpec((None, tm, w), lambda b, j: (b, j, 0))
    return pl.pallas_call(
        _rwpost_kernel,
        grid=(bsz, t // tm),
        in_specs=[tok, tok, tok, tok, _const_spec((1, w)), _const_spec((1, w)),
                  _const_spec(hind.shape)],
        out_specs=tok,
        out_shape=jax.ShapeDtypeStruct((bsz, t, w), BF16),
        compiler_params=_params(2),
        name="rwpost",
    )(yf, yb, g, gb, lp["rw_gn_w"], lp["rw_gn_b"], hind)


def _glaprep_kernel(q_cur, q_prev, q_next, ald_ref, conv, up, bias, o_q, o_k, o_v, o_la):
    j = pl.program_id(1)
    x = q_cur[...].astype(F32)
    prev, nxt = _shifted(x, q_prev[...], q_next[...], j == 0, j == pl.num_programs(1) - 1)
    y = _silu(conv[0:1, :] * prev + conv[1:2, :] * x + conv[2:3, :] * nxt)
    o_q[...] = (y[:, 0:GLA_KW] * (GLA_DK ** -0.5)).astype(o_q.dtype)
    o_k[...] = y[:, GLA_KW:2 * GLA_KW].astype(o_k.dtype)
    o_v[...] = y[:, 2 * GLA_KW:].astype(o_v.dtype)
    ald = ald_ref[...]
    for di in range(2):
        o_la[di] = _log_sigmoid(_dot(ald, up[di]) + bias[di:di + 1, :]) / GLA_TAU


def _glaprep(qkv, ald, lp, *, tm):
    bsz, t, fq = qkv.shape
    prev_spec, next_spec = _halo_specs(fq, tm, t)
    tok = lambda f: pl.BlockSpec((None, tm, f), lambda b, j: (b, j, 0))
    small = [lp["gla_conv"], lp["gla_alpha_up_pad"], lp["gla_alpha_bias"]]
    return pl.pallas_call(
        _glaprep_kernel,
        grid=(bsz, t // tm),
        in_specs=[tok(fq), prev_spec, next_spec, tok(LANES)] + [_const_spec(s.shape) for s in small],
        out_specs=[tok(GLA_KW), tok(GLA_KW), tok(GLA_VW),
                   pl.BlockSpec((2, None, tm, GLA_KW), lambda b, j: (0, b, j, 0))],
        out_shape=[jax.ShapeDtypeStruct((bsz, t, GLA_KW), BF16),
                   jax.ShapeDtypeStruct((bsz, t, GLA_KW), BF16),
                   jax.ShapeDtypeStruct((bsz, t, GLA_VW), BF16),
                   jax.ShapeDtypeStruct((2, bsz, t, GLA_KW), F32)],
        compiler_params=_params(2),
        name="glaprep",
    )(qkv, qkv, qkv, ald, *small)


def _glascan_kernel(qf, kf, vf, laf, qb, kb, vb, lab, s0_ref, of_ref, ob_ref, sfin_ref, s_scr,
                    *, nb):
    j = pl.program_id(1)

    @pl.when(j == 0)
    def _():
        s_scr[...] = s0_ref[...]

    row = lax.broadcasted_iota(jnp.int32, (CHUNK, CHUNK), 0)
    col = lax.broadcasted_iota(jnp.int32, (CHUNK, CHUNK), 1)
    chains = []
    for di, (q_ref, k_ref, v_ref, la_ref, o_ref) in enumerate(
            ((qf, kf, vf, laf, of_ref), (qb, kb, vb, lab, ob_ref))):
        reverse = di == 1
        keep = _before(row, col, reverse, True)
        for ci in (reversed(range(nb)) if reverse else range(nb)):
            rows = slice(ci * CHUNK, (ci + 1) * CHUNK)
            cum, tot = _cum_parts(la_ref[rows, :], reverse)
            k = k_ref[rows, :].astype(F32)
            q_dec = (q_ref[rows, :].astype(F32) * jnp.exp(cum)).astype(BF16)
            k_inv = (k * jnp.exp(-cum)).astype(BF16)
            k_end = (k * jnp.exp(tot - cum)).astype(BF16)
            dec = jnp.exp(tot)
            v = v_ref[rows, :]
            for h in range(GLA_HEADS):
                sk = slice(h * GLA_DK, (h + 1) * GLA_DK)
                sv = slice(h * GLA_DV, (h + 1) * GLA_DV)
                chains.append(dict(di=di, h=h, rows=rows, sv=sv, keep=keep, o_ref=o_ref,
                                   q=q_dec[:, sk], ki=k_inv[:, sk], ke=k_end[:, sk],
                                   dec=dec[:, sk], v=v[:, sv]))
    for ch in chains:
        ch["sc"] = jnp.where(ch["keep"], _dot_nt(ch["q"], ch["ki"]), 0.0)
        ch["kv"] = _dot_tn(ch["v"], ch["ke"])
    state = {}
    for ch in chains:
        key = (ch["di"], ch["h"])
        s = state[key] if key in state else s_scr[ch["di"], ch["h"]]
        ch["s"] = s
        state[key] = s * ch["dec"] + ch["kv"]
    for (di, h), s in state.items():
        s_scr[di, h] = s
    for ch in chains:
        o = _dot(ch["sc"], ch["v"]) + _dot_nt(ch["q"], ch["s"])
        ch["o_ref"][ch["rows"], ch["sv"]] = o.astype(ch["o_ref"].dtype)

    @pl.when(j == pl.num_programs(1) - 1)
    def _():
        sfin_ref[...] = s_scr[...]


def _glascan(q, k, v, la, s0):
    bsz, t, _ = q.shape
    nb = GLA_BLOCK_CHUNKS
    tb = nb * CHUNK
    nblk = t // tb
    tok_f = lambda f: pl.BlockSpec((None, tb, f), lambda b, j: (b, j, 0))
    tok_b = lambda f: pl.BlockSpec((None, tb, f), lambda b, j: (b, nblk - 1 - j, 0))
    dir_f = pl.BlockSpec((None, None, tb, GLA_KW), lambda b, j: (0, b, j, 0))
    dir_b = pl.BlockSpec((None, None, tb, GLA_KW), lambda b, j: (1, b, nblk - 1 - j, 0))
    st = pl.BlockSpec((None, 2, GLA_HEADS, GLA_DV, GLA_DK), lambda b, j: (b, 0, 0, 0, 0))
    o_shape = jax.ShapeDtypeStruct((bsz, t, GLA_VW), BF16)
    return pl.pallas_call(
        functools.partial(_glascan_kernel, nb=nb),
        grid=(bsz, nblk),
        in_specs=[tok_f(GLA_KW), tok_f(GLA_KW), tok_f(GLA_VW), dir_f,
                  tok_b(GLA_KW), tok_b(GLA_KW), tok_b(GLA_VW), dir_b, st],
        out_specs=[tok_f(GLA_VW), tok_b(GLA_VW), st],
        out_shape=[o_shape, o_shape, jax.ShapeDtypeStruct(s0.shape, F32)],
        scratch_shapes=[pltpu.VMEM((2, GLA_HEADS, GLA_DV, GLA_DK), F32)],
        compiler_params=_params(2),
        name="glascan",
    )(q, k, v, la, q, k, v, la, s0)


def _glapost_kernel(of_ref, ob_ref, gate_ref, nw, y_ref, *, y_tr, a, k):
    o = of_ref[...].astype(F32) + ob_ref[...].astype(F32)
    gate = gate_ref[...].astype(F32)
    ys = []
    for h in range(GLA_HEADS):
        sv = slice(h * GLA_DV, (h + 1) * GLA_DV)
        ys.append(_rms(o[:, sv], nw[...]) * _silu(gate[:, sv]))
    _store_tok(y_ref, jnp.concatenate(ys, axis=1), y_tr, a, k)


def _glapost(of, ob, gate, lp, *, y_tr, a, tm):
    bsz, t, w = of.shape
    k = tm // a
    tok = pl.BlockSpec((None, tm, w), lambda b, j: (b, j, 0))
    out = pl.pallas_call(
        functools.partial(_glapost_kernel, y_tr=y_tr, a=a, k=k),
        grid=(bsz, t // tm),
        in_specs=[tok, tok, tok, _const_spec((1, GLA_DV))],
        out_specs=_tok_spec(w, tm, y_tr, a),
        out_shape=_tok_shape(bsz, t, w, y_tr, a, BF16),
        compiler_params=_params(2),
        name="glapost",
    )(of, ob, gate, lp["gla_norm_w"])
    return _tok_unview(out, y_tr, t, w)


def _merge_kernel(x_ref, yrw_ref, ygla_ref, mg_ref, mod_ref, gains, w_rwo, w_glao, w_mo, w1, w2,
                  o_ref, *, x_tr, a, k):
    d = x_ref.shape[-1]
    x = _load_tok(x_ref, x_tr, a, k)
    mg = mg_ref[...].astype(F32)
    br = (_sigmoid(mg[:, :d]) * _dot(yrw_ref[...], w_rwo[...])
          + _sigmoid(mg[:, d:]) * _dot(ygla_ref[...], w_glao[...]))
    m = _dot(br, w_mo[...])
    x1 = x + mod_ref[2:3, :] * _rms(m, gains[1:2, :])
    h2 = _rms(x1, gains[2:3, :]) * (1.0 + mod_ref[4:5, :]) + mod_ref[3:4, :]
    hid = jnp.maximum(_dot(h2, w1[...]), 0.0)
    f = _dot(hid * hid, w2[...])
    x2 = x1 + mod_ref[5:6, :] * _rms(f, gains[3:4, :])
    _store_tok(o_ref, x2, x_tr, a, k)


def _merge(x, yrw, ygla, mg, mods, mod_row, gains, ws, *, x_tr, a, tm):
    bsz, t, d = x.shape
    k = tm // a
    kern = functools.partial(_merge_kernel, x_tr=x_tr, a=a, k=k)
    out = pl.pallas_call(
        kern,
        grid=(bsz, t // tm),
        in_specs=[_tok_spec(d, tm, x_tr, a), _tok_spec(RW_WIDTH, tm, False, a),
                  _tok_spec(GLA_VW, tm, False, a), _tok_spec(2 * d, tm, False, a),
                  pl.BlockSpec((None, 6, d), lambda b, j: (mod_row(b), 0, 0)),
                  _const_spec(gains.shape)] + [_const_spec(w.shape) for w in ws],
        out_specs=_tok_spec(d, tm, x_tr, a),
        out_shape=_tok_shape(bsz, t, d, x_tr, a, F32),
        compiler_params=_params(2),
        name="merge",
    )(_tok_view(x, x_tr, a), yrw, ygla, mg, mods, gains, *ws)
    return _tok_unview(out, x_tr, t, d)


def _layer_params(l, p):
    w_in = p["w_in"][l]
    g0 = RW_COLS
    g1 = g0 + GLA_QKV_W
    g2 = g1 + 2 * GLA_GATE_RANK
    g3 = g2 + GLA_VW
    bf = lambda w: w.astype(BF16)
    row = lambda w: w.reshape(1, -1)
    lp = {
        "w_in_parts": [
            bf(w_in[:, :g0]),
            bf(w_in[:, g0:g1]),
            bf(jnp.pad(w_in[:, g1:g2], ((0, 0), (0, LANES - 2 * GLA_GATE_RANK)))),
            bf(w_in[:, g2:g3]),
            bf(w_in[:, g3:]),
        ],
        "rw_mu": row(p["rw_mu"][l]),
        "rw_w0": p["rw_w0"][l],
        "rw_a0": p["rw_a0"][l],
        "rw_g_up": bf(p["rw_g_up"][l]),
        "rw_k_k": row(p["rw_k_k"][l]),
        "rw_k_a": row(p["rw_k_a"][l]),
        "rw_r_k": row(p["rw_r_k"][l]),
        "rw_gn_w": row(p["rw_gn_w"][l]),
        "rw_gn_b": row(p["rw_gn_b"][l]),
        "gla_conv": p["gla_conv"][l],
        "gla_alpha_bias": p["gla_alpha_bias"][l],
        "gla_norm_w": row(p["gla_norm_w"][l]),
        "merge_ws": [bf(p["rw_out"][l]), bf(p["gla_out"][l]), bf(p["merge_out"][l]),
                     bf(p["mlp_w1"][l]), bf(p["mlp_w2"][l])],
        "gains": jnp.stack([p["norm_mix_pre"][l], p["norm_mix_post"][l],
                            p["norm_ffn_pre"][l], p["norm_ffn_post"][l]]),
    }
    pad_dir = lambda w, r: jnp.stack([jnp.pad(w[di], ((di * r, LANES - (di + 1) * r), (0, 0)))
                                      for di in range(2)])
    lp["rw_w_up_pad"] = bf(pad_dir(p["rw_w_up"][l], p["rw_w_up"].shape[2]))
    lp["rw_a_up_pad"] = bf(pad_dir(p["rw_a_up"][l], p["rw_a_up"].shape[2]))
    lp["gla_alpha_up_pad"] = bf(pad_dir(p["gla_alpha_up"][l], GLA_GATE_RANK))
    if l > 0:
        lp["rw_vres_down"] = bf(p["rw_vres_down"][l - 1])
        lp["rw_vres_up"] = bf(p["rw_vres_up"][l - 1])
        lp["rw_vres_bias"] = row(p["rw_vres_bias"][l - 1])
    return lp


def _head_indicator():
    h = jnp.arange(RW_WIDTH) // RW_HEAD_DIM
    return (h[:, None] == h[None, :]).astype(BF16)


def _mixer(x, mods, mod_row, vfirst, s_rw, s_gla, lp, hind, *, p_col, need_out):
    bsz, t, d = x.shape
    tm = min(TOKEN_TILE, t)
    tm_merge = min(MERGE_TILE, t)
    if p_col is None:
        x_tr, g_tr, a_p, a_q = False, False, SUBLANES, SUBLANES
    else:
        x_tr, g_tr = p_col, True
        rows = t // GRID_W
        a_p, a_q = (rows, GRID_W) if p_col else (GRID_W, rows)
    rw, gq, gald, gg, mg = _inproj(x, mods, mod_row, lp["gains"][0:1], lp["w_in_parts"],
                                   x_tr=x_tr, g_tr=g_tr, a=a_p, tm=tm)
    r, v, kk, g, gb, lw, kd, bd = _rwprep(rw, vfirst, lp, hind, vf_tr=x_tr, a=a_p, tm=tm)
    yf, yb, s_rw_out = _rwscan(r, v, kk, lw, kd, bd, s_rw)
    q, k, gv, la = _glaprep(gq, gald, lp, tm=tm)
    of, ob, s_gla_out = _glascan(q, k, gv, la, s_gla)
    if not need_out:
        return None, v, s_rw_out, s_gla_out
    yrw = _rwpost(yf, yb, g, gb, lp, hind, tm=tm)
    ygla = _glapost(of, ob, gg, lp, y_tr=g_tr, a=a_q, tm=tm)
    x_new = _merge(x, yrw, ygla, mg, mods, mod_row, lp["gains"], lp["merge_ws"],
                   x_tr=x_tr, a=a_p, tm=tm_merge)
    return x_new, v, s_rw_out, s_gla_out


def kernel(x, c, ctx, c_ctx, w_in, rw_mu, rw_w0, rw_w_up, rw_a0, rw_a_up, rw_g_up, rw_k_k, rw_k_a,
           rw_r_k, rw_gn_w, rw_gn_b, rw_vres_down, rw_vres_up, rw_vres_bias, rw_out, gla_conv,
           gla_alpha_up, gla_alpha_bias, gla_norm_w, gla_out, merge_out, mlp_w1, mlp_w2, ada_w,
           ada_b, norm_mix_pre, norm_mix_post, norm_ffn_pre, norm_ffn_post):
    p = dict(w_in=w_in, rw_mu=rw_mu, rw_w0=rw_w0, rw_w_up=rw_w_up, rw_a0=rw_a0, rw_a_up=rw_a_up,
             rw_g_up=rw_g_up, rw_k_k=rw_k_k, rw_k_a=rw_k_a, rw_r_k=rw_r_k.reshape(rw_r_k.shape[0], -1),
             rw_gn_w=rw_gn_w, rw_gn_b=rw_gn_b, rw_vres_down=rw_vres_down, rw_vres_up=rw_vres_up,
             rw_vres_bias=rw_vres_bias, rw_out=rw_out, gla_conv=gla_conv, gla_alpha_up=gla_alpha_up,
             gla_alpha_bias=gla_alpha_bias, gla_norm_w=gla_norm_w, gla_out=gla_out,
             merge_out=merge_out, mlp_w1=mlp_w1, mlp_w2=mlp_w2, norm_mix_pre=norm_mix_pre,
             norm_mix_post=norm_mix_post, norm_ffn_pre=norm_ffn_pre, norm_ffn_post=norm_ffn_post)
    bsz, t, d = x.shape
    t_ctx = ctx.shape[1]
    depth = w_in.shape[0]
    block = GLA_BLOCK_CHUNKS * CHUNK
    assert bsz < SUBLANES and d % LANES == 0
    assert t % (GRID_W * SUBLANES) == 0 and t % block == 0 and t_ctx % block == 0
    assert t % min(TOKEN_TILE, t) == 0 and t_ctx % min(TOKEN_TILE, t_ctx) == 0

    cc = jnp.concatenate([c, c_ctx[None, :], jnp.zeros((SUBLANES - 1 - bsz, d), F32)], axis=0)
    mods = _ada_mods(cc, ada_w, ada_b).reshape(depth, SUBLANES, 6, d)
    hind = _head_indicator()
    lat_row = lambda b: b
    ctx_row = lambda b: bsz

    x_lat, x_ctx = x, ctx
    vf_lat = vf_ctx = None
    for l in range(depth):
        last = l == depth - 1
        lp = _layer_params(l, p)
        z_rw = jnp.zeros((bsz, 2, RW_HEADS, RW_HEAD_DIM, RW_HEAD_DIM), F32)
        z_gla = jnp.zeros((bsz, 2, GLA_HEADS, GLA_DV, GLA_DK), F32)
        x_ctx_new, v_ctx, s_rw, s_gla = _mixer(
            x_ctx, mods[l], ctx_row, vf_ctx, z_rw, z_gla, lp, hind, p_col=None, need_out=not last)
        x_lat, v_lat, _, _ = _mixer(
            x_lat, mods[l], lat_row, vf_lat, s_rw, s_gla, lp, hind,
            p_col=(l % 2 == 1), need_out=True)
        if l == 0:
            vf_lat, vf_ctx = v_lat, v_ctx
        if not last:
            x_ctx = x_ctx_new
    return x_lat
```

```python
import functools

import jax
import jax.numpy as jnp
from jax import lax
from jax.experimental import pallas as pl
from jax.experimental.pallas import tpu as pltpu

F32 = jnp.float32
BF16 = jnp.bfloat16

GRID_W = 64
RMS_EPS = 1e-6
RW_HEADS = 8
RW_HEAD_DIM = 64
RW_WIDTH = RW_HEADS * RW_HEAD_DIM
RW_GN_EPS = 64e-5
RW_COLS = 1920
GLA_HEADS = 4
GLA_DK = 64
GLA_DV = 128
GLA_KW = GLA_HEADS * GLA_DK
GLA_VW = GLA_HEADS * GLA_DV
GLA_QKV_W = 2 * GLA_KW + GLA_VW
GLA_GATE_RANK = 16
GLA_TAU = 16.0
CHUNK = 64
GLA_BLOCK_CHUNKS = 4
LANES = 128
SUBLANES = 8
HALO = 16
VMEM_LIMIT = 56 * 1024 * 1024
TOKEN_TILE = 1024
MERGE_TILE = 512


def _dot(a, b):
    return jnp.dot(a.astype(BF16), b.astype(BF16), preferred_element_type=F32)


def _dot_nt(a, b):
    return lax.dot_general(a.astype(BF16), b.astype(BF16), (((1,), (1,)), ((), ())),
                           preferred_element_type=F32)


def _dot_tn(a, b):
    return lax.dot_general(a.astype(BF16), b.astype(BF16), (((0,), (0,)), ((), ())),
                           preferred_element_type=F32)


def _split2(x):
    hi = x.astype(BF16)
    lo = (x - hi.astype(F32)).astype(BF16)
    return hi, lo


def _dot_ind_rhs(x, ind):
    hi, lo = _split2(x)
    return (jnp.dot(hi, ind, preferred_element_type=F32)
            + jnp.dot(lo, ind, preferred_element_type=F32))


def _dot_ind_lhs(ind, x):
    hi, lo = _split2(x)
    return (jnp.dot(ind, hi, preferred_element_type=F32)
            + jnp.dot(ind, lo, preferred_element_type=F32))


def _sigmoid(x):
    return jax.nn.sigmoid(x)


def _silu(x):
    return x * jax.nn.sigmoid(x)


def _softplus(z):
    return jnp.maximum(z, 0.0) + jnp.log1p(jnp.exp(-jnp.abs(z)))


def _log_sigmoid(z):
    return -_softplus(-z)


def _rms(x, gain):
    return x * lax.rsqrt(jnp.mean(x * x, axis=-1, keepdims=True) + RMS_EPS) * gain


def _load_tok(ref, transposed, a, k):
    if not transposed:
        return ref[...]
    return jnp.concatenate([ref[:, i, :] for i in range(k)], axis=0)


def _store_tok(ref, val, transposed, a, k):
    val = val.astype(ref.dtype)
    if not transposed:
        ref[...] = val
    else:
        for i in range(k):
            ref[:, i, :] = val[i * a:(i + 1) * a, :]


def _tok_view(arr, transposed, a):
    if not transposed:
        return arr
    b, t, f = arr.shape
    return arr.reshape(b, a, t // a, f)


def _tok_unview(arr, transposed, t, f):
    if not transposed:
        return arr
    return arr.reshape(arr.shape[0], t, f)


def _tok_spec(f, tm, transposed, a):
    if not transposed:
        return pl.BlockSpec((None, tm, f), lambda b, j: (b, j, 0))
    return pl.BlockSpec((None, a, tm // a, f), lambda b, j: (b, 0, j, 0))


def _tok_shape(bsz, t, f, transposed, a, dtype):
    if not transposed:
        return jax.ShapeDtypeStruct((bsz, t, f), dtype)
    return jax.ShapeDtypeStruct((bsz, a, t // a, f), dtype)


def _const_spec(shape):
    nd = len(shape)
    return pl.BlockSpec(shape, lambda *_: (0,) * nd, pipeline_mode=pl.Buffered(1))


def _params(ndim):
    return pltpu.CompilerParams(dimension_semantics=("arbitrary",) * ndim,
                                vmem_limit_bytes=VMEM_LIMIT)


def _halo_specs(f, tm, t):
    sub = tm // HALO
    last = t // HALO - 1
    prev = pl.BlockSpec((None, HALO, f), lambda b, j: (b, jnp.maximum(j * sub - 1, 0), 0))
    nxt = pl.BlockSpec((None, HALO, f), lambda b, j: (b, jnp.minimum((j + 1) * sub, last), 0))
    return prev, nxt


def _shifted(cur, prev_blk, next_blk, first, last):
    tm = cur.shape[0]
    row = lax.broadcasted_iota(jnp.int32, cur.shape, 0)
    p_row = jnp.where(first, 0.0, prev_blk[HALO - 1:HALO, :].astype(F32))
    n_row = jnp.where(last, 0.0, next_blk[0:1, :].astype(F32))
    prev = jnp.where(row == 0, p_row, pltpu.roll(cur, 1, 0))
    nxt = jnp.where(row == tm - 1, n_row, pltpu.roll(cur, tm - 1, 0))
    return prev, nxt


def _ada_kernel(c_ref, w_ref, b_ref, o_ref):
    cc = c_ref[...]
    o_ref[...] = _dot(_silu(cc), w_ref[...]) + b_ref[...]


def _ada_mods(cc, ada_w, ada_b):
    nl, d, n6 = ada_w.shape
    tn = d
    return pl.pallas_call(
        _ada_kernel,
        grid=(nl, n6 // tn),
        in_specs=[pl.BlockSpec((SUBLANES, d), lambda l, n: (0, 0)),
                  pl.BlockSpec((None, d, tn), lambda l, n: (l, 0, n)),
                  pl.BlockSpec((None, 1, tn), lambda l, n: (l, 0, n))],
        out_specs=pl.BlockSpec((None, SUBLANES, tn), lambda l, n: (l, 0, n)),
        out_shape=jax.ShapeDtypeStruct((nl, SUBLANES, n6), F32),
        compiler_params=_params(2),
        name="ada",
    )(cc, ada_w, ada_b.reshape(nl, 1, n6))


def _inproj_kernel(x_ref, mod_ref, gain_ref, w_rw, w_q, w_ald, w_gg, w_mg,
                   o_rw, o_q, o_ald, o_gg, o_mg, *, x_tr, g_tr, a, k):
    x = _load_tok(x_ref, x_tr, a, k)
    h = _rms(x, gain_ref[...]) * (1.0 + mod_ref[1:2, :]) + mod_ref[0:1, :]
    hb = h.astype(BF16)
    proj = lambda w: jnp.dot(hb, w[...], preferred_element_type=F32)
    _store_tok(o_rw, proj(w_rw), False, a, k)
    _store_tok(o_mg, proj(w_mg), False, a, k)
    _store_tok(o_q, proj(w_q), g_tr, a, k)
    _store_tok(o_ald, proj(w_ald), g_tr, a, k)
    _store_tok(o_gg, proj(w_gg), g_tr, a, k)


def _inproj(x, mods, mod_row, gain, ws, *, x_tr, g_tr, a, tm):
    bsz, t, d = x.shape
    k = tm // a
    kern = functools.partial(_inproj_kernel, x_tr=x_tr, g_tr=g_tr, a=a, k=k)
    widths = (RW_COLS, GLA_QKV_W, LANES, GLA_VW, 2 * d)
    out_tr = (False, g_tr, g_tr, g_tr, False)
    outs = pl.pallas_call(
        kern,
        grid=(bsz, t // tm),
        in_specs=[_tok_spec(d, tm, x_tr, a),
                  pl.BlockSpec((None, 6, d), lambda b, j: (mod_row(b), 0, 0)),
                  _const_spec((1, d))] + [_const_spec(w.shape) for w in ws],
        out_specs=[_tok_spec(f, tm, tr, a) for f, tr in zip(widths, out_tr)],
        out_shape=[_tok_shape(bsz, t, f, tr, a, BF16) for f, tr in zip(widths, out_tr)],
        compiler_params=_params(2),
        name="inproj",
    )(_tok_view(x, x_tr, a), mods, gain, *ws)
    return [_tok_unview(o, tr, t, f) for o, f, tr in zip(outs, widths, out_tr)]


def _rwprep_kernel(*refs, has_vres, vf_tr, a, k):
    it = iter(refs)
    f_cur, f_prev, f_next = next(it), next(it), next(it)
    vf_ref = next(it) if has_vres else None
    mu, w0, w_up, a0, a_up, g_up, k_k, k_a, r_k = (next(it) for _ in range(9))
    if has_vres:
        vdown, vup, vbias = next(it), next(it), next(it)
    hind = next(it)
    o_r, o_v, o_kk, o_g, o_gb, o_lw, o_kd, o_bd = (next(it) for _ in range(8))

    j = pl.program_id(1)
    first = j == 0
    last = j == pl.num_programs(1) - 1
    f = f_cur[...].astype(F32)
    prev, nxt = _shifted(f, f_prev[...], f_next[...], first, last)
    fs = f + mu[...] * (0.5 * (prev + nxt) - f)
    w = RW_WIDTH
    r = fs[:, 0:w]
    kx = fs[:, w:2 * w]
    v = fs[:, 2 * w:3 * w]
    wd = fs[:, 3 * w:3 * w + LANES]
    ad = fs[:, 3 * w + LANES:3 * w + 2 * LANES]
    gd = fs[:, 3 * w + 2 * LANES:3 * w + 3 * LANES]
    if has_vres:
        vf = _load_tok(vf_ref, vf_tr, a, k).astype(F32)
        mix = _sigmoid(vbias[...] + _dot(_dot(v, vdown[...]), vup[...]))
        v = v + (vf - v) * mix
    kk = kx * k_k[...]
    kk = kk * lax.rsqrt(_dot_ind_rhs(kk * kk, hind[...]) + 1e-12)
    twd = jnp.tanh(wd)
    ksum = None
    for di in range(2):
        wlog = -_softplus(-(w0[di:di + 1, :] + _dot(twd, w_up[di]))) - 0.5
        o_lw[di] = -jnp.exp(wlog)
        lr = _sigmoid(a0[di:di + 1, :] + _dot(ad, a_up[di]))
        kd = kx * (1.0 + (lr - 1.0) * k_a[...])
        o_kd[di] = kd.astype(o_kd.dtype)
        o_bd[di] = (lr * kk).astype(o_bd.dtype)
        ksum = kd if ksum is None else ksum + kd
    g = _dot(_sigmoid(gd), g_up[...])
    bonus = _dot_ind_rhs(r * ksum * r_k[...], hind[...]) * v
    o_r[...] = r.astype(o_r.dtype)
    o_v[...] = v.astype(o_v.dtype)
    o_kk[...] = kk.astype(o_kk.dtype)
    o_g[...] = g.astype(o_g.dtype)
    o_gb[...] = (bonus * g).astype(o_gb.dtype)


def _rwprep(f, vfirst, lp, hind, *, vf_tr, a, tm):
    bsz, t, fc = f.shape
    w = RW_WIDTH
    has_vres = vfirst is not None
    k = tm // a
    kern = functools.partial(_rwprep_kernel, has_vres=has_vres, vf_tr=vf_tr, a=a, k=k)
    prev_spec, next_spec = _halo_specs(fc, tm, t)
    args = [f, f, f]
    in_specs = [_tok_spec(fc, tm, False, a), prev_spec, next_spec]
    if has_vres:
        args.append(_tok_view(vfirst, vf_tr, a))
        in_specs.append(_tok_spec(w, tm, vf_tr, a))
    small = [lp["rw_mu"], lp["rw_w0"], lp["rw_w_up_pad"], lp["rw_a0"], lp["rw_a_up_pad"],
             lp["rw_g_up"], lp["rw_k_k"], lp["rw_k_a"], lp["rw_r_k"]]
    if has_vres:
        small += [lp["rw_vres_down"], lp["rw_vres_up"], lp["rw_vres_bias"]]
    small.append(hind)
    args += small
    in_specs += [_const_spec(s.shape) for s in small]
    tok = pl.BlockSpec((None, tm, w), lambda b, j: (b, j, 0))
    tok2 = pl.BlockSpec((2, None, tm, w), lambda b, j: (0, b, j, 0))
    s1 = jax.ShapeDtypeStruct((bsz, t, w), BF16)
    s2 = lambda dt: jax.ShapeDtypeStruct((2, bsz, t, w), dt)
    return pl.pallas_call(
        kern,
        grid=(bsz, t // tm),
        in_specs=in_specs,
        out_specs=[tok] * 5 + [tok2] * 3,
        out_shape=[s1] * 5 + [s2(F32), s2(BF16), s2(BF16)],
        compiler_params=_params(2),
        name="rwprep",
    )(*args)


def _before(row, col, reverse, inclusive):
    if reverse:
        return (col >= row) if inclusive else (col > row)
    return (col <= row) if inclusive else (col < row)


def _cum_parts(lw, reverse):
    row = lax.broadcasted_iota(jnp.int32, (CHUNK, CHUNK), 0)
    col = lax.broadcasted_iota(jnp.int32, (CHUNK, CHUNK), 1)
    tri = jnp.where(_before(row, col, reverse, True), 1.0, 0.0).astype(BF16)
    cum = _dot_ind_lhs(tri, lw)
    tot = cum[0:1, :] if reverse else cum[CHUNK - 1:CHUNK, :]
    return cum, tot


def _rwscan_kernel(rf, vf, kkf, lwf, kdf, bdf, rb, vb, kkb, lwb, kdb, bdb, s0_ref,
                   yf_ref, yb_ref, sfin_ref, s_scr):
    c = pl.program_id(1)

    @pl.when(c == 0)
    def _():
        s_scr[...] = s0_ref[...]

    n = RW_HEAD_DIM
    pw = 2 * n
    lane = lax.broadcasted_iota(jnp.int32, (CHUNK, pw), 1)
    lo = lane < n
    row = lax.broadcasted_iota(jnp.int32, (CHUNK, pw), 0)
    eye2 = jnp.where(row == lane % n, 1.0, 0.0)
    grow = lax.broadcasted_iota(jnp.int32, (2 * CHUNK, 2 * pw), 0)
    gcol = lax.broadcasted_iota(jnp.int32, (2 * CHUNK, 2 * pw), 1)
    rp = grow % CHUNK
    cp = gcol % CHUNK
    bottom = grow // CHUNK

    def halves(z):
        zb = z.astype(BF16)
        zero = jnp.zeros_like(zb)
        return jnp.where(lo, zb, zero), jnp.where(lo, zero, zb)

    def bd(z):
        return jnp.concatenate(halves(z), axis=0)

    def pick(z):
        return jnp.where(lo, z[:n], z[n:])

    chains = []
    for di, (r_ref, v_ref, kk_ref, lw_ref, kd_ref, bd_ref, y_ref) in enumerate(
            ((rf, vf, kkf, lwf, kdf, bdf, yf_ref), (rb, vb, kkb, lwb, kdb, bdb, yb_ref))):
        reverse = di == 1
        lw = lw_ref[...]
        cum, tot = _cum_parts(lw, reverse)
        e_neg = jnp.exp(-cum)
        e_end = jnp.exp(tot - cum)
        kd = kd_ref[...].astype(F32)
        bdv = bd_ref[...].astype(F32)
        ops = dict(
            rt=(r_ref[...].astype(F32) * jnp.exp(cum)).astype(BF16),
            at=(kk_ref[...].astype(F32) * jnp.exp(cum - lw)).astype(BF16),
            kt=(kd * e_neg).astype(BF16),
            bt=(bdv * e_neg).astype(BF16),
            kh=(kd * e_end).astype(BF16),
            bh=(bdv * e_end).astype(BF16),
            v=v_ref[...],
            g_tot=jnp.exp(tot),
        )
        keep = (cp > rp - bottom) if reverse else (cp < rp + bottom)
        for hp in range(RW_HEADS // 2):
            sl = slice(hp * pw, (hp + 1) * pw)
            ch = {k: val[:, sl] for k, val in ops.items()}
            ch.update(di=di, hp=hp, sl=sl, keep=keep, y_ref=y_ref)
            chains.append(ch)

    for ch in chains:
        g = _dot_nt(jnp.concatenate([ch["at"], ch["rt"]], axis=0),
                    jnp.concatenate(halves(ch["bt"]) + halves(ch["kt"]), axis=0))
        g = jnp.where(ch["keep"], g, 0.0).astype(BF16)
        ch["l"] = g[:CHUNK, :pw]
        ch["ak"] = g[:CHUNK, pw:]
        ch["rbk"] = g[CHUNK:, :]
    for ch in chains:
        ch["p"] = _dot(ch["l"], bd(ch["l"]))
        ch["av"] = _dot(ch["ak"], bd(ch["v"]))
        ch["kv"] = pick(_dot_tn(ch["v"], ch["kh"]))
        ch["x"] = eye2 - ch["l"].astype(F32)
    for _ in range(4):
        for ch in chains:
            z = _dot(jnp.concatenate([ch["x"], ch["p"]], axis=0), bd(ch["p"]))
            ch["x"] = ch["x"] + z[:CHUNK]
            ch["p"] = z[CHUNK:]
    for ch in chains:
        ch["x"] = ch["x"] + _dot(ch["x"], bd(ch["p"]))
    for ch in chains:
        wu = _dot(ch["x"], jnp.concatenate([bd(ch["at"]), bd(ch["av"])], axis=1))
        ch["wm"] = wu[:, :pw].astype(BF16)
        ch["u0"] = -wu[:, pw:]
    for ch in chains:
        t = ch["u0"].T
        ch["u0t"] = jnp.concatenate([t[:n], t[n:]], axis=1)
    for ch in chains:
        s = s_scr[ch["di"], ch["hp"]]
        ch["s"] = s
        pr = _dot_nt(jnp.concatenate([ch["wm"], ch["rt"]], axis=0), bd(s))
        ch["u"] = ch["u0"] - pr[:CHUNK]
        ch["rs"] = pr[CHUNK:]
        ch["ut"] = ch["u0t"] - _dot_nt(s, bd(ch["wm"]))
    for ch in chains:
        y = ch["rs"] + _dot(ch["rbk"], jnp.concatenate([bd(ch["u"]), bd(ch["v"])], axis=0))
        ch["y_ref"][:, ch["sl"]] = y.astype(ch["y_ref"].dtype)
        s_scr[ch["di"], ch["hp"]] = (ch["s"] * ch["g_tot"] + ch["kv"]
                                     + _dot(ch["ut"], bd(ch["bh"])))

    @pl.when(c == pl.num_programs(1) - 1)
    def _():
        sfin_ref[...] = s_scr[...]


RW_STATE_SHAPE = (2, RW_HEADS // 2, RW_HEAD_DIM, 2 * RW_HEAD_DIM)


def _rwscan(r, v, kk, lw, kd, bd, s0):
    bsz, t, w = r.shape
    nc = t // CHUNK
    tok_f = pl.BlockSpec((None, CHUNK, w), lambda b, c: (b, c, 0))
    tok_b = pl.BlockSpec((None, CHUNK, w), lambda b, c: (b, nc - 1 - c, 0))
    dir_f = pl.BlockSpec((None, None, CHUNK, w), lambda b, c: (0, b, c, 0))
    dir_b = pl.BlockSpec((None, None, CHUNK, w), lambda b, c: (1, b, nc - 1 - c, 0))
    st = pl.BlockSpec((None,) + RW_STATE_SHAPE, lambda b, c: (b, 0, 0, 0, 0))
    y_shape = jax.ShapeDtypeStruct((bsz, t, w), BF16)
    yf, yb, sfin = pl.pallas_call(
        _rwscan_kernel,
        grid=(bsz, nc),
        in_specs=[tok_f, tok_f, tok_f, dir_f, dir_f, dir_f,
                  tok_b, tok_b, tok_b, dir_b, dir_b, dir_b, st],
        out_specs=[tok_f, tok_b, st],
        out_shape=[y_shape, y_shape, jax.ShapeDtypeStruct(s0.shape, F32)],
        scratch_shapes=[pltpu.VMEM(RW_STATE_SHAPE, F32)],
        compiler_params=_params(2),
        name="rwscan",
    )(r, v, kk, lw, kd, bd, r, v, kk, lw, kd, bd, s0)
    return yf, yb, sfin


def _rwpost_kernel(yf_ref, yb_ref, g_ref, gb_ref, gnw, gnb, hind, o_ref):
    y = yf_ref[...].astype(F32) + yb_ref[...].astype(F32)
    inv_n = 1.0 / RW_HEAD_DIM
    mean = _dot_ind_rhs(y, hind[...]) * inv_n
    yc = y - mean
    var = _dot_ind_rhs(yc * yc, hind[...]) * inv_n
    yn = yc * lax.rsqrt(var + RW_GN_EPS) * gnw[...] + gnb[...]
    o_ref[...] = (yn * g_ref[...].astype(F32) + gb_ref[...].astype(F32)).astype(o_ref.dtype)


def _rwpost(yf, yb, g, gb, lp, hind, *, tm):
    bsz, t, w = yf.shape
    tok = pl.BlockSpec((None, tm, w), lambda b, j: (b, j, 0))
    return pl.pallas_call(
        _rwpost_kernel,
        grid=(bsz, t // tm),
        in_specs=[tok, tok, tok, tok, _const_spec((1, w)), _const_spec((1, w)),
                  _const_spec(hind.shape)],
        out_specs=tok,
        out_shape=jax.ShapeDtypeStruct((bsz, t, w), BF16),
        compiler_params=_params(2),
        name="rwpost",
    )(yf, yb, g, gb, lp["rw_gn_w"], lp["rw_gn_b"], hind)


def _glaprep_kernel(q_cur, q_prev, q_next, ald_ref, conv, up, bias, o_q, o_k, o_v, o_la):
    j = pl.program_id(1)
    x = q_cur[...].astype(F32)
    prev, nxt = _shifted(x, q_prev[...], q_next[...], j == 0, j == pl.num_programs(1) - 1)
    y = _silu(conv[0:1, :] * prev + conv[1:2, :] * x + conv[2:3, :] * nxt)
    o_q[...] = (y[:, 0:GLA_KW] * (GLA_DK ** -0.5)).astype(o_q.dtype)
    o_k[...] = y[:, GLA_KW:2 * GLA_KW].astype(o_k.dtype)
    o_v[...] = y[:, 2 * GLA_KW:].astype(o_v.dtype)
    ald = ald_ref[...]
    for di in range(2):
        o_la[di] = _log_sigmoid(_dot(ald, up[di]) + bias[di:di + 1, :]) / GLA_TAU


def _glaprep(qkv, ald, lp, *, tm):
    bsz, t, fq = qkv.shape
    prev_spec, next_spec = _halo_specs(fq, tm, t)
    tok = lambda f: pl.BlockSpec((None, tm, f), lambda b, j: (b, j, 0))
    small = [lp["gla_conv"], lp["gla_alpha_up_pad"], lp["gla_alpha_bias"]]
    return pl.pallas_call(
        _glaprep_kernel,
        grid=(bsz, t // tm),
        in_specs=[tok(fq), prev_spec, next_spec, tok(LANES)] + [_const_spec(s.shape) for s in small],
        out_specs=[tok(GLA_KW), tok(GLA_KW), tok(GLA_VW),
                   pl.BlockSpec((2, None, tm, GLA_KW), lambda b, j: (0, b, j, 0))],
        out_shape=[jax.ShapeDtypeStruct((bsz, t, GLA_KW), BF16),
                   jax.ShapeDtypeStruct((bsz, t, GLA_KW), BF16),
                   jax.ShapeDtypeStruct((bsz, t, GLA_VW), BF16),
                   jax.ShapeDtypeStruct((2, bsz, t, GLA_KW), F32)],
        compiler_params=_params(2),
        name="glaprep",
    )(qkv, qkv, qkv, ald, *small)


def _glascan_kernel(qf, kf, vf, laf, qb, kb, vb, lab, s0_ref, of_ref, ob_ref, sfin_ref, s_scr,
                    *, nb):
    j = pl.program_id(1)

    @pl.when(j == 0)
    def _():
        s_scr[...] = s0_ref[...]

    row = lax.broadcasted_iota(jnp.int32, (CHUNK, CHUNK), 0)
    col = lax.broadcasted_iota(jnp.int32, (CHUNK, CHUNK), 1)
    chains = []
    for di, (q_ref, k_ref, v_ref, la_ref, o_ref) in enumerate(
            ((qf, kf, vf, laf, of_ref), (qb, kb, vb, lab, ob_ref))):
        reverse = di == 1
        keep = _before(row, col, reverse, True)
        for ci in (reversed(range(nb)) if reverse else range(nb)):
            rows = slice(ci * CHUNK, (ci + 1) * CHUNK)
            cum, tot = _cum_parts(la_ref[rows, :], reverse)
            k = k_ref[rows, :].astype(F32)
            q_dec = (q_ref[rows, :].astype(F32) * jnp.exp(cum)).astype(BF16)
            k_inv = (k * jnp.exp(-cum)).astype(BF16)
            k_end = (k * jnp.exp(tot - cum)).astype(BF16)
            dec = jnp.exp(tot)
            v = v_ref[rows, :]
            for h in range(GLA_HEADS):
                sk = slice(h * GLA_DK, (h + 1) * GLA_DK)
                sv = slice(h * GLA_DV, (h + 1) * GLA_DV)
                chains.append(dict(di=di, h=h, rows=rows, sv=sv, keep=keep, o_ref=o_ref,
                                   q=q_dec[:, sk], ki=k_inv[:, sk], ke=k_end[:, sk],
                                   dec=dec[:, sk], v=v[:, sv]))
    for ch in chains:
        ch["sc"] = jnp.where(ch["keep"], _dot_nt(ch["q"], ch["ki"]), 0.0)
        ch["kv"] = _dot_tn(ch["v"], ch["ke"])
    state = {}
    for ch in chains:
        key = (ch["di"], ch["h"])
        s = state[key] if key in state else s_scr[ch["di"], ch["h"]]
        ch["s"] = s
        state[key] = s * ch["dec"] + ch["kv"]
    for (di, h), s in state.items():
        s_scr[di, h] = s
    for ch in chains:
        o = _dot(ch["sc"], ch["v"]) + _dot_nt(ch["q"], ch["s"])
        ch["o_ref"][ch["rows"], ch["sv"]] = o.astype(ch["o_ref"].dtype)

    @pl.when(j == pl.num_programs(1) - 1)
    def _():
        sfin_ref[...] = s_scr[...]


def _glascan(q, k, v, la, s0):
    bsz, t, _ = q.shape
    nb = GLA_BLOCK_CHUNKS
    tb = nb * CHUNK
    nblk = t // tb
    tok_f = lambda f: pl.BlockSpec((None, tb, f), lambda b, j: (b, j, 0))
    tok_b = lambda f: pl.BlockSpec((None, tb, f), lambda b, j: (b, nblk - 1 - j, 0))
    dir_f = pl.BlockSpec((None, None, tb, GLA_KW), lambda b, j: (0, b, j, 0))
    dir_b = pl.BlockSpec((None, None, tb, GLA_KW), lambda b, j: (1, b, nblk - 1 - j, 0))
    st = pl.BlockSpec((None, 2, GLA_HEADS, GLA_DV, GLA_DK), lambda b, j: (b, 0, 0, 0, 0))
    o_shape = jax.ShapeDtypeStruct((bsz, t, GLA_VW), BF16)
    return pl.pallas_call(
        functools.partial(_glascan_kernel, nb=nb),
        grid=(bsz, nblk),
        in_specs=[tok_f(GLA_KW), tok_f(GLA_KW), tok_f(GLA_VW), dir_f,
                  tok_b(GLA_KW), tok_b(GLA_KW), tok_b(GLA_VW), dir_b, st],
        out_specs=[tok_f(GLA_VW), tok_b(GLA_VW), st],
        out_shape=[o_shape, o_shape, jax.ShapeDtypeStruct(s0.shape, F32)],
        scratch_shapes=[pltpu.VMEM((2, GLA_HEADS, GLA_DV, GLA_DK), F32)],
        compiler_params=_params(2),
        name="glascan",
    )(q, k, v, la, q, k, v, la, s0)


def _glapost_kernel(of_ref, ob_ref, gate_ref, nw, y_ref, *, y_tr, a, k):
    o = of_ref[...].astype(F32) + ob_ref[...].astype(F32)
    gate = gate_ref[...].astype(F32)
    ys = []
    for h in range(GLA_HEADS):
        sv = slice(h * GLA_DV, (h + 1) * GLA_DV)
        ys.append(_rms(o[:, sv], nw[...]) * _silu(gate[:, sv]))
    _store_tok(y_ref, jnp.concatenate(ys, axis=1), y_tr, a, k)


def _glapost(of, ob, gate, lp, *, y_tr, a, tm):
    bsz, t, w = of.shape
    k = tm // a
    tok = pl.BlockSpec((None, tm, w), lambda b, j: (b, j, 0))
    out = pl.pallas_call(
        functools.partial(_glapost_kernel, y_tr=y_tr, a=a, k=k),
        grid=(bsz, t // tm),
        in_specs=[tok, tok, tok, _const_spec((1, GLA_DV))],
        out_specs=_tok_spec(w, tm, y_tr, a),
        out_shape=_tok_shape(bsz, t, w, y_tr, a, BF16),
        compiler_params=_params(2),
        name="glapost",
    )(of, ob, gate, lp["gla_norm_w"])
    return _tok_unview(out, y_tr, t, w)


def _merge_kernel(x_ref, yrw_ref, ygla_ref, mg_ref, mod_ref, gains, w_rwo, w_glao, w_mo, w1, w2,
                  o_ref, *, x_tr, a, k):
    d = x_ref.shape[-1]
    x = _load_tok(x_ref, x_tr, a, k)
    mg = mg_ref[...].astype(F32)
    br = (_sigmoid(mg[:, :d]) * _dot(yrw_ref[...], w_rwo[...])
          + _sigmoid(mg[:, d:]) * _dot(ygla_ref[...], w_glao[...]))
    m = _dot(br, w_mo[...])
    x1 = x + mod_ref[2:3, :] * _rms(m, gains[1:2, :])
    h2 = _rms(x1, gains[2:3, :]) * (1.0 + mod_ref[4:5, :]) + mod_ref[3:4, :]
    hid = jnp.maximum(_dot(h2, w1[...]), 0.0)
    f = _dot(hid * hid, w2[...])
    x2 = x1 + mod_ref[5:6, :] * _rms(f, gains[3:4, :])
    _store_tok(o_ref, x2, x_tr, a, k)


def _merge(x, yrw, ygla, mg, mods, mod_row, gains, ws, *, x_tr, a, tm):
    bsz, t, d = x.shape
    k = tm // a
    kern = functools.partial(_merge_kernel, x_tr=x_tr, a=a, k=k)
    out = pl.pallas_call(
        kern,
        grid=(bsz, t // tm),
        in_specs=[_tok_spec(d, tm, x_tr, a), _tok_spec(RW_WIDTH, tm, False, a),
                  _tok_spec(GLA_VW, tm, False, a), _tok_spec(2 * d, tm, False, a),
                  pl.BlockSpec((None, 6, d), lambda b, j: (mod_row(b), 0, 0)),
                  _const_spec(gains.shape)] + [_const_spec(w.shape) for w in ws],
        out_specs=_tok_spec(d, tm, x_tr, a),
        out_shape=_tok_shape(bsz, t, d, x_tr, a, F32),
        compiler_params=_params(2),
        name="merge",
    )(_tok_view(x, x_tr, a), yrw, ygla, mg, mods, gains, *ws)
    return _tok_unview(out, x_tr, t, d)


def _layer_params(l, p):
    w_in = p["w_in"][l]
    g0 = RW_COLS
    g1 = g0 + GLA_QKV_W
    g2 = g1 + 2 * GLA_GATE_RANK
    g3 = g2 + GLA_VW
    bf = lambda w: w.astype(BF16)
    row = lambda w: w.reshape(1, -1)
    lp = {
        "w_in_parts": [
            bf(w_in[:, :g0]),
            bf(w_in[:, g0:g1]),
            bf(jnp.pad(w_in[:, g1:g2], ((0, 0), (0, LANES - 2 * GLA_GATE_RANK)))),
            bf(w_in[:, g2:g3]),
            bf(w_in[:, g3:]),
        ],
        "rw_mu": row(p["rw_mu"][l]),
        "rw_w0": p["rw_w0"][l],
        "rw_a0": p["rw_a0"][l],
        "rw_g_up": bf(p["rw_g_up"][l]),
        "rw_k_k": row(p["rw_k_k"][l]),
        "rw_k_a": row(p["rw_k_a"][l]),
        "rw_r_k": row(p["rw_r_k"][l]),
        "rw_gn_w": row(p["rw_gn_w"][l]),
        "rw_gn_b": row(p["rw_gn_b"][l]),
        "gla_conv": p["gla_conv"][l],
        "gla_alpha_bias": p["gla_alpha_bias"][l],
        "gla_norm_w": row(p["gla_norm_w"][l]),
        "merge_ws": [bf(p["rw_out"][l]), bf(p["gla_out"][l]), bf(p["merge_out"][l]),
                     bf(p["mlp_w1"][l]), bf(p["mlp_w2"][l])],
        "gains": jnp.stack([p["norm_mix_pre"][l], p["norm_mix_post"][l],
                            p["norm_ffn_pre"][l], p["norm_ffn_post"][l]]),
    }
    pad_dir = lambda w, r: jnp.stack([jnp.pad(w[di], ((di * r, LANES - (di + 1) * r), (0, 0)))
                                      for di in range(2)])
    lp["rw_w_up_pad"] = bf(pad_dir(p["rw_w_up"][l], p["rw_w_up"].shape[2]))
    lp["rw_a_up_pad"] = bf(pad_dir(p["rw_a_up"][l], p["rw_a_up"].shape[2]))
    lp["gla_alpha_up_pad"] = bf(pad_dir(p["gla_alpha_up"][l], GLA_GATE_RANK))
    if l > 0:
        lp["rw_vres_down"] = bf(p["rw_vres_down"][l - 1])
        lp["rw_vres_up"] = bf(p["rw_vres_up"][l - 1])
        lp["rw_vres_bias"] = row(p["rw_vres_bias"][l - 1])
    return lp


def _head_indicator():
    h = jnp.arange(RW_WIDTH) // RW_HEAD_DIM
    return (h[:, None] == h[None, :]).astype(BF16)


def _mixer(x, mods, mod_row, vfirst, s_rw, s_gla, lp, hind, *, p_col, need_out):
    bsz, t, d = x.shape
    tm = min(TOKEN_TILE, t)
    tm_merge = min(MERGE_TILE, t)
    if p_col is None:
        x_tr, g_tr, a_p, a_q = False, False, SUBLANES, SUBLANES
    else:
        x_tr, g_tr = p_col, True
        rows = t // GRID_W
        a_p, a_q = (rows, GRID_W) if p_col else (GRID_W, rows)
    rw, gq, gald, gg, mg = _inproj(x, mods, mod_row, lp["gains"][0:1], lp["w_in_parts"],
                                   x_tr=x_tr, g_tr=g_tr, a=a_p, tm=tm)
    r, v, kk, g, gb, lw, kd, bd = _rwprep(rw, vfirst, lp, hind, vf_tr=x_tr, a=a_p, tm=tm)
    yf, yb, s_rw_out = _rwscan(r, v, kk, lw, kd, bd, s_rw)
    q, k, gv, la = _glaprep(gq, gald, lp, tm=tm)
    of, ob, s_gla_out = _glascan(q, k, gv, la, s_gla)
    if not need_out:
        return None, v, s_rw_out, s_gla_out
    yrw = _rwpost(yf, yb, g, gb, lp, hind, tm=tm)
    ygla = _glapost(of, ob, gg, lp, y_tr=g_tr, a=a_q, tm=tm)
    x_new = _merge(x, yrw, ygla, mg, mods, mod_row, lp["gains"], lp["merge_ws"],
                   x_tr=x_tr, a=a_p, tm=tm_merge)
    return x_new, v, s_rw_out, s_gla_out


def kernel(x, c, ctx, c_ctx, w_in, rw_mu, rw_w0, rw_w_up, rw_a0, rw_a_up, rw_g_up, rw_k_k, rw_k_a,
           rw_r_k, rw_gn_w, rw_gn_b, rw_vres_down, rw_vres_up, rw_vres_bias, rw_out, gla_conv,
           gla_alpha_up, gla_alpha_bias, gla_norm_w, gla_out, merge_out, mlp_w1, mlp_w2, ada_w,
           ada_b, norm_mix_pre, norm_mix_post, norm_ffn_pre, norm_ffn_post):
    p = dict(w_in=w_in, rw_mu=rw_mu, rw_w0=rw_w0, rw_w_up=rw_w_up, rw_a0=rw_a0, rw_a_up=rw_a_up,
             rw_g_up=rw_g_up, rw_k_k=rw_k_k, rw_k_a=rw_k_a, rw_r_k=rw_r_k.reshape(rw_r_k.shape[0], -1),
             rw_gn_w=rw_gn_w, rw_gn_b=rw_gn_b, rw_vres_down=rw_vres_down, rw_vres_up=rw_vres_up,
             rw_vres_bias=rw_vres_bias, rw_out=rw_out, gla_conv=gla_conv, gla_alpha_up=gla_alpha_up,
             gla_alpha_bias=gla_alpha_bias, gla_norm_w=gla_norm_w, gla_out=gla_out,
             merge_out=merge_out, mlp_w1=mlp_w1, mlp_w2=mlp_w2, norm_mix_pre=norm_mix_pre,
             norm_mix_post=norm_mix_post, norm_ffn_pre=norm_ffn_pre, norm_ffn_post=norm_ffn_post)
    bsz, t, d = x.shape
    t_ctx = ctx.shape[1]
    depth = w_in.shape[0]
    block = GLA_BLOCK_CHUNKS * CHUNK
    assert bsz < SUBLANES and d % LANES == 0
    assert t % (GRID_W * SUBLANES) == 0 and t % block == 0 and t_ctx % block == 0
    assert t % min(TOKEN_TILE, t) == 0 and t_ctx % min(TOKEN_TILE, t_ctx) == 0

    cc = jnp.concatenate([c, c_ctx[None, :], jnp.zeros((SUBLANES - 1 - bsz, d), F32)], axis=0)
    mods = _ada_mods(cc, ada_w, ada_b).reshape(depth, SUBLANES, 6, d)
    hind = _head_indicator()
    lat_row = lambda b: b
    ctx_row = lambda b: bsz

    x_lat, x_ctx = x, ctx
    vf_lat = vf_ctx = None
    for l in range(depth):
        last = l == depth - 1
        lp = _layer_params(l, p)
        z_rw = jnp.zeros((bsz,) + RW_STATE_SHAPE, F32)
        z_gla = jnp.zeros((bsz, 2, GLA_HEADS, GLA_DV, GLA_DK), F32)
        x_ctx_new, v_ctx, s_rw, s_gla = _mixer(
            x_ctx, mods[l], ctx_row, vf_ctx, z_rw, z_gla, lp, hind, p_col=None, need_out=not last)
        x_lat, v_lat, _, _ = _mixer(
            x_lat, mods[l], lat_row, vf_lat, s_rw, s_gla, lp, hind,
            p_col=(l % 2 == 1), need_out=True)
        if l == 0:
            vf_lat, vf_ctx = v_lat, v_ctx
        if not last:
            x_ctx = x_ctx_new
    return x_lat
```

```python
import functools

import jax
import jax.numpy as jnp
from jax import lax
from jax.experimental import pallas as pl
from jax.experimental.pallas import tpu as pltpu

F32 = jnp.float32
BF16 = jnp.bfloat16

GRID_W = 64
RMS_EPS = 1e-6
RW_HEADS = 8
RW_HEAD_DIM = 64
RW_WIDTH = RW_HEADS * RW_HEAD_DIM
RW_GN_EPS = 64e-5
RW_COLS = 1920
GLA_HEADS = 4
GLA_DK = 64
GLA_DV = 128
GLA_KW = GLA_HEADS * GLA_DK
GLA_VW = GLA_HEADS * GLA_DV
GLA_QKV_W = 2 * GLA_KW + GLA_VW
GLA_GATE_RANK = 16
GLA_TAU = 16.0
CHUNK = 64
GLA_BLOCK_CHUNKS = 4
LANES = 128
SUBLANES = 8
HALO = 16
VMEM_LIMIT = 56 * 1024 * 1024
TOKEN_TILE = 1024
PROJ_TILE = 512
MERGE_TILE = 512
GLA_IN_DTYPE = F32


def _dot(a, b):
    return jnp.dot(a.astype(BF16), b.astype(BF16), preferred_element_type=F32)


def _dot_nt(a, b):
    return lax.dot_general(a.astype(BF16), b.astype(BF16), (((1,), (1,)), ((), ())),
                           preferred_element_type=F32)


def _dot_tn(a, b):
    return lax.dot_general(a.astype(BF16), b.astype(BF16), (((0,), (0,)), ((), ())),
                           preferred_element_type=F32)


def _split2(x):
    hi = x.astype(BF16)
    lo = (x - hi.astype(F32)).astype(BF16)
    return hi, lo


def _dot_ind_rhs(x, ind):
    hi, lo = _split2(x)
    return (jnp.dot(hi, ind, preferred_element_type=F32)
            + jnp.dot(lo, ind, preferred_element_type=F32))


def _dot_ind_lhs(ind, x):
    hi, lo = _split2(x)
    return (jnp.dot(ind, hi, preferred_element_type=F32)
            + jnp.dot(ind, lo, preferred_element_type=F32))


def _sigmoid(x):
    return jax.nn.sigmoid(x)


def _silu(x):
    return x * jax.nn.sigmoid(x)


def _softplus(z):
    return jnp.maximum(z, 0.0) + jnp.log1p(jnp.exp(-jnp.abs(z)))


def _log_sigmoid(z):
    return -_softplus(-z)


def _rms(x, gain):
    return x * lax.rsqrt(jnp.mean(x * x, axis=-1, keepdims=True) + RMS_EPS) * gain


def _load_tok(ref, transposed, a, k):
    if not transposed:
        return ref[...]
    return jnp.concatenate([ref[:, i, :] for i in range(k)], axis=0)


def _store_tok(ref, val, transposed, a, k):
    val = val.astype(ref.dtype)
    if not transposed:
        ref[...] = val
    else:
        for i in range(k):
            ref[:, i, :] = val[i * a:(i + 1) * a, :]


def _tok_view(arr, transposed, a):
    if not transposed:
        return arr
    b, t, f = arr.shape
    return arr.reshape(b, a, t // a, f)


def _tok_unview(arr, transposed, t, f):
    if not transposed:
        return arr
    return arr.reshape(arr.shape[0], t, f)


def _tok_spec(f, tm, transposed, a):
    if not transposed:
        return pl.BlockSpec((None, tm, f), lambda b, j: (b, j, 0))
    return pl.BlockSpec((None, a, tm // a, f), lambda b, j: (b, 0, j, 0))


def _tok_shape(bsz, t, f, transposed, a, dtype):
    if not transposed:
        return jax.ShapeDtypeStruct((bsz, t, f), dtype)
    return jax.ShapeDtypeStruct((bsz, a, t // a, f), dtype)


def _const_spec(shape):
    nd = len(shape)
    return pl.BlockSpec(shape, lambda *_: (0,) * nd, pipeline_mode=pl.Buffered(1))


def _params(ndim):
    return pltpu.CompilerParams(dimension_semantics=("arbitrary",) * ndim,
                                vmem_limit_bytes=VMEM_LIMIT)


def _halo_specs(f, tm, t):
    sub = tm // HALO
    last = t // HALO - 1
    prev = pl.BlockSpec((None, HALO, f), lambda b, j: (b, jnp.maximum(j * sub - 1, 0), 0))
    nxt = pl.BlockSpec((None, HALO, f), lambda b, j: (b, jnp.minimum((j + 1) * sub, last), 0))
    return prev, nxt


def _shifted(cur, prev_blk, next_blk, first, last):
    tm = cur.shape[0]
    row = lax.broadcasted_iota(jnp.int32, cur.shape, 0)
    p_row = jnp.where(first, 0.0, prev_blk[HALO - 1:HALO, :].astype(F32))
    n_row = jnp.where(last, 0.0, next_blk[0:1, :].astype(F32))
    prev = jnp.where(row == 0, p_row, pltpu.roll(cur, 1, 0))
    nxt = jnp.where(row == tm - 1, n_row, pltpu.roll(cur, tm - 1, 0))
    return prev, nxt


def _ada_kernel(c_ref, w_ref, b_ref, o_ref):
    cc = c_ref[...]
    o_ref[...] = _dot(_silu(cc), w_ref[...]) + b_ref[...]


def _ada_mods(cc, ada_w, ada_b):
    nl, d, n6 = ada_w.shape
    tn = d
    return pl.pallas_call(
        _ada_kernel,
        grid=(nl, n6 // tn),
        in_specs=[pl.BlockSpec((SUBLANES, d), lambda l, n: (0, 0)),
                  pl.BlockSpec((None, d, tn), lambda l, n: (l, 0, n)),
                  pl.BlockSpec((None, 1, tn), lambda l, n: (l, 0, n))],
        out_specs=pl.BlockSpec((None, SUBLANES, tn), lambda l, n: (l, 0, n)),
        out_shape=jax.ShapeDtypeStruct((nl, SUBLANES, n6), F32),
        compiler_params=_params(2),
        name="ada",
    )(cc, ada_w, ada_b.reshape(nl, 1, n6))


def _inproj_kernel(*refs, has_vres, x_tr, g_tr, a, k):
    it = iter(refs)
    x_ref, xp_ref, xn_ref = next(it), next(it), next(it)
    vf_ref = next(it) if has_vres else None
    mod_ref, gain_ref, w_rw, w_q, w_ald, w_gg, w_mg = (next(it) for _ in range(7))
    mu, w0, w_up, a0, a_up, g_up, k_k, k_a, r_k = (next(it) for _ in range(9))
    if has_vres:
        vdown, vup, vbias = next(it), next(it), next(it)
    hind = next(it)
    o_q, o_ald, o_gg, o_mg = (next(it) for _ in range(4))
    o_r, o_v, o_kk, o_g, o_gb, o_lw, o_kd, o_bd = (next(it) for _ in range(8))

    j = pl.program_id(1)
    keep_prev = jnp.where(j == 0, 0.0, 1.0)
    keep_next = jnp.where(j == pl.num_programs(1) - 1, 0.0, 1.0)
    x = _load_tok(x_ref, x_tr, a, k)
    tm = x.shape[0]
    xp = xp_ref[SUBLANES - 1] if x_tr else xp_ref[...]
    xn = xn_ref[0] if x_tr else xn_ref[...]
    modulate = lambda z: _rms(z, gain_ref[...]) * (1.0 + mod_ref[1:2, :]) + mod_ref[0:1, :]
    h = modulate(x)
    hb = h.astype(BF16)
    h_ext = jnp.concatenate([modulate(xp) * keep_prev, h, modulate(xn) * keep_next], axis=0)
    f_ext = jnp.dot(h_ext.astype(BF16), w_rw[...], preferred_element_type=F32)
    proj = lambda w: jnp.dot(hb, w[...], preferred_element_type=F32)
    _store_tok(o_mg, proj(w_mg), False, a, k)
    _store_tok(o_q, proj(w_q), g_tr, a, k)
    _store_tok(o_ald, proj(w_ald), g_tr, a, k)
    _store_tok(o_gg, proj(w_gg), g_tr, a, k)

    f = f_ext[SUBLANES:SUBLANES + tm]
    nbr = f_ext[SUBLANES - 1:SUBLANES - 1 + tm] + f_ext[SUBLANES + 1:SUBLANES + 1 + tm]
    fs = f + mu[...] * (0.5 * nbr - f)
    w = RW_WIDTH
    r = fs[:, 0:w]
    kx = fs[:, w:2 * w]
    v = fs[:, 2 * w:3 * w]
    wd = fs[:, 3 * w:3 * w + LANES]
    ad = fs[:, 3 * w + LANES:3 * w + 2 * LANES]
    gd = fs[:, 3 * w + 2 * LANES:3 * w + 3 * LANES]
    if has_vres:
        vf = _load_tok(vf_ref, x_tr, a, k).astype(F32)
        mix = _sigmoid(vbias[...] + _dot(_dot(v, vdown[...]), vup[...]))
        v = v + (vf - v) * mix
    kk = kx * k_k[...]
    kk = kk * lax.rsqrt(_dot_ind_rhs(kk * kk, hind[...]) + 1e-12)
    twd = jnp.tanh(wd)
    ksum = None
    for di in range(2):
        wlog = -_softplus(-(w0[di:di + 1, :] + _dot(twd, w_up[di]))) - 0.5
        o_lw[di] = -jnp.exp(wlog)
        lr = _sigmoid(a0[di:di + 1, :] + _dot(ad, a_up[di]))
        kd = kx * (1.0 + (lr - 1.0) * k_a[...])
        o_kd[di] = kd.astype(o_kd.dtype)
        o_bd[di] = (lr * kk).astype(o_bd.dtype)
        ksum = kd if ksum is None else ksum + kd
    g = _dot(_sigmoid(gd), g_up[...])
    bonus = _dot_ind_rhs(r * ksum * r_k[...], hind[...]) * v
    o_r[...] = r.astype(o_r.dtype)
    o_v[...] = v.astype(o_v.dtype)
    o_kk[...] = kk.astype(o_kk.dtype)
    o_g[...] = g.astype(o_g.dtype)
    o_gb[...] = (bonus * g).astype(o_gb.dtype)


def _neighbour_specs(d, tm, t, x_tr, a):
    sub = tm // SUBLANES
    if not x_tr:
        last = t // SUBLANES - 1
        prev = pl.BlockSpec((None, SUBLANES, d), lambda b, j: (b, jnp.maximum(j * sub - 1, 0), 0))
        nxt = pl.BlockSpec((None, SUBLANES, d), lambda b, j: (b, jnp.minimum((j + 1) * sub, last), 0))
        return prev, nxt
    kb = tm // a // SUBLANES
    last = t // a // SUBLANES - 1
    shape = (None, SUBLANES, SUBLANES, d)
    prev = pl.BlockSpec(shape, lambda b, j: (b, a // SUBLANES - 1, jnp.maximum(j * kb - 1, 0), 0))
    nxt = pl.BlockSpec(shape, lambda b, j: (b, 0, jnp.minimum((j + 1) * kb, last), 0))
    return prev, nxt


def _inproj(x, vfirst, mods, mod_row, lp, hind, *, x_tr, g_tr, a, tm):
    bsz, t, d = x.shape
    w = RW_WIDTH
    k = tm // a
    has_vres = vfirst is not None
    kern = functools.partial(_inproj_kernel, has_vres=has_vres, x_tr=x_tr, g_tr=g_tr, a=a, k=k)
    xv = _tok_view(x, x_tr, a)
    prev_spec, next_spec = _neighbour_specs(d, tm, t, x_tr, a)
    args = [xv, xv, xv]
    in_specs = [_tok_spec(d, tm, x_tr, a), prev_spec, next_spec]
    if has_vres:
        args.append(_tok_view(vfirst, x_tr, a))
        in_specs.append(_tok_spec(w, tm, x_tr, a))
    args.append(mods)
    in_specs.append(pl.BlockSpec((None, 6, d), lambda b, j: (mod_row(b), 0, 0)))
    small = [lp["gains"][0:1]] + lp["w_in_parts"] + [
        lp["rw_mu"], lp["rw_w0"], lp["rw_w_up_pad"], lp["rw_a0"], lp["rw_a_up_pad"],
        lp["rw_g_up"], lp["rw_k_k"], lp["rw_k_a"], lp["rw_r_k"]]
    if has_vres:
        small += [lp["rw_vres_down"], lp["rw_vres_up"], lp["rw_vres_bias"]]
    small.append(hind)
    args += small
    in_specs += [_const_spec(s.shape) for s in small]
    gla_widths = (GLA_QKV_W, LANES, GLA_VW)
    tok = pl.BlockSpec((None, tm, w), lambda b, j: (b, j, 0))
    tok2 = pl.BlockSpec((2, None, tm, w), lambda b, j: (0, b, j, 0))
    s1 = jax.ShapeDtypeStruct((bsz, t, w), BF16)
    s2 = lambda dt: jax.ShapeDtypeStruct((2, bsz, t, w), dt)
    outs = pl.pallas_call(
        kern,
        grid=(bsz, t // tm),
        in_specs=in_specs,
        out_specs=[_tok_spec(f, tm, g_tr, a) for f in gla_widths]
        + [_tok_spec(2 * d, tm, False, a)] + [tok] * 5 + [tok2] * 3,
        out_shape=[_tok_shape(bsz, t, f, g_tr, a, GLA_IN_DTYPE) for f in gla_widths]
        + [_tok_shape(bsz, t, 2 * d, False, a, BF16)] + [s1] * 5 + [s2(F32), s2(BF16), s2(BF16)],
        compiler_params=_params(2),
        name="inproj",
    )(*args)
    gla = [_tok_unview(o, g_tr, t, f) for o, f in zip(outs[:3], gla_widths)]
    return gla, outs[3], outs[4:]


def _before(row, col, reverse, inclusive):
    if reverse:
        return (col >= row) if inclusive else (col > row)
    return (col <= row) if inclusive else (col < row)


def _cum_parts(lw, reverse):
    row = lax.broadcasted_iota(jnp.int32, (CHUNK, CHUNK), 0)
    col = lax.broadcasted_iota(jnp.int32, (CHUNK, CHUNK), 1)
    tri = jnp.where(_before(row, col, reverse, True), 1.0, 0.0).astype(BF16)
    cum = _dot_ind_lhs(tri, lw)
    tot = cum[0:1, :] if reverse else cum[CHUNK - 1:CHUNK, :]
    return cum, tot


def _rwscan_kernel(rf, vf, kkf, lwf, kdf, bdf, rb, vb, kkb, lwb, kdb, bdb, s0_ref,
                   yf_ref, yb_ref, sfin_ref, s_scr):
    c = pl.program_id(1)

    @pl.when(c == 0)
    def _():
        s_scr[...] = s0_ref[...]

    n = RW_HEAD_DIM
    pw = 2 * n
    lane = lax.broadcasted_iota(jnp.int32, (CHUNK, pw), 1)
    lo = lane < n
    row = lax.broadcasted_iota(jnp.int32, (CHUNK, pw), 0)
    eye2 = jnp.where(row == lane % n, 1.0, 0.0)
    grow = lax.broadcasted_iota(jnp.int32, (2 * CHUNK, 2 * pw), 0)
    gcol = lax.broadcasted_iota(jnp.int32, (2 * CHUNK, 2 * pw), 1)
    rp = grow % CHUNK
    cp = gcol % CHUNK
    bottom = grow // CHUNK

    def halves(z):
        zb = z.astype(BF16)
        zero = jnp.zeros_like(zb)
        return jnp.where(lo, zb, zero), jnp.where(lo, zero, zb)

    def bd(z):
        return jnp.concatenate(halves(z), axis=0)

    def pick(z):
        return jnp.where(lo, z[:n], z[n:])

    chains = []
    for di, (r_ref, v_ref, kk_ref, lw_ref, kd_ref, bd_ref, y_ref) in enumerate(
            ((rf, vf, kkf, lwf, kdf, bdf, yf_ref), (rb, vb, kkb, lwb, kdb, bdb, yb_ref))):
        reverse = di == 1
        lw = lw_ref[...]
        cum, tot = _cum_parts(lw, reverse)
        e_neg = jnp.exp(-cum)
        e_end = jnp.exp(tot - cum)
        kd = kd_ref[...].astype(F32)
        bdv = bd_ref[...].astype(F32)
        ops = dict(
            rt=(r_ref[...].astype(F32) * jnp.exp(cum)).astype(BF16),
            at=(kk_ref[...].astype(F32) * jnp.exp(cum - lw)).astype(BF16),
            kt=(kd * e_neg).astype(BF16),
            bt=(bdv * e_neg).astype(BF16),
            kh=(kd * e_end).astype(BF16),
            bh=(bdv * e_end).astype(BF16),
            v=v_ref[...],
            g_tot=jnp.exp(tot),
        )
        keep = (cp > rp - bottom) if reverse else (cp < rp + bottom)
        for hp in range(RW_HEADS // 2):
            sl = slice(hp * pw, (hp + 1) * pw)
            ch = {k: val[:, sl] for k, val in ops.items()}
            ch.update(di=di, hp=hp, sl=sl, keep=keep, y_ref=y_ref)
            chains.append(ch)

    for ch in chains:
        g = _dot_nt(jnp.concatenate([ch["at"], ch["rt"]], axis=0),
                    jnp.concatenate(halves(ch["bt"]) + halves(ch["kt"]), axis=0))
        g = jnp.where(ch["keep"], g, 0.0).astype(BF16)
        ch["l"] = g[:CHUNK, :pw]
        ch["ak"] = g[:CHUNK, pw:]
        ch["rbk"] = g[CHUNK:, :]
    for ch in chains:
        ch["p"] = _dot(ch["l"], bd(ch["l"]))
        ch["av"] = _dot(ch["ak"], bd(ch["v"]))
        ch["kv"] = pick(_dot_tn(ch["v"], ch["kh"]))
        ch["x"] = eye2 - ch["l"].astype(F32)
    for _ in range(4):
        for ch in chains:
            z = _dot(jnp.concatenate([ch["x"], ch["p"]], axis=0), bd(ch["p"]))
            ch["x"] = ch["x"] + z[:CHUNK]
            ch["p"] = z[CHUNK:]
    for ch in chains:
        ch["x"] = ch["x"] + _dot(ch["x"], bd(ch["p"]))
    for ch in chains:
        wu = _dot(ch["x"], jnp.concatenate([bd(ch["at"]), bd(ch["av"])], axis=1))
        ch["wm"] = wu[:, :pw].astype(BF16)
        ch["u0"] = -wu[:, pw:]
    for ch in chains:
        t = ch["u0"].T
        ch["u0t"] = jnp.concatenate([t[:n], t[n:]], axis=1)
    for ch in chains:
        s = s_scr[ch["di"], ch["hp"]]
        ch["s"] = s
        pr = _dot_nt(jnp.concatenate([ch["wm"], ch["rt"]], axis=0), bd(s))
        ch["u"] = ch["u0"] - pr[:CHUNK]
        ch["rs"] = pr[CHUNK:]
        ch["ut"] = ch["u0t"] - _dot_nt(s, bd(ch["wm"]))
    for ch in chains:
        y = ch["rs"] + _dot(ch["rbk"], jnp.concatenate([bd(ch["u"]), bd(ch["v"])], axis=0))
        ch["y_ref"][:, ch["sl"]] = y.astype(ch["y_ref"].dtype)
        s_scr[ch["di"], ch["hp"]] = (ch["s"] * ch["g_tot"] + ch["kv"]
                                     + _dot(ch["ut"], bd(ch["bh"])))

    @pl.when(c == pl.num_programs(1) - 1)
    def _():
        sfin_ref[...] = s_scr[...]


RW_STATE_SHAPE = (2, RW_HEADS // 2, RW_HEAD_DIM, 2 * RW_HEAD_DIM)


def _rwscan(r, v, kk, lw, kd, bd, s0):
    bsz, t, w = r.shape
    nc = t // CHUNK
    tok_f = pl.BlockSpec((None, CHUNK, w), lambda b, c: (b, c, 0))
    tok_b = pl.BlockSpec((None, CHUNK, w), lambda b, c: (b, nc - 1 - c, 0))
    dir_f = pl.BlockSpec((None, None, CHUNK, w), lambda b, c: (0, b, c, 0))
    dir_b = pl.BlockSpec((None, None, CHUNK, w), lambda b, c: (1, b, nc - 1 - c, 0))
    st = pl.BlockSpec((None,) + RW_STATE_SHAPE, lambda b, c: (b, 0, 0, 0, 0))
    y_shape = jax.ShapeDtypeStruct((bsz, t, w), BF16)
    yf, yb, sfin = pl.pallas_call(
        _rwscan_kernel,
        grid=(bsz, nc),
        in_specs=[tok_f, tok_f, tok_f, dir_f, dir_f, dir_f,
                  tok_b, tok_b, tok_b, dir_b, dir_b, dir_b, st],
        out_specs=[tok_f, tok_b, st],
        out_shape=[y_shape, y_shape, jax.ShapeDtypeStruct(s0.shape, F32)],
        scratch_shapes=[pltpu.VMEM(RW_STATE_SHAPE, F32)],
        compiler_params=_params(2),
        name="rwscan",
    )(r, v, kk, lw, kd, bd, r, v, kk, lw, kd, bd, s0)
    return yf, yb, sfin


def _rw_output(yf, yb, g, gb, gn, hind):
    y = yf.astype(F32) + yb.astype(F32)
    inv_n = 1.0 / RW_HEAD_DIM
    mean = _dot_ind_rhs(y, hind) * inv_n
    yc = y - mean
    var = _dot_ind_rhs(yc * yc, hind) * inv_n
    yn = yc * lax.rsqrt(var + RW_GN_EPS) * gn[0:1, :] + gn[1:2, :]
    return yn * g.astype(F32) + gb.astype(F32)


def _glaprep_kernel(q_cur, q_prev, q_next, ald_ref, conv, up, bias, o_q, o_k, o_v, o_la):
    j = pl.program_id(1)
    x = q_cur[...].astype(F32)
    prev, nxt = _shifted(x, q_prev[...], q_next[...], j == 0, j == pl.num_programs(1) - 1)
    y = _silu(conv[0:1, :] * prev + conv[1:2, :] * x + conv[2:3, :] * nxt)
    o_q[...] = (y[:, 0:GLA_KW] * (GLA_DK ** -0.5)).astype(o_q.dtype)
    o_k[...] = y[:, GLA_KW:2 * GLA_KW].astype(o_k.dtype)
    o_v[...] = y[:, 2 * GLA_KW:].astype(o_v.dtype)
    ald = ald_ref[...]
    for di in range(2):
        o_la[di] = _log_sigmoid(_dot(ald, up[di]) + bias[di:di + 1, :]) / GLA_TAU


def _glaprep(qkv, ald, lp, *, tm):
    bsz, t, fq = qkv.shape
    prev_spec, next_spec = _halo_specs(fq, tm, t)
    tok = lambda f: pl.BlockSpec((None, tm, f), lambda b, j: (b, j, 0))
    small = [lp["gla_conv"], lp["gla_alpha_up_pad"], lp["gla_alpha_bias"]]
    return pl.pallas_call(
        _glaprep_kernel,
        grid=(bsz, t // tm),
        in_specs=[tok(fq), prev_spec, next_spec, tok(LANES)] + [_const_spec(s.shape) for s in small],
        out_specs=[tok(GLA_KW), tok(GLA_KW), tok(GLA_VW),
                   pl.BlockSpec((2, None, tm, GLA_KW), lambda b, j: (0, b, j, 0))],
        out_shape=[jax.ShapeDtypeStruct((bsz, t, GLA_KW), BF16),
                   jax.ShapeDtypeStruct((bsz, t, GLA_KW), BF16),
                   jax.ShapeDtypeStruct((bsz, t, GLA_VW), BF16),
                   jax.ShapeDtypeStruct((2, bsz, t, GLA_KW), F32)],
        compiler_params=_params(2),
        name="glaprep",
    )(qkv, qkv, qkv, ald, *small)


def _glascan_kernel(qf, kf, vf, laf, qb, kb, vb, lab, s0_ref, of_ref, ob_ref, sfin_ref, s_scr,
                    *, nb):
    j = pl.program_id(1)

    @pl.when(j == 0)
    def _():
        s_scr[...] = s0_ref[...]

    row = lax.broadcasted_iota(jnp.int32, (CHUNK, CHUNK), 0)
    col = lax.broadcasted_iota(jnp.int32, (CHUNK, CHUNK), 1)
    chains = []
    for di, (q_ref, k_ref, v_ref, la_ref, o_ref) in enumerate(
            ((qf, kf, vf, laf, of_ref), (qb, kb, vb, lab, ob_ref))):
        reverse = di == 1
        keep = _before(row, col, reverse, True)
        for ci in (reversed(range(nb)) if reverse else range(nb)):
            rows = slice(ci * CHUNK, (ci + 1) * CHUNK)
            cum, tot = _cum_parts(la_ref[rows, :], reverse)
            k = k_ref[rows, :].astype(F32)
            q_dec = (q_ref[rows, :].astype(F32) * jnp.exp(cum)).astype(BF16)
            k_inv = (k * jnp.exp(-cum)).astype(BF16)
            k_end = (k * jnp.exp(tot - cum)).astype(BF16)
            dec = jnp.exp(tot)
            v = v_ref[rows, :]
            for h in range(GLA_HEADS):
                sk = slice(h * GLA_DK, (h + 1) * GLA_DK)
                sv = slice(h * GLA_DV, (h + 1) * GLA_DV)
                chains.append(dict(di=di, h=h, rows=rows, sv=sv, keep=keep, o_ref=o_ref,
                                   q=q_dec[:, sk], ki=k_inv[:, sk], ke=k_end[:, sk],
                                   dec=dec[:, sk], v=v[:, sv]))
    for ch in chains:
        ch["sc"] = jnp.where(ch["keep"], _dot_nt(ch["q"], ch["ki"]), 0.0)
        ch["kv"] = _dot_tn(ch["v"], ch["ke"])
    state = {}
    for ch in chains:
        key = (ch["di"], ch["h"])
        s = state[key] if key in state else s_scr[ch["di"], ch["h"]]
        ch["s"] = s
        state[key] = s * ch["dec"] + ch["kv"]
    for (di, h), s in state.items():
        s_scr[di, h] = s
    for ch in chains:
        o = _dot(ch["sc"], ch["v"]) + _dot_nt(ch["q"], ch["s"])
        ch["o_ref"][ch["rows"], ch["sv"]] = o.astype(ch["o_ref"].dtype)

    @pl.when(j == pl.num_programs(1) - 1)
    def _():
        sfin_ref[...] = s_scr[...]


def _glascan(q, k, v, la, s0):
    bsz, t, _ = q.shape
    nb = GLA_BLOCK_CHUNKS
    tb = nb * CHUNK
    nblk = t // tb
    tok_f = lambda f: pl.BlockSpec((None, tb, f), lambda b, j: (b, j, 0))
    tok_b = lambda f: pl.BlockSpec((None, tb, f), lambda b, j: (b, nblk - 1 - j, 0))
    dir_f = pl.BlockSpec((None, None, tb, GLA_KW), lambda b, j: (0, b, j, 0))
    dir_b = pl.BlockSpec((None, None, tb, GLA_KW), lambda b, j: (1, b, nblk - 1 - j, 0))
    st = pl.BlockSpec((None, 2, GLA_HEADS, GLA_DV, GLA_DK), lambda b, j: (b, 0, 0, 0, 0))
    o_shape = jax.ShapeDtypeStruct((bsz, t, GLA_VW), BF16)
    return pl.pallas_call(
        functools.partial(_glascan_kernel, nb=nb),
        grid=(bsz, nblk),
        in_specs=[tok_f(GLA_KW), tok_f(GLA_KW), tok_f(GLA_VW), dir_f,
                  tok_b(GLA_KW), tok_b(GLA_KW), tok_b(GLA_VW), dir_b, st],
        out_specs=[tok_f(GLA_VW), tok_b(GLA_VW), st],
        out_shape=[o_shape, o_shape, jax.ShapeDtypeStruct(s0.shape, F32)],
        scratch_shapes=[pltpu.VMEM((2, GLA_HEADS, GLA_DV, GLA_DK), F32)],
        compiler_params=_params(2),
        name="glascan",
    )(q, k, v, la, q, k, v, la, s0)


def _glapost_kernel(of_ref, ob_ref, gate_ref, nw, y_ref, *, y_tr, a, k):
    o = of_ref[...].astype(F32) + ob_ref[...].astype(F32)
    gate = gate_ref[...].astype(F32)
    ys = []
    for h in range(GLA_HEADS):
        sv = slice(h * GLA_DV, (h + 1) * GLA_DV)
        ys.append(_rms(o[:, sv], nw[...]) * _silu(gate[:, sv]))
    _store_tok(y_ref, jnp.concatenate(ys, axis=1), y_tr, a, k)


def _glapost(of, ob, gate, lp, *, y_tr, a, tm):
    bsz, t, w = of.shape
    k = tm // a
    tok = pl.BlockSpec((None, tm, w), lambda b, j: (b, j, 0))
    out = pl.pallas_call(
        functools.partial(_glapost_kernel, y_tr=y_tr, a=a, k=k),
        grid=(bsz, t // tm),
        in_specs=[tok, tok, tok, _const_spec((1, GLA_DV))],
        out_specs=_tok_spec(w, tm, y_tr, a),
        out_shape=_tok_shape(bsz, t, w, y_tr, a, BF16),
        compiler_params=_params(2),
        name="glapost",
    )(of, ob, gate, lp["gla_norm_w"])
    return _tok_unview(out, y_tr, t, w)


def _merge_kernel(x_ref, yf_ref, yb_ref, g_ref, gb_ref, ygla_ref, mg_ref, mod_ref, gains, gn, hind,
                  w_rwo, w_glao, w_mo, w1, w2, o_ref, *, x_tr, a, k):
    d = x_ref.shape[-1]
    x = _load_tok(x_ref, x_tr, a, k)
    mg = mg_ref[...].astype(F32)
    yrw = _rw_output(yf_ref[...], yb_ref[...], g_ref[...], gb_ref[...], gn, hind[...])
    br = (_sigmoid(mg[:, :d]) * _dot(yrw, w_rwo[...])
          + _sigmoid(mg[:, d:]) * _dot(ygla_ref[...], w_glao[...]))
    m = _dot(br, w_mo[...])
    x1 = x + mod_ref[2:3, :] * _rms(m, gains[1:2, :])
    h2 = _rms(x1, gains[2:3, :]) * (1.0 + mod_ref[4:5, :]) + mod_ref[3:4, :]
    hid = jnp.maximum(_dot(h2, w1[...]), 0.0)
    f = _dot(hid * hid, w2[...])
    x2 = x1 + mod_ref[5:6, :] * _rms(f, gains[3:4, :])
    _store_tok(o_ref, x2, x_tr, a, k)


def _merge(x, rw_parts, ygla, mg, mods, mod_row, gains, gn, hind, ws, *, x_tr, a, tm):
    bsz, t, d = x.shape
    k = tm // a
    kern = functools.partial(_merge_kernel, x_tr=x_tr, a=a, k=k)
    rw_tok = _tok_spec(RW_WIDTH, tm, False, a)
    out = pl.pallas_call(
        kern,
        grid=(bsz, t // tm),
        in_specs=[_tok_spec(d, tm, x_tr, a), rw_tok, rw_tok, rw_tok, rw_tok,
                  _tok_spec(GLA_VW, tm, False, a), _tok_spec(2 * d, tm, False, a),
                  pl.BlockSpec((None, 6, d), lambda b, j: (mod_row(b), 0, 0)),
                  _const_spec(gains.shape), _const_spec(gn.shape), _const_spec(hind.shape)]
        + [_const_spec(w.shape) for w in ws],
        out_specs=_tok_spec(d, tm, x_tr, a),
        out_shape=_tok_shape(bsz, t, d, x_tr, a, F32),
        compiler_params=_params(2),
        name="merge",
    )(_tok_view(x, x_tr, a), *rw_parts, ygla, mg, mods, gains, gn, hind, *ws)
    return _tok_unview(out, x_tr, t, d)


def _layer_params(l, p):
    w_in = p["w_in"][l]
    g0 = RW_COLS
    g1 = g0 + GLA_QKV_W
    g2 = g1 + 2 * GLA_GATE_RANK
    g3 = g2 + GLA_VW
    bf = lambda w: w.astype(BF16)
    row = lambda w: w.reshape(1, -1)
    lp = {
        "w_in_parts": [
            bf(w_in[:, :g0]),
            bf(w_in[:, g0:g1]),
            bf(jnp.pad(w_in[:, g1:g2], ((0, 0), (0, LANES - 2 * GLA_GATE_RANK)))),
            bf(w_in[:, g2:g3]),
            bf(w_in[:, g3:]),
        ],
        "rw_mu": row(p["rw_mu"][l]),
        "rw_w0": p["rw_w0"][l],
        "rw_a0": p["rw_a0"][l],
        "rw_g_up": bf(p["rw_g_up"][l]),
        "rw_k_k": row(p["rw_k_k"][l]),
        "rw_k_a": row(p["rw_k_a"][l]),
        "rw_r_k": row(p["rw_r_k"][l]),
        "rw_gn": jnp.stack([p["rw_gn_w"][l], p["rw_gn_b"][l]]),
        "gla_conv": p["gla_conv"][l],
        "gla_alpha_bias": p["gla_alpha_bias"][l],
        "gla_norm_w": row(p["gla_norm_w"][l]),
        "merge_ws": [bf(p["rw_out"][l]), bf(p["gla_out"][l]), bf(p["merge_out"][l]),
                     bf(p["mlp_w1"][l]), bf(p["mlp_w2"][l])],
        "gains": jnp.stack([p["norm_mix_pre"][l], p["norm_mix_post"][l],
                            p["norm_ffn_pre"][l], p["norm_ffn_post"][l]]),
    }
    pad_dir = lambda w, r: jnp.stack([jnp.pad(w[di], ((di * r, LANES - (di + 1) * r), (0, 0)))
                                      for di in range(2)])
    lp["rw_w_up_pad"] = bf(pad_dir(p["rw_w_up"][l], p["rw_w_up"].shape[2]))
    lp["rw_a_up_pad"] = bf(pad_dir(p["rw_a_up"][l], p["rw_a_up"].shape[2]))
    lp["gla_alpha_up_pad"] = bf(pad_dir(p["gla_alpha_up"][l], GLA_GATE_RANK))
    if l > 0:
        lp["rw_vres_down"] = bf(p["rw_vres_down"][l - 1])
        lp["rw_vres_up"] = bf(p["rw_vres_up"][l - 1])
        lp["rw_vres_bias"] = row(p["rw_vres_bias"][l - 1])
    return lp


def _head_indicator():
    h = jnp.arange(RW_WIDTH) // RW_HEAD_DIM
    return (h[:, None] == h[None, :]).astype(BF16)


def _mixer(x, mods, mod_row, vfirst, s_rw, s_gla, lp, hind, *, p_col, need_out):
    bsz, t, d = x.shape
    tm = min(TOKEN_TILE, t)
    tm_merge = min(MERGE_TILE, t)
    if p_col is None:
        x_tr, g_tr, a_p, a_q = False, False, SUBLANES, SUBLANES
    else:
        x_tr, g_tr = p_col, True
        rows = t // GRID_W
        a_p, a_q = (rows, GRID_W) if p_col else (GRID_W, rows)
    (gq, gald, gg), mg, (r, v, kk, g, gb, lw, kd, bd) = _inproj(
        x, vfirst, mods, mod_row, lp, hind, x_tr=x_tr, g_tr=g_tr, a=a_p, tm=min(PROJ_TILE, t))
    yf, yb, s_rw_out = _rwscan(r, v, kk, lw, kd, bd, s_rw)
    q, k, gv, la = _glaprep(gq, gald, lp, tm=tm)
    of, ob, s_gla_out = _glascan(q, k, gv, la, s_gla)
    if not need_out:
        return None, v, s_rw_out, s_gla_out
    ygla = _glapost(of, ob, gg, lp, y_tr=g_tr, a=a_q, tm=tm)
    x_new = _merge(x, (yf, yb, g, gb), ygla, mg, mods, mod_row, lp["gains"], lp["rw_gn"], hind,
                   lp["merge_ws"], x_tr=x_tr, a=a_p, tm=tm_merge)
    return x_new, v, s_rw_out, s_gla_out


def kernel(x, c, ctx, c_ctx, w_in, rw_mu, rw_w0, rw_w_up, rw_a0, rw_a_up, rw_g_up, rw_k_k, rw_k_a,
           rw_r_k, rw_gn_w, rw_gn_b, rw_vres_down, rw_vres_up, rw_vres_bias, rw_out, gla_conv,
           gla_alpha_up, gla_alpha_bias, gla_norm_w, gla_out, merge_out, mlp_w1, mlp_w2, ada_w,
           ada_b, norm_mix_pre, norm_mix_post, norm_ffn_pre, norm_ffn_post):
    p = dict(w_in=w_in, rw_mu=rw_mu, rw_w0=rw_w0, rw_w_up=rw_w_up, rw_a0=rw_a0, rw_a_up=rw_a_up,
             rw_g_up=rw_g_up, rw_k_k=rw_k_k, rw_k_a=rw_k_a, rw_r_k=rw_r_k.reshape(rw_r_k.shape[0], -1),
             rw_gn_w=rw_gn_w, rw_gn_b=rw_gn_b, rw_vres_down=rw_vres_down, rw_vres_up=rw_vres_up,
             rw_vres_bias=rw_vres_bias, rw_out=rw_out, gla_conv=gla_conv, gla_alpha_up=gla_alpha_up,
             gla_alpha_bias=gla_alpha_bias, gla_norm_w=gla_norm_w, gla_out=gla_out,
             merge_out=merge_out, mlp_w1=mlp_w1, mlp_w2=mlp_w2, norm_mix_pre=norm_mix_pre,
             norm_mix_post=norm_mix_post, norm_ffn_pre=norm_ffn_pre, norm_ffn_post=norm_ffn_post)
    bsz, t, d = x.shape
    t_ctx = ctx.shape[1]
    depth = w_in.shape[0]
    block = GLA_BLOCK_CHUNKS * CHUNK
    assert bsz < SUBLANES and d % LANES == 0
    assert t % (GRID_W * SUBLANES) == 0 and t % block == 0 and t_ctx % block == 0
    assert t % min(TOKEN_TILE, t) == 0 and t_ctx % min(TOKEN_TILE, t_ctx) == 0

    cc = jnp.concatenate([c, c_ctx[None, :], jnp.zeros((SUBLANES - 1 - bsz, d), F32)], axis=0)
    mods = _ada_mods(cc, ada_w, ada_b).reshape(depth, SUBLANES, 6, d)
    hind = _head_indicator()
    lat_row = lambda b: b
    ctx_row = lambda b: bsz

    x_lat, x_ctx = x, ctx
    vf_lat = vf_ctx = None
    for l in range(depth):
        last = l == depth - 1
        lp = _layer_params(l, p)
        z_rw = jnp.zeros((bsz,) + RW_STATE_SHAPE, F32)
        z_gla = jnp.zeros((bsz, 2, GLA_HEADS, GLA_DV, GLA_DK), F32)
        x_ctx_new, v_ctx, s_rw, s_gla = _mixer(
            x_ctx, mods[l], ctx_row, vf_ctx, z_rw, z_gla, lp, hind, p_col=None, need_out=not last)
        x_lat, v_lat, _, _ = _mixer(
            x_lat, mods[l], lat_row, vf_lat, s_rw, s_gla, lp, hind,
            p_col=(l % 2 == 1), need_out=True)
        if l == 0:
            vf_lat, vf_ctx = v_lat, v_ctx
        if not last:
            x_ctx = x_ctx_new
    return x_lat
```

```python
import functools

import jax
import jax.numpy as jnp
from jax import lax
from jax.experimental import pallas as pl
from jax.experimental.pallas import tpu as pltpu

F32 = jnp.float32
BF16 = jnp.bfloat16

GRID_W = 64
RMS_EPS = 1e-6
RW_HEADS = 8
RW_HEAD_DIM = 64
RW_WIDTH = RW_HEADS * RW_HEAD_DIM
RW_GN_EPS = 64e-5
RW_COLS = 1920
GLA_HEADS = 4
GLA_DK = 64
GLA_DV = 128
GLA_KW = GLA_HEADS * GLA_DK
GLA_VW = GLA_HEADS * GLA_DV
GLA_QKV_W = 2 * GLA_KW + GLA_VW
GLA_GATE_RANK = 16
GLA_TAU = 16.0
CHUNK = 64
GLA_BLOCK_CHUNKS = 4
RW_SCAN_ROWS = 4
LANES = 128
SUBLANES = 8
HALO = 16
VMEM_LIMIT = 56 * 1024 * 1024
TOKEN_TILE = 1024
PROJ_TILE = 512
MERGE_TILE = 512
GLA_IN_DTYPE = F32


def _dot(a, b):
    return jnp.dot(a.astype(BF16), b.astype(BF16), preferred_element_type=F32)


def _dot_nt(a, b):
    return lax.dot_general(a.astype(BF16), b.astype(BF16), (((1,), (1,)), ((), ())),
                           preferred_element_type=F32)


def _dot_tn(a, b):
    return lax.dot_general(a.astype(BF16), b.astype(BF16), (((0,), (0,)), ((), ())),
                           preferred_element_type=F32)


def _split2(x):
    hi = x.astype(BF16)
    lo = (x - hi.astype(F32)).astype(BF16)
    return hi, lo


def _dot_ind_rhs(x, ind):
    hi, lo = _split2(x)
    return (jnp.dot(hi, ind, preferred_element_type=F32)
            + jnp.dot(lo, ind, preferred_element_type=F32))


def _dot_ind_lhs(ind, x):
    hi, lo = _split2(x)
    return (jnp.dot(ind, hi, preferred_element_type=F32)
            + jnp.dot(ind, lo, preferred_element_type=F32))


def _sigmoid(x):
    return jax.nn.sigmoid(x)


def _silu(x):
    return x * jax.nn.sigmoid(x)


def _softplus(z):
    return jnp.maximum(z, 0.0) + jnp.log1p(jnp.exp(-jnp.abs(z)))


def _log_sigmoid(z):
    return -_softplus(-z)


def _rms(x, gain):
    return x * lax.rsqrt(jnp.mean(x * x, axis=-1, keepdims=True) + RMS_EPS) * gain


def _load_tok(ref, transposed, a, k):
    if not transposed:
        return ref[...]
    return jnp.concatenate([ref[:, i, :] for i in range(k)], axis=0)


def _store_tok(ref, val, transposed, a, k):
    val = val.astype(ref.dtype)
    if not transposed:
        ref[...] = val
    else:
        for i in range(k):
            ref[:, i, :] = val[i * a:(i + 1) * a, :]


def _tok_view(arr, transposed, a):
    if not transposed:
        return arr
    b, t, f = arr.shape
    return arr.reshape(b, a, t // a, f)


def _tok_unview(arr, transposed, t, f):
    if not transposed:
        return arr
    return arr.reshape(arr.shape[0], t, f)


def _tok_spec(f, tm, transposed, a):
    if not transposed:
        return pl.BlockSpec((None, tm, f), lambda b, j: (b, j, 0))
    return pl.BlockSpec((None, a, tm // a, f), lambda b, j: (b, 0, j, 0))


def _tok_shape(bsz, t, f, transposed, a, dtype):
    if not transposed:
        return jax.ShapeDtypeStruct((bsz, t, f), dtype)
    return jax.ShapeDtypeStruct((bsz, a, t // a, f), dtype)


def _const_spec(shape):
    nd = len(shape)
    return pl.BlockSpec(shape, lambda *_: (0,) * nd, pipeline_mode=pl.Buffered(1))


def _params(ndim):
    return pltpu.CompilerParams(dimension_semantics=("arbitrary",) * ndim,
                                vmem_limit_bytes=VMEM_LIMIT)


def _halo_specs(f, tm, t):
    sub = tm // HALO
    last = t // HALO - 1
    prev = pl.BlockSpec((None, HALO, f), lambda b, j: (b, jnp.maximum(j * sub - 1, 0), 0))
    nxt = pl.BlockSpec((None, HALO, f), lambda b, j: (b, jnp.minimum((j + 1) * sub, last), 0))
    return prev, nxt


def _shifted(cur, prev_blk, next_blk, first, last):
    tm = cur.shape[0]
    row = lax.broadcasted_iota(jnp.int32, cur.shape, 0)
    p_row = jnp.where(first, 0.0, prev_blk[HALO - 1:HALO, :].astype(F32))
    n_row = jnp.where(last, 0.0, next_blk[0:1, :].astype(F32))
    prev = jnp.where(row == 0, p_row, pltpu.roll(cur, 1, 0))
    nxt = jnp.where(row == tm - 1, n_row, pltpu.roll(cur, tm - 1, 0))
    return prev, nxt


def _ada_kernel(c_ref, w_ref, b_ref, o_ref):
    cc = c_ref[...]
    o_ref[...] = _dot(_silu(cc), w_ref[...]) + b_ref[...]


def _ada_mods(cc, ada_w, ada_b):
    nl, d, n6 = ada_w.shape
    tn = d
    return pl.pallas_call(
        _ada_kernel,
        grid=(nl, n6 // tn),
        in_specs=[pl.BlockSpec((SUBLANES, d), lambda l, n: (0, 0)),
                  pl.BlockSpec((None, d, tn), lambda l, n: (l, 0, n)),
                  pl.BlockSpec((None, 1, tn), lambda l, n: (l, 0, n))],
        out_specs=pl.BlockSpec((None, SUBLANES, tn), lambda l, n: (l, 0, n)),
        out_shape=jax.ShapeDtypeStruct((nl, SUBLANES, n6), F32),
        compiler_params=_params(2),
        name="ada",
    )(cc, ada_w, ada_b.reshape(nl, 1, n6))


def _inproj_kernel(*refs, has_vres, x_tr, g_tr, a, k):
    it = iter(refs)
    x_ref, xp_ref, xn_ref = next(it), next(it), next(it)
    vf_ref = next(it) if has_vres else None
    mod_ref, gain_ref, w_rw, w_q, w_ald, w_gg, w_mg = (next(it) for _ in range(7))
    mu, w0, w_up, a0, a_up, g_up, k_k, k_a, r_k = (next(it) for _ in range(9))
    if has_vres:
        vdown, vup, vbias = next(it), next(it), next(it)
    hind = next(it)
    o_q, o_ald, o_gg, o_mg = (next(it) for _ in range(4))
    o_r, o_v, o_kk, o_g, o_gb, o_lw, o_kd, o_bd = (next(it) for _ in range(8))

    j = pl.program_id(1)
    keep_prev = jnp.where(j == 0, 0.0, 1.0)
    keep_next = jnp.where(j == pl.num_programs(1) - 1, 0.0, 1.0)
    x = _load_tok(x_ref, x_tr, a, k)
    tm = x.shape[0]
    xp = xp_ref[SUBLANES - 1] if x_tr else xp_ref[...]
    xn = xn_ref[0] if x_tr else xn_ref[...]
    modulate = lambda z: _rms(z, gain_ref[...]) * (1.0 + mod_ref[1:2, :]) + mod_ref[0:1, :]
    h = modulate(x)
    hb = h.astype(BF16)
    h_ext = jnp.concatenate([modulate(xp) * keep_prev, h, modulate(xn) * keep_next], axis=0)
    f_ext = jnp.dot(h_ext.astype(BF16), w_rw[...], preferred_element_type=F32)
    proj = lambda w: jnp.dot(hb, w[...], preferred_element_type=F32)
    _store_tok(o_mg, proj(w_mg), False, a, k)
    _store_tok(o_q, proj(w_q), g_tr, a, k)
    _store_tok(o_ald, proj(w_ald), g_tr, a, k)
    _store_tok(o_gg, proj(w_gg), g_tr, a, k)

    f = f_ext[SUBLANES:SUBLANES + tm]
    nbr = f_ext[SUBLANES - 1:SUBLANES - 1 + tm] + f_ext[SUBLANES + 1:SUBLANES + 1 + tm]
    fs = f + mu[...] * (0.5 * nbr - f)
    w = RW_WIDTH
    r = fs[:, 0:w]
    kx = fs[:, w:2 * w]
    v = fs[:, 2 * w:3 * w]
    wd = fs[:, 3 * w:3 * w + LANES]
    ad = fs[:, 3 * w + LANES:3 * w + 2 * LANES]
    gd = fs[:, 3 * w + 2 * LANES:3 * w + 3 * LANES]
    if has_vres:
        vf = _load_tok(vf_ref, x_tr, a, k).astype(F32)
        mix = _sigmoid(vbias[...] + _dot(_dot(v, vdown[...]), vup[...]))
        v = v + (vf - v) * mix
    kk = kx * k_k[...]
    kk = kk * lax.rsqrt(_dot_ind_rhs(kk * kk, hind[...]) + 1e-12)
    twd = jnp.tanh(wd)
    ksum = None
    for di in range(2):
        wlog = -_softplus(-(w0[di:di + 1, :] + _dot(twd, w_up[di]))) - 0.5
        o_lw[di] = -jnp.exp(wlog)
        lr = _sigmoid(a0[di:di + 1, :] + _dot(ad, a_up[di]))
        kd = kx * (1.0 + (lr - 1.0) * k_a[...])
        o_kd[di] = kd.astype(o_kd.dtype)
        o_bd[di] = (lr * kk).astype(o_bd.dtype)
        ksum = kd if ksum is None else ksum + kd
    g = _dot(_sigmoid(gd), g_up[...])
    bonus = _dot_ind_rhs(r * ksum * r_k[...], hind[...]) * v
    o_r[...] = r.astype(o_r.dtype)
    o_v[...] = v.astype(o_v.dtype)
    o_kk[...] = kk.astype(o_kk.dtype)
    o_g[...] = g.astype(o_g.dtype)
    o_gb[...] = (bonus * g).astype(o_gb.dtype)


def _neighbour_specs(d, tm, t, x_tr, a):
    sub = tm // SUBLANES
    if not x_tr:
        last = t // SUBLANES - 1
        prev = pl.BlockSpec((None, SUBLANES, d), lambda b, j: (b, jnp.maximum(j * sub - 1, 0), 0))
        nxt = pl.BlockSpec((None, SUBLANES, d), lambda b, j: (b, jnp.minimum((j + 1) * sub, last), 0))
        return prev, nxt
    kb = tm // a // SUBLANES
    last = t // a // SUBLANES - 1
    shape = (None, SUBLANES, SUBLANES, d)
    prev = pl.BlockSpec(shape, lambda b, j: (b, a // SUBLANES - 1, jnp.maximum(j * kb - 1, 0), 0))
    nxt = pl.BlockSpec(shape, lambda b, j: (b, 0, jnp.minimum((j + 1) * kb, last), 0))
    return prev, nxt


def _inproj(x, vfirst, mods, mod_row, lp, hind, *, x_tr, g_tr, a, tm):
    bsz, t, d = x.shape
    w = RW_WIDTH
    k = tm // a
    has_vres = vfirst is not None
    kern = functools.partial(_inproj_kernel, has_vres=has_vres, x_tr=x_tr, g_tr=g_tr, a=a, k=k)
    xv = _tok_view(x, x_tr, a)
    prev_spec, next_spec = _neighbour_specs(d, tm, t, x_tr, a)
    args = [xv, xv, xv]
    in_specs = [_tok_spec(d, tm, x_tr, a), prev_spec, next_spec]
    if has_vres:
        args.append(_tok_view(vfirst, x_tr, a))
        in_specs.append(_tok_spec(w, tm, x_tr, a))
    args.append(mods)
    in_specs.append(pl.BlockSpec((None, 6, d), lambda b, j: (mod_row(b), 0, 0)))
    small = [lp["gains"][0:1]] + lp["w_in_parts"] + [
        lp["rw_mu"], lp["rw_w0"], lp["rw_w_up_pad"], lp["rw_a0"], lp["rw_a_up_pad"],
        lp["rw_g_up"], lp["rw_k_k"], lp["rw_k_a"], lp["rw_r_k"]]
    if has_vres:
        small += [lp["rw_vres_down"], lp["rw_vres_up"], lp["rw_vres_bias"]]
    small.append(hind)
    args += small
    in_specs += [_const_spec(s.shape) for s in small]
    gla_widths = (GLA_QKV_W, LANES, GLA_VW)
    tok = pl.BlockSpec((None, tm, w), lambda b, j: (b, j, 0))
    tok2 = pl.BlockSpec((2, None, tm, w), lambda b, j: (0, b, j, 0))
    s1 = jax.ShapeDtypeStruct((bsz, t, w), BF16)
    s2 = lambda dt: jax.ShapeDtypeStruct((2, bsz, t, w), dt)
    outs = pl.pallas_call(
        kern,
        grid=(bsz, t // tm),
        in_specs=in_specs,
        out_specs=[_tok_spec(f, tm, g_tr, a) for f in gla_widths]
        + [_tok_spec(2 * d, tm, False, a)] + [tok] * 5 + [tok2] * 3,
        out_shape=[_tok_shape(bsz, t, f, g_tr, a, GLA_IN_DTYPE) for f in gla_widths]
        + [_tok_shape(bsz, t, 2 * d, False, a, BF16)] + [s1] * 5 + [s2(F32), s2(BF16), s2(BF16)],
        compiler_params=_params(2),
        name="inproj",
    )(*args)
    gla = [_tok_unview(o, g_tr, t, f) for o, f in zip(outs[:3], gla_widths)]
    return gla, outs[3], outs[4:]


def _before(row, col, reverse, inclusive):
    if reverse:
        return (col >= row) if inclusive else (col > row)
    return (col <= row) if inclusive else (col < row)


def _cum_parts(lw, reverse):
    row = lax.broadcasted_iota(jnp.int32, (CHUNK, CHUNK), 0)
    col = lax.broadcasted_iota(jnp.int32, (CHUNK, CHUNK), 1)
    tri = jnp.where(_before(row, col, reverse, True), 1.0, 0.0).astype(BF16)
    cum = _dot_ind_lhs(tri, lw)
    tot = cum[0:1, :] if reverse else cum[CHUNK - 1:CHUNK, :]
    return cum, tot


def _rwscan_kernel(rf, vf, kkf, lwf, kdf, bdf, rb, vb, kkb, lwb, kdb, bdb, s0_ref,
                   yf_ref, yb_ref, sfin_ref, s_scr):
    c = pl.program_id(1)

    @pl.when(c == 0)
    def _():
        s_scr[...] = s0_ref[...]

    n = RW_HEAD_DIM
    pw = 2 * n
    lane = lax.broadcasted_iota(jnp.int32, (CHUNK, pw), 1)
    lo = lane < n
    row = lax.broadcasted_iota(jnp.int32, (CHUNK, pw), 0)
    eye2 = jnp.where(row == lane % n, 1.0, 0.0)
    grow = lax.broadcasted_iota(jnp.int32, (2 * CHUNK, 2 * pw), 0)
    gcol = lax.broadcasted_iota(jnp.int32, (2 * CHUNK, 2 * pw), 1)
    rp = grow % CHUNK
    cp = gcol % CHUNK
    bottom = grow // CHUNK

    def halves(z):
        zb = z.astype(BF16)
        zero = jnp.zeros_like(zb)
        return jnp.where(lo, zb, zero), jnp.where(lo, zero, zb)

    def bd(z):
        return jnp.concatenate(halves(z), axis=0)

    def pick(z):
        return jnp.where(lo, z[:n], z[n:])

    chains = []
    dirs = ((rf, vf, kkf, lwf, kdf, bdf, yf_ref), (rb, vb, kkb, lwb, kdb, bdb, yb_ref))
    for bi in range(s_scr.shape[0]):
        for di, (r_ref, v_ref, kk_ref, lw_ref, kd_ref, bd_ref, y_ref) in enumerate(dirs):
            reverse = di == 1
            lw = lw_ref[bi]
            cum, tot = _cum_parts(lw, reverse)
            e_neg = jnp.exp(-cum)
            e_end = jnp.exp(tot - cum)
            kd = kd_ref[bi].astype(F32)
            bdv = bd_ref[bi].astype(F32)
            ops = dict(
                rt=(r_ref[bi].astype(F32) * jnp.exp(cum)).astype(BF16),
                at=(kk_ref[bi].astype(F32) * jnp.exp(cum - lw)).astype(BF16),
                kt=(kd * e_neg).astype(BF16),
                bt=(bdv * e_neg).astype(BF16),
                kh=(kd * e_end).astype(BF16),
                bh=(bdv * e_end).astype(BF16),
                v=v_ref[bi],
                g_tot=jnp.exp(tot),
            )
            keep = (cp > rp - bottom) if reverse else (cp < rp + bottom)
            for hp in range(RW_HEADS // 2):
                sl = slice(hp * pw, (hp + 1) * pw)
                ch = {k: val[:, sl] for k, val in ops.items()}
                ch.update(bi=bi, di=di, hp=hp, sl=sl, keep=keep, y_ref=y_ref)
                chains.append(ch)

    for ch in chains:
        g = _dot_nt(jnp.concatenate([ch["at"], ch["rt"]], axis=0),
                    jnp.concatenate(halves(ch["bt"]) + halves(ch["kt"]), axis=0))
        g = jnp.where(ch["keep"], g, 0.0).astype(BF16)
        ch["l"] = g[:CHUNK, :pw]
        ch["ak"] = g[:CHUNK, pw:]
        ch["rbk"] = g[CHUNK:, :]
    for ch in chains:
        ch["p"] = _dot(ch["l"], bd(ch["l"]))
        ch["av"] = _dot(ch["ak"], bd(ch["v"]))
        ch["kv"] = pick(_dot_tn(ch["v"], ch["kh"]))
        ch["x"] = eye2 - ch["l"].astype(F32)
    for _ in range(4):
        for ch in chains:
            z = _dot(jnp.concatenate([ch["x"], ch["p"]], axis=0), bd(ch["p"]))
            ch["x"] = ch["x"] + z[:CHUNK]
            ch["p"] = z[CHUNK:]
    for ch in chains:
        ch["x"] = ch["x"] + _dot(ch["x"], bd(ch["p"]))
    for ch in chains:
        wu = _dot(ch["x"], jnp.concatenate([bd(ch["at"]), bd(ch["av"])], axis=1))
        ch["wm"] = wu[:, :pw].astype(BF16)
        ch["u0"] = -wu[:, pw:]
    for ch in chains:
        t = ch["u0"].T
        ch["u0t"] = jnp.concatenate([t[:n], t[n:]], axis=1)
    for ch in chains:
        s = s_scr[ch["bi"], ch["di"], ch["hp"]]
        ch["s"] = s
        pr = _dot_nt(jnp.concatenate([ch["wm"], ch["rt"]], axis=0), bd(s))
        ch["u"] = ch["u0"] - pr[:CHUNK]
        ch["rs"] = pr[CHUNK:]
        ch["ut"] = ch["u0t"] - _dot_nt(s, bd(ch["wm"]))
    for ch in chains:
        y = ch["rs"] + _dot(ch["rbk"], jnp.concatenate([bd(ch["u"]), bd(ch["v"])], axis=0))
        ch["y_ref"][ch["bi"], :, ch["sl"]] = y.astype(ch["y_ref"].dtype)
        s_scr[ch["bi"], ch["di"], ch["hp"]] = (ch["s"] * ch["g_tot"] + ch["kv"]
                                               + _dot(ch["ut"], bd(ch["bh"])))

    @pl.when(c == pl.num_programs(1) - 1)
    def _():
        sfin_ref[...] = s_scr[...]


RW_STATE_SHAPE = (2, RW_HEADS // 2, RW_HEAD_DIM, 2 * RW_HEAD_DIM)


def _rwscan(r, v, kk, lw, kd, bd, s0):
    bsz, t, w = r.shape
    nc = t // CHUNK
    nr = RW_SCAN_ROWS if bsz % RW_SCAN_ROWS == 0 else 1
    tok_f = pl.BlockSpec((nr, CHUNK, w), lambda b, c: (b, c, 0))
    tok_b = pl.BlockSpec((nr, CHUNK, w), lambda b, c: (b, nc - 1 - c, 0))
    dir_f = pl.BlockSpec((None, nr, CHUNK, w), lambda b, c: (0, b, c, 0))
    dir_b = pl.BlockSpec((None, nr, CHUNK, w), lambda b, c: (1, b, nc - 1 - c, 0))
    st = pl.BlockSpec((nr,) + RW_STATE_SHAPE, lambda b, c: (b, 0, 0, 0, 0))
    y_shape = jax.ShapeDtypeStruct((bsz, t, w), BF16)
    yf, yb, sfin = pl.pallas_call(
        _rwscan_kernel,
        grid=(bsz // nr, nc),
        in_specs=[tok_f, tok_f, tok_f, dir_f, dir_f, dir_f,
                  tok_b, tok_b, tok_b, dir_b, dir_b, dir_b, st],
        out_specs=[tok_f, tok_b, st],
        out_shape=[y_shape, y_shape, jax.ShapeDtypeStruct(s0.shape, F32)],
        scratch_shapes=[pltpu.VMEM((nr,) + RW_STATE_SHAPE, F32)],
        compiler_params=_params(2),
        name="rwscan",
    )(r, v, kk, lw, kd, bd, r, v, kk, lw, kd, bd, s0)
    return yf, yb, sfin


def _rw_output(yf, yb, g, gb, gn, hind):
    y = yf.astype(F32) + yb.astype(F32)
    inv_n = 1.0 / RW_HEAD_DIM
    mean = _dot_ind_rhs(y, hind) * inv_n
    yc = y - mean
    var = _dot_ind_rhs(yc * yc, hind) * inv_n
    yn = yc * lax.rsqrt(var + RW_GN_EPS) * gn[0:1, :] + gn[1:2, :]
    return yn * g.astype(F32) + gb.astype(F32)


def _glaprep_kernel(q_cur, q_prev, q_next, ald_ref, conv, up, bias, o_q, o_k, o_v, o_la):
    j = pl.program_id(1)
    x = q_cur[...].astype(F32)
    prev, nxt = _shifted(x, q_prev[...], q_next[...], j == 0, j == pl.num_programs(1) - 1)
    y = _silu(conv[0:1, :] * prev + conv[1:2, :] * x + conv[2:3, :] * nxt)
    o_q[...] = (y[:, 0:GLA_KW] * (GLA_DK ** -0.5)).astype(o_q.dtype)
    o_k[...] = y[:, GLA_KW:2 * GLA_KW].astype(o_k.dtype)
    o_v[...] = y[:, 2 * GLA_KW:].astype(o_v.dtype)
    ald = ald_ref[...]
    for di in range(2):
        o_la[di] = _log_sigmoid(_dot(ald, up[di]) + bias[di:di + 1, :]) / GLA_TAU


def _glaprep(qkv, ald, lp, *, tm):
    bsz, t, fq = qkv.shape
    prev_spec, next_spec = _halo_specs(fq, tm, t)
    tok = lambda f: pl.BlockSpec((None, tm, f), lambda b, j: (b, j, 0))
    small = [lp["gla_conv"], lp["gla_alpha_up_pad"], lp["gla_alpha_bias"]]
    return pl.pallas_call(
        _glaprep_kernel,
        grid=(bsz, t // tm),
        in_specs=[tok(fq), prev_spec, next_spec, tok(LANES)] + [_const_spec(s.shape) for s in small],
        out_specs=[tok(GLA_KW), tok(GLA_KW), tok(GLA_VW),
                   pl.BlockSpec((2, None, tm, GLA_KW), lambda b, j: (0, b, j, 0))],
        out_shape=[jax.ShapeDtypeStruct((bsz, t, GLA_KW), BF16),
                   jax.ShapeDtypeStruct((bsz, t, GLA_KW), BF16),
                   jax.ShapeDtypeStruct((bsz, t, GLA_VW), BF16),
                   jax.ShapeDtypeStruct((2, bsz, t, GLA_KW), F32)],
        compiler_params=_params(2),
        name="glaprep",
    )(qkv, qkv, qkv, ald, *small)


def _glascan_kernel(qf, kf, vf, laf, qb, kb, vb, lab, s0_ref, of_ref, ob_ref, sfin_ref, s_scr,
                    *, nb):
    j = pl.program_id(1)

    @pl.when(j == 0)
    def _():
        s_scr[...] = s0_ref[...]

    row = lax.broadcasted_iota(jnp.int32, (CHUNK, CHUNK), 0)
    col = lax.broadcasted_iota(jnp.int32, (CHUNK, CHUNK), 1)
    chains = []
    for di, (q_ref, k_ref, v_ref, la_ref, o_ref) in enumerate(
            ((qf, kf, vf, laf, of_ref), (qb, kb, vb, lab, ob_ref))):
        reverse = di == 1
        keep = _before(row, col, reverse, True)
        for ci in (reversed(range(nb)) if reverse else range(nb)):
            rows = slice(ci * CHUNK, (ci + 1) * CHUNK)
            cum, tot = _cum_parts(la_ref[rows, :], reverse)
            k = k_ref[rows, :].astype(F32)
            q_dec = (q_ref[rows, :].astype(F32) * jnp.exp(cum)).astype(BF16)
            k_inv = (k * jnp.exp(-cum)).astype(BF16)
            k_end = (k * jnp.exp(tot - cum)).astype(BF16)
            dec = jnp.exp(tot)
            v = v_ref[rows, :]
            for h in range(GLA_HEADS):
                sk = slice(h * GLA_DK, (h + 1) * GLA_DK)
                sv = slice(h * GLA_DV, (h + 1) * GLA_DV)
                chains.append(dict(di=di, h=h, rows=rows, sv=sv, keep=keep, o_ref=o_ref,
                                   q=q_dec[:, sk], ki=k_inv[:, sk], ke=k_end[:, sk],
                                   dec=dec[:, sk], v=v[:, sv]))
    for ch in chains:
        ch["sc"] = jnp.where(ch["keep"], _dot_nt(ch["q"], ch["ki"]), 0.0)
        ch["kv"] = _dot_tn(ch["v"], ch["ke"])
    state = {}
    for ch in chains:
        key = (ch["di"], ch["h"])
        s = state[key] if key in state else s_scr[ch["di"], ch["h"]]
        ch["s"] = s
        state[key] = s * ch["dec"] + ch["kv"]
    for (di, h), s in state.items():
        s_scr[di, h] = s
    for ch in chains:
        o = _dot(ch["sc"], ch["v"]) + _dot_nt(ch["q"], ch["s"])
        ch["o_ref"][ch["rows"], ch["sv"]] = o.astype(ch["o_ref"].dtype)

    @pl.when(j == pl.num_programs(1) - 1)
    def _():
        sfin_ref[...] = s_scr[...]


def _glascan(q, k, v, la, s0):
    bsz, t, _ = q.shape
    nb = GLA_BLOCK_CHUNKS
    tb = nb * CHUNK
    nblk = t // tb
    tok_f = lambda f: pl.BlockSpec((None, tb, f), lambda b, j: (b, j, 0))
    tok_b = lambda f: pl.BlockSpec((None, tb, f), lambda b, j: (b, nblk - 1 - j, 0))
    dir_f = pl.BlockSpec((None, None, tb, GLA_KW), lambda b, j: (0, b, j, 0))
    dir_b = pl.BlockSpec((None, None, tb, GLA_KW), lambda b, j: (1, b, nblk - 1 - j, 0))
    st = pl.BlockSpec((None, 2, GLA_HEADS, GLA_DV, GLA_DK), lambda b, j: (b, 0, 0, 0, 0))
    o_shape = jax.ShapeDtypeStruct((bsz, t, GLA_VW), BF16)
    return pl.pallas_call(
        functools.partial(_glascan_kernel, nb=nb),
        grid=(bsz, nblk),
        in_specs=[tok_f(GLA_KW), tok_f(GLA_KW), tok_f(GLA_VW), dir_f,
                  tok_b(GLA_KW), tok_b(GLA_KW), tok_b(GLA_VW), dir_b, st],
        out_specs=[tok_f(GLA_VW), tok_b(GLA_VW), st],
        out_shape=[o_shape, o_shape, jax.ShapeDtypeStruct(s0.shape, F32)],
        scratch_shapes=[pltpu.VMEM((2, GLA_HEADS, GLA_DV, GLA_DK), F32)],
        compiler_params=_params(2),
        name="glascan",
    )(q, k, v, la, q, k, v, la, s0)


def _glapost_kernel(of_ref, ob_ref, gate_ref, nw, y_ref, *, y_tr, a, k):
    o = of_ref[...].astype(F32) + ob_ref[...].astype(F32)
    gate = gate_ref[...].astype(F32)
    ys = []
    for h in range(GLA_HEADS):
        sv = slice(h * GLA_DV, (h + 1) * GLA_DV)
        ys.append(_rms(o[:, sv], nw[...]) * _silu(gate[:, sv]))
    _store_tok(y_ref, jnp.concatenate(ys, axis=1), y_tr, a, k)


def _glapost(of, ob, gate, lp, *, y_tr, a, tm):
    bsz, t, w = of.shape
    k = tm // a
    tok = pl.BlockSpec((None, tm, w), lambda b, j: (b, j, 0))
    out = pl.pallas_call(
        functools.partial(_glapost_kernel, y_tr=y_tr, a=a, k=k),
        grid=(bsz, t // tm),
        in_specs=[tok, tok, tok, _const_spec((1, GLA_DV))],
        out_specs=_tok_spec(w, tm, y_tr, a),
        out_shape=_tok_shape(bsz, t, w, y_tr, a, BF16),
        compiler_params=_params(2),
        name="glapost",
    )(of, ob, gate, lp["gla_norm_w"])
    return _tok_unview(out, y_tr, t, w)


def _merge_kernel(x_ref, yf_ref, yb_ref, g_ref, gb_ref, ygla_ref, mg_ref, mod_ref, gains, gn, hind,
                  w_rwo, w_glao, w_mo, w1, w2, o_ref, *, x_tr, a, k):
    d = x_ref.shape[-1]
    x = _load_tok(x_ref, x_tr, a, k)
    mg = mg_ref[...].astype(F32)
    yrw = _rw_output(yf_ref[...], yb_ref[...], g_ref[...], gb_ref[...], gn, hind[...])
    br = (_sigmoid(mg[:, :d]) * _dot(yrw, w_rwo[...])
          + _sigmoid(mg[:, d:]) * _dot(ygla_ref[...], w_glao[...]))
    m = _dot(br, w_mo[...])
    x1 = x + mod_ref[2:3, :] * _rms(m, gains[1:2, :])
    h2 = _rms(x1, gains[2:3, :]) * (1.0 + mod_ref[4:5, :]) + mod_ref[3:4, :]
    hid = jnp.maximum(_dot(h2, w1[...]), 0.0)
    f = _dot(hid * hid, w2[...])
    x2 = x1 + mod_ref[5:6, :] * _rms(f, gains[3:4, :])
    _store_tok(o_ref, x2, x_tr, a, k)


def _merge(x, rw_parts, ygla, mg, mods, mod_row, gains, gn, hind, ws, *, x_tr, a, tm):
    bsz, t, d = x.shape
    k = tm // a
    kern = functools.partial(_merge_kernel, x_tr=x_tr, a=a, k=k)
    rw_tok = _tok_spec(RW_WIDTH, tm, False, a)
    out = pl.pallas_call(
        kern,
        grid=(bsz, t // tm),
        in_specs=[_tok_spec(d, tm, x_tr, a), rw_tok, rw_tok, rw_tok, rw_tok,
                  _tok_spec(GLA_VW, tm, False, a), _tok_spec(2 * d, tm, False, a),
                  pl.BlockSpec((None, 6, d), lambda b, j: (mod_row(b), 0, 0)),
                  _const_spec(gains.shape), _const_spec(gn.shape), _const_spec(hind.shape)]
        + [_const_spec(w.shape) for w in ws],
        out_specs=_tok_spec(d, tm, x_tr, a),
        out_shape=_tok_shape(bsz, t, d, x_tr, a, F32),
        compiler_params=_params(2),
        name="merge",
    )(_tok_view(x, x_tr, a), *rw_parts, ygla, mg, mods, gains, gn, hind, *ws)
    return _tok_unview(out, x_tr, t, d)


def _layer_params(l, p):
    w_in = p["w_in"][l]
    g0 = RW_COLS
    g1 = g0 + GLA_QKV_W
    g2 = g1 + 2 * GLA_GATE_RANK
    g3 = g2 + GLA_VW
    bf = lambda w: w.astype(BF16)
    row = lambda w: w.reshape(1, -1)
    lp = {
        "w_in_parts": [
            bf(w_in[:, :g0]),
            bf(w_in[:, g0:g1]),
            bf(jnp.pad(w_in[:, g1:g2], ((0, 0), (0, LANES - 2 * GLA_GATE_RANK)))),
            bf(w_in[:, g2:g3]),
            bf(w_in[:, g3:]),
        ],
        "rw_mu": row(p["rw_mu"][l]),
        "rw_w0": p["rw_w0"][l],
        "rw_a0": p["rw_a0"][l],
        "rw_g_up": bf(p["rw_g_up"][l]),
        "rw_k_k": row(p["rw_k_k"][l]),
        "rw_k_a": row(p["rw_k_a"][l]),
        "rw_r_k": row(p["rw_r_k"][l]),
        "rw_gn": jnp.stack([p["rw_gn_w"][l], p["rw_gn_b"][l]]),
        "gla_conv": p["gla_conv"][l],
        "gla_alpha_bias": p["gla_alpha_bias"][l],
        "gla_norm_w": row(p["gla_norm_w"][l]),
        "merge_ws": [bf(p["rw_out"][l]), bf(p["gla_out"][l]), bf(p["merge_out"][l]),
                     bf(p["mlp_w1"][l]), bf(p["mlp_w2"][l])],
        "gains": jnp.stack([p["norm_mix_pre"][l], p["norm_mix_post"][l],
                            p["norm_ffn_pre"][l], p["norm_ffn_post"][l]]),
    }
    pad_dir = lambda w, r: jnp.stack([jnp.pad(w[di], ((di * r, LANES - (di + 1) * r), (0, 0)))
                                      for di in range(2)])
    lp["rw_w_up_pad"] = bf(pad_dir(p["rw_w_up"][l], p["rw_w_up"].shape[2]))
    lp["rw_a_up_pad"] = bf(pad_dir(p["rw_a_up"][l], p["rw_a_up"].shape[2]))
    lp["gla_alpha_up_pad"] = bf(pad_dir(p["gla_alpha_up"][l], GLA_GATE_RANK))
    if l > 0:
        lp["rw_vres_down"] = bf(p["rw_vres_down"][l - 1])
        lp["rw_vres_up"] = bf(p["rw_vres_up"][l - 1])
        lp["rw_vres_bias"] = row(p["rw_vres_bias"][l - 1])
    return lp


def _head_indicator():
    h = jnp.arange(RW_WIDTH) // RW_HEAD_DIM
    return (h[:, None] == h[None, :]).astype(BF16)


def _mixer(x, mods, mod_row, vfirst, s_rw, s_gla, lp, hind, *, p_col, need_out):
    bsz, t, d = x.shape
    tm = min(TOKEN_TILE, t)
    tm_merge = min(MERGE_TILE, t)
    if p_col is None:
        x_tr, g_tr, a_p, a_q = False, False, SUBLANES, SUBLANES
    else:
        x_tr, g_tr = p_col, True
        rows = t // GRID_W
        a_p, a_q = (rows, GRID_W) if p_col else (GRID_W, rows)
    (gq, gald, gg), mg, (r, v, kk, g, gb, lw, kd, bd) = _inproj(
        x, vfirst, mods, mod_row, lp, hind, x_tr=x_tr, g_tr=g_tr, a=a_p, tm=min(PROJ_TILE, t))
    yf, yb, s_rw_out = _rwscan(r, v, kk, lw, kd, bd, s_rw)
    q, k, gv, la = _glaprep(gq, gald, lp, tm=tm)
    of, ob, s_gla_out = _glascan(q, k, gv, la, s_gla)
    if not need_out:
        return None, v, s_rw_out, s_gla_out
    ygla = _glapost(of, ob, gg, lp, y_tr=g_tr, a=a_q, tm=tm)
    x_new = _merge(x, (yf, yb, g, gb), ygla, mg, mods, mod_row, lp["gains"], lp["rw_gn"], hind,
                   lp["merge_ws"], x_tr=x_tr, a=a_p, tm=tm_merge)
    return x_new, v, s_rw_out, s_gla_out


def kernel(x, c, ctx, c_ctx, w_in, rw_mu, rw_w0, rw_w_up, rw_a0, rw_a_up, rw_g_up, rw_k_k, rw_k_a,
           rw_r_k, rw_gn_w, rw_gn_b, rw_vres_down, rw_vres_up, rw_vres_bias, rw_out, gla_conv,
           gla_alpha_up, gla_alpha_bias, gla_norm_w, gla_out, merge_out, mlp_w1, mlp_w2, ada_w,
           ada_b, norm_mix_pre, norm_mix_post, norm_ffn_pre, norm_ffn_post):
    p = dict(w_in=w_in, rw_mu=rw_mu, rw_w0=rw_w0, rw_w_up=rw_w_up, rw_a0=rw_a0, rw_a_up=rw_a_up,
             rw_g_up=rw_g_up, rw_k_k=rw_k_k, rw_k_a=rw_k_a, rw_r_k=rw_r_k.reshape(rw_r_k.shape[0], -1),
             rw_gn_w=rw_gn_w, rw_gn_b=rw_gn_b, rw_vres_down=rw_vres_down, rw_vres_up=rw_vres_up,
             rw_vres_bias=rw_vres_bias, rw_out=rw_out, gla_conv=gla_conv, gla_alpha_up=gla_alpha_up,
             gla_alpha_bias=gla_alpha_bias, gla_norm_w=gla_norm_w, gla_out=gla_out,
             merge_out=merge_out, mlp_w1=mlp_w1, mlp_w2=mlp_w2, norm_mix_pre=norm_mix_pre,
             norm_mix_post=norm_mix_post, norm_ffn_pre=norm_ffn_pre, norm_ffn_post=norm_ffn_post)
    bsz, t, d = x.shape
    t_ctx = ctx.shape[1]
    depth = w_in.shape[0]
    block = GLA_BLOCK_CHUNKS * CHUNK
    assert bsz < SUBLANES and d % LANES == 0
    assert t % (GRID_W * SUBLANES) == 0 and t % block == 0 and t_ctx % block == 0
    assert t % min(TOKEN_TILE, t) == 0 and t_ctx % min(TOKEN_TILE, t_ctx) == 0

    cc = jnp.concatenate([c, c_ctx[None, :], jnp.zeros((SUBLANES - 1 - bsz, d), F32)], axis=0)
    mods = _ada_mods(cc, ada_w, ada_b).reshape(depth, SUBLANES, 6, d)
    hind = _head_indicator()
    lat_row = lambda b: b
    ctx_row = lambda b: bsz

    x_lat, x_ctx = x, ctx
    vf_lat = vf_ctx = None
    for l in range(depth):
        last = l == depth - 1
        lp = _layer_params(l, p)
        z_rw = jnp.zeros((bsz,) + RW_STATE_SHAPE, F32)
        z_gla = jnp.zeros((bsz, 2, GLA_HEADS, GLA_DV, GLA_DK), F32)
        x_ctx_new, v_ctx, s_rw, s_gla = _mixer(
            x_ctx, mods[l], ctx_row, vf_ctx, z_rw, z_gla, lp, hind, p_col=None, need_out=not last)
        x_lat, v_lat, _, _ = _mixer(
            x_lat, mods[l], lat_row, vf_lat, s_rw, s_gla, lp, hind,
            p_col=(l % 2 == 1), need_out=True)
        if l == 0:
            vf_lat, vf_ctx = v_lat, v_ctx
        if not last:
            x_ctx = x_ctx_new
    return x_lat
```

```python
import functools

import jax
import jax.numpy as jnp
from jax import lax
from jax.experimental import pallas as pl
from jax.experimental.pallas import tpu as pltpu

F32 = jnp.float32
BF16 = jnp.bfloat16

GRID_W = 64
RMS_EPS = 1e-6
RW_HEADS = 8
RW_HEAD_DIM = 64
RW_WIDTH = RW_HEADS * RW_HEAD_DIM
RW_GN_EPS = 64e-5
RW_COLS = 1920
GLA_HEADS = 4
GLA_DK = 64
GLA_DV = 128
GLA_KW = GLA_HEADS * GLA_DK
GLA_VW = GLA_HEADS * GLA_DV
GLA_QKV_W = 2 * GLA_KW + GLA_VW
GLA_GATE_RANK = 16
GLA_TAU = 16.0
CHUNK = 64
GLA_BLOCK_CHUNKS = 8
RW_SCAN_ROWS = 4
LANES = 128
SUBLANES = 8
HALO = 16
VMEM_LIMIT = 56 * 1024 * 1024
TOKEN_TILE = 1024
PROJ_TILE = 512
MERGE_TILE = 512
GLA_IN_DTYPE = F32


def _dot(a, b):
    return jnp.dot(a.astype(BF16), b.astype(BF16), preferred_element_type=F32)


def _dot_nt(a, b):
    return lax.dot_general(a.astype(BF16), b.astype(BF16), (((1,), (1,)), ((), ())),
                           preferred_element_type=F32)


def _dot_tn(a, b):
    return lax.dot_general(a.astype(BF16), b.astype(BF16), (((0,), (0,)), ((), ())),
                           preferred_element_type=F32)


def _split2(x):
    hi = x.astype(BF16)
    lo = (x - hi.astype(F32)).astype(BF16)
    return hi, lo


def _head_sums(x, ind):
    return jnp.dot(x.astype(BF16), ind, preferred_element_type=F32)


def _dot_ind_lhs(ind, x):
    hi, lo = _split2(x)
    return (jnp.dot(ind, hi, preferred_element_type=F32)
            + jnp.dot(ind, lo, preferred_element_type=F32))


def _sigmoid(x):
    return jax.nn.sigmoid(x)


def _silu(x):
    return x * jax.nn.sigmoid(x)


def _softplus(z):
    return jnp.maximum(z, 0.0) + jnp.log1p(jnp.exp(-jnp.abs(z)))


def _log_sigmoid(z):
    return -_softplus(-z)


def _rms(x, gain):
    return x * lax.rsqrt(jnp.mean(x * x, axis=-1, keepdims=True) + RMS_EPS) * gain


def _load_tok(ref, transposed, a, k):
    if not transposed:
        return ref[...]
    return jnp.concatenate([ref[:, i, :] for i in range(k)], axis=0)


def _store_tok(ref, val, transposed, a, k):
    val = val.astype(ref.dtype)
    if not transposed:
        ref[...] = val
    else:
        for i in range(k):
            ref[:, i, :] = val[i * a:(i + 1) * a, :]


def _tok_view(arr, transposed, a):
    if not transposed:
        return arr
    b, t, f = arr.shape
    return arr.reshape(b, a, t // a, f)


def _tok_unview(arr, transposed, t, f):
    if not transposed:
        return arr
    return arr.reshape(arr.shape[0], t, f)


def _tok_spec(f, tm, transposed, a):
    if not transposed:
        return pl.BlockSpec((None, tm, f), lambda b, j: (b, j, 0))
    return pl.BlockSpec((None, a, tm // a, f), lambda b, j: (b, 0, j, 0))


def _tok_shape(bsz, t, f, transposed, a, dtype):
    if not transposed:
        return jax.ShapeDtypeStruct((bsz, t, f), dtype)
    return jax.ShapeDtypeStruct((bsz, a, t // a, f), dtype)


def _const_spec(shape):
    nd = len(shape)
    return pl.BlockSpec(shape, lambda *_: (0,) * nd, pipeline_mode=pl.Buffered(1))


def _params(ndim):
    return pltpu.CompilerParams(dimension_semantics=("arbitrary",) * ndim,
                                vmem_limit_bytes=VMEM_LIMIT)


def _halo_specs(f, tm, t):
    sub = tm // HALO
    last = t // HALO - 1
    prev = pl.BlockSpec((None, HALO, f), lambda b, j: (b, jnp.maximum(j * sub - 1, 0), 0))
    nxt = pl.BlockSpec((None, HALO, f), lambda b, j: (b, jnp.minimum((j + 1) * sub, last), 0))
    return prev, nxt


def _shifted(cur, prev_blk, next_blk, first, last):
    tm = cur.shape[0]
    row = lax.broadcasted_iota(jnp.int32, cur.shape, 0)
    p_row = jnp.where(first, 0.0, prev_blk[HALO - 1:HALO, :].astype(F32))
    n_row = jnp.where(last, 0.0, next_blk[0:1, :].astype(F32))
    prev = jnp.where(row == 0, p_row, pltpu.roll(cur, 1, 0))
    nxt = jnp.where(row == tm - 1, n_row, pltpu.roll(cur, tm - 1, 0))
    return prev, nxt


def _ada_kernel(c_ref, w_ref, b_ref, o_ref):
    cc = c_ref[...]
    o_ref[...] = _dot(_silu(cc), w_ref[...]) + b_ref[...]


def _ada_mods(cc, ada_w, ada_b):
    nl, d, n6 = ada_w.shape
    tn = d
    return pl.pallas_call(
        _ada_kernel,
        grid=(nl, n6 // tn),
        in_specs=[pl.BlockSpec((SUBLANES, d), lambda l, n: (0, 0)),
                  pl.BlockSpec((None, d, tn), lambda l, n: (l, 0, n)),
                  pl.BlockSpec((None, 1, tn), lambda l, n: (l, 0, n))],
        out_specs=pl.BlockSpec((None, SUBLANES, tn), lambda l, n: (l, 0, n)),
        out_shape=jax.ShapeDtypeStruct((nl, SUBLANES, n6), F32),
        compiler_params=_params(2),
        name="ada",
    )(cc, ada_w, ada_b.reshape(nl, 1, n6))


def _inproj_kernel(*refs, has_vres, x_tr, g_tr, a, k):
    it = iter(refs)
    x_ref, xp_ref, xn_ref = next(it), next(it), next(it)
    vf_ref = next(it) if has_vres else None
    mod_ref, gain_ref, w_rw, w_q, w_ald, w_gg, w_mg = (next(it) for _ in range(7))
    mu, w0, w_up, a0, a_up, g_up, k_k, k_a, r_k = (next(it) for _ in range(9))
    if has_vres:
        vdown, vup, vbias = next(it), next(it), next(it)
    hind = next(it)
    o_q, o_ald, o_gg, o_mg = (next(it) for _ in range(4))
    o_r, o_v, o_kk, o_g, o_gb, o_lw, o_kd, o_bd = (next(it) for _ in range(8))

    j = pl.program_id(1)
    keep_prev = jnp.where(j == 0, 0.0, 1.0)
    keep_next = jnp.where(j == pl.num_programs(1) - 1, 0.0, 1.0)
    x = _load_tok(x_ref, x_tr, a, k)
    tm = x.shape[0]
    xp = xp_ref[SUBLANES - 1] if x_tr else xp_ref[...]
    xn = xn_ref[0] if x_tr else xn_ref[...]
    modulate = lambda z: _rms(z, gain_ref[...]) * (1.0 + mod_ref[1:2, :]) + mod_ref[0:1, :]
    h = modulate(x)
    hb = h.astype(BF16)
    h_ext = jnp.concatenate([modulate(xp) * keep_prev, h, modulate(xn) * keep_next], axis=0)
    f_ext = jnp.dot(h_ext.astype(BF16), w_rw[...], preferred_element_type=F32)
    proj = lambda w: jnp.dot(hb, w[...], preferred_element_type=F32)
    _store_tok(o_mg, proj(w_mg), False, a, k)
    _store_tok(o_q, proj(w_q), g_tr, a, k)
    _store_tok(o_ald, proj(w_ald), g_tr, a, k)
    _store_tok(o_gg, proj(w_gg), g_tr, a, k)

    f = f_ext[SUBLANES:SUBLANES + tm]
    nbr = f_ext[SUBLANES - 1:SUBLANES - 1 + tm] + f_ext[SUBLANES + 1:SUBLANES + 1 + tm]
    fs = f + mu[...] * (0.5 * nbr - f)
    w = RW_WIDTH
    r = fs[:, 0:w]
    kx = fs[:, w:2 * w]
    v = fs[:, 2 * w:3 * w]
    wd = fs[:, 3 * w:3 * w + LANES]
    ad = fs[:, 3 * w + LANES:3 * w + 2 * LANES]
    gd = fs[:, 3 * w + 2 * LANES:3 * w + 3 * LANES]
    if has_vres:
        vf = _load_tok(vf_ref, x_tr, a, k).astype(F32)
        mix = _sigmoid(vbias[...] + _dot(_dot(v, vdown[...]), vup[...]))
        v = v + (vf - v) * mix
    kk = kx * k_k[...]
    kk = kk * lax.rsqrt(_head_sums(kk * kk, hind[...]) + 1e-12)
    twd = jnp.tanh(wd)
    ksum = None
    for di in range(2):
        wlog = -_softplus(-(w0[di:di + 1, :] + _dot(twd, w_up[di]))) - 0.5
        o_lw[di] = -jnp.exp(wlog)
        lr = _sigmoid(a0[di:di + 1, :] + _dot(ad, a_up[di]))
        kd = kx * (1.0 + (lr - 1.0) * k_a[...])
        o_kd[di] = kd.astype(o_kd.dtype)
        o_bd[di] = (lr * kk).astype(o_bd.dtype)
        ksum = kd if ksum is None else ksum + kd
    g = _dot(_sigmoid(gd), g_up[...])
    bonus = _head_sums(r * ksum * r_k[...], hind[...]) * v
    o_r[...] = r.astype(o_r.dtype)
    o_v[...] = v.astype(o_v.dtype)
    o_kk[...] = kk.astype(o_kk.dtype)
    o_g[...] = g.astype(o_g.dtype)
    o_gb[...] = (bonus * g).astype(o_gb.dtype)


def _neighbour_specs(d, tm, t, x_tr, a):
    sub = tm // SUBLANES
    if not x_tr:
        last = t // SUBLANES - 1
        prev = pl.BlockSpec((None, SUBLANES, d), lambda b, j: (b, jnp.maximum(j * sub - 1, 0), 0))
        nxt = pl.BlockSpec((None, SUBLANES, d), lambda b, j: (b, jnp.minimum((j + 1) * sub, last), 0))
        return prev, nxt
    kb = tm // a // SUBLANES
    last = t // a // SUBLANES - 1
    shape = (None, SUBLANES, SUBLANES, d)
    prev = pl.BlockSpec(shape, lambda b, j: (b, a // SUBLANES - 1, jnp.maximum(j * kb - 1, 0), 0))
    nxt = pl.BlockSpec(shape, lambda b, j: (b, 0, jnp.minimum((j + 1) * kb, last), 0))
    return prev, nxt


def _inproj(x, vfirst, mods, mod_row, lp, hind, *, x_tr, g_tr, a, tm):
    bsz, t, d = x.shape
    w = RW_WIDTH
    k = tm // a
    has_vres = vfirst is not None
    kern = functools.partial(_inproj_kernel, has_vres=has_vres, x_tr=x_tr, g_tr=g_tr, a=a, k=k)
    xv = _tok_view(x, x_tr, a)
    prev_spec, next_spec = _neighbour_specs(d, tm, t, x_tr, a)
    args = [xv, xv, xv]
    in_specs = [_tok_spec(d, tm, x_tr, a), prev_spec, next_spec]
    if has_vres:
        args.append(_tok_view(vfirst, x_tr, a))
        in_specs.append(_tok_spec(w, tm, x_tr, a))
    args.append(mods)
    in_specs.append(pl.BlockSpec((None, 6, d), lambda b, j: (mod_row(b), 0, 0)))
    small = [lp["gains"][0:1]] + lp["w_in_parts"] + [
        lp["rw_mu"], lp["rw_w0"], lp["rw_w_up_pad"], lp["rw_a0"], lp["rw_a_up_pad"],
        lp["rw_g_up"], lp["rw_k_k"], lp["rw_k_a"], lp["rw_r_k"]]
    if has_vres:
        small += [lp["rw_vres_down"], lp["rw_vres_up"], lp["rw_vres_bias"]]
    small.append(hind)
    args += small
    in_specs += [_const_spec(s.shape) for s in small]
    gla_widths = (GLA_QKV_W, LANES, GLA_VW)
    tok = pl.BlockSpec((None, tm, w), lambda b, j: (b, j, 0))
    tok2 = pl.BlockSpec((2, None, tm, w), lambda b, j: (0, b, j, 0))
    s1 = jax.ShapeDtypeStruct((bsz, t, w), BF16)
    s2 = lambda dt: jax.ShapeDtypeStruct((2, bsz, t, w), dt)
    outs = pl.pallas_call(
        kern,
        grid=(bsz, t // tm),
        in_specs=in_specs,
        out_specs=[_tok_spec(f, tm, g_tr, a) for f in gla_widths]
        + [_tok_spec(2 * d, tm, False, a)] + [tok] * 5 + [tok2] * 3,
        out_shape=[_tok_shape(bsz, t, f, g_tr, a, GLA_IN_DTYPE) for f in gla_widths]
        + [_tok_shape(bsz, t, 2 * d, False, a, BF16)] + [s1] * 5 + [s2(F32), s2(BF16), s2(BF16)],
        compiler_params=_params(2),
        name="inproj",
    )(*args)
    gla = [_tok_unview(o, g_tr, t, f) for o, f in zip(outs[:3], gla_widths)]
    return gla, outs[3], outs[4:]


def _before(row, col, reverse, inclusive):
    if reverse:
        return (col >= row) if inclusive else (col > row)
    return (col <= row) if inclusive else (col < row)


def _cum_parts(lw, reverse):
    row = lax.broadcasted_iota(jnp.int32, (CHUNK, CHUNK), 0)
    col = lax.broadcasted_iota(jnp.int32, (CHUNK, CHUNK), 1)
    tri = jnp.where(_before(row, col, reverse, True), 1.0, 0.0).astype(BF16)
    cum = _dot_ind_lhs(tri, lw)
    tot = cum[0:1, :] if reverse else cum[CHUNK - 1:CHUNK, :]
    return cum, tot


def _rwscan_kernel(rf, vf, kkf, lwf, kdf, bdf, rb, vb, kkb, lwb, kdb, bdb, s0_ref,
                   yf_ref, yb_ref, sfin_ref, s_scr):
    c = pl.program_id(1)

    @pl.when(c == 0)
    def _():
        s_scr[...] = s0_ref[...]

    n = RW_HEAD_DIM
    pw = 2 * n
    lane = lax.broadcasted_iota(jnp.int32, (CHUNK, pw), 1)
    lo = lane < n
    row = lax.broadcasted_iota(jnp.int32, (CHUNK, pw), 0)
    eye2 = jnp.where(row == lane % n, 1.0, 0.0)
    grow = lax.broadcasted_iota(jnp.int32, (2 * CHUNK, 2 * pw), 0)
    gcol = lax.broadcasted_iota(jnp.int32, (2 * CHUNK, 2 * pw), 1)
    rp = grow % CHUNK
    cp = gcol % CHUNK
    bottom = grow // CHUNK

    def halves(z):
        zb = z.astype(BF16)
        zero = jnp.zeros_like(zb)
        return jnp.where(lo, zb, zero), jnp.where(lo, zero, zb)

    def bd(z):
        return jnp.concatenate(halves(z), axis=0)

    def pick(z):
        return jnp.where(lo, z[:n], z[n:])

    chains = []
    dirs = ((rf, vf, kkf, lwf, kdf, bdf, yf_ref), (rb, vb, kkb, lwb, kdb, bdb, yb_ref))
    for bi in range(s_scr.shape[0]):
        for di, (r_ref, v_ref, kk_ref, lw_ref, kd_ref, bd_ref, y_ref) in enumerate(dirs):
            reverse = di == 1
            lw = lw_ref[bi]
            cum, tot = _cum_parts(lw, reverse)
            e_neg = jnp.exp(-cum)
            e_end = jnp.exp(tot - cum)
            kd = kd_ref[bi].astype(F32)
            bdv = bd_ref[bi].astype(F32)
            ops = dict(
                rt=(r_ref[bi].astype(F32) * jnp.exp(cum)).astype(BF16),
                at=(kk_ref[bi].astype(F32) * jnp.exp(cum - lw)).astype(BF16),
                kt=(kd * e_neg).astype(BF16),
                bt=(bdv * e_neg).astype(BF16),
                kh=(kd * e_end).astype(BF16),
                bh=(bdv * e_end).astype(BF16),
                v=v_ref[bi],
                g_tot=jnp.exp(tot),
            )
            keep = (cp > rp - bottom) if reverse else (cp < rp + bottom)
            for hp in range(RW_HEADS // 2):
                sl = slice(hp * pw, (hp + 1) * pw)
                ch = {k: val[:, sl] for k, val in ops.items()}
                ch.update(bi=bi, di=di, hp=hp, sl=sl, keep=keep, y_ref=y_ref)
                chains.append(ch)

    for ch in chains:
        g = _dot_nt(jnp.concatenate([ch["at"], ch["rt"]], axis=0),
                    jnp.concatenate(halves(ch["bt"]) + halves(ch["kt"]), axis=0))
        g = jnp.where(ch["keep"], g, 0.0).astype(BF16)
        ch["l"] = g[:CHUNK, :pw]
        ch["ak"] = g[:CHUNK, pw:]
        ch["rbk"] = g[CHUNK:, :]
    for ch in chains:
        ch["p"] = _dot(ch["l"], bd(ch["l"]))
        ch["av"] = _dot(ch["ak"], bd(ch["v"]))
        ch["kv"] = pick(_dot_tn(ch["v"], ch["kh"]))
        ch["x"] = eye2 - ch["l"].astype(F32)
    for _ in range(4):
        for ch in chains:
            z = _dot(jnp.concatenate([ch["x"], ch["p"]], axis=0), bd(ch["p"]))
            ch["x"] = ch["x"] + z[:CHUNK]
            ch["p"] = z[CHUNK:]
    for ch in chains:
        ch["x"] = ch["x"] + _dot(ch["x"], bd(ch["p"]))
    for ch in chains:
        wu = _dot(ch["x"], jnp.concatenate([bd(ch["at"]), bd(ch["av"])], axis=1))
        ch["wm"] = wu[:, :pw].astype(BF16)
        ch["u0"] = -wu[:, pw:]
    for ch in chains:
        t = ch["u0"].T
        ch["u0t"] = jnp.concatenate([t[:n], t[n:]], axis=1)
    for ch in chains:
        s = s_scr[ch["bi"], ch["di"], ch["hp"]]
        ch["s"] = s
        pr = _dot_nt(jnp.concatenate([ch["wm"], ch["rt"]], axis=0), bd(s))
        ch["u"] = ch["u0"] - pr[:CHUNK]
        ch["rs"] = pr[CHUNK:]
        ch["ut"] = ch["u0t"] - _dot_nt(s, bd(ch["wm"]))
    for ch in chains:
        y = ch["rs"] + _dot(ch["rbk"], jnp.concatenate([bd(ch["u"]), bd(ch["v"])], axis=0))
        ch["y_ref"][ch["bi"], :, ch["sl"]] = y.astype(ch["y_ref"].dtype)
        s_scr[ch["bi"], ch["di"], ch["hp"]] = (ch["s"] * ch["g_tot"] + ch["kv"]
                                               + _dot(ch["ut"], bd(ch["bh"])))

    @pl.when(c == pl.num_programs(1) - 1)
    def _():
        sfin_ref[...] = s_scr[...]


RW_STATE_SHAPE = (2, RW_HEADS // 2, RW_HEAD_DIM, 2 * RW_HEAD_DIM)


def _rwscan(r, v, kk, lw, kd, bd, s0):
    bsz, t, w = r.shape
    nc = t // CHUNK
    nr = RW_SCAN_ROWS if bsz % RW_SCAN_ROWS == 0 else 1
    tok_f = pl.BlockSpec((nr, CHUNK, w), lambda b, c: (b, c, 0))
    tok_b = pl.BlockSpec((nr, CHUNK, w), lambda b, c: (b, nc - 1 - c, 0))
    dir_f = pl.BlockSpec((None, nr, CHUNK, w), lambda b, c: (0, b, c, 0))
    dir_b = pl.BlockSpec((None, nr, CHUNK, w), lambda b, c: (1, b, nc - 1 - c, 0))
    st = pl.BlockSpec((nr,) + RW_STATE_SHAPE, lambda b, c: (b, 0, 0, 0, 0))
    y_shape = jax.ShapeDtypeStruct((bsz, t, w), BF16)
    yf, yb, sfin = pl.pallas_call(
        _rwscan_kernel,
        grid=(bsz // nr, nc),
        in_specs=[tok_f, tok_f, tok_f, dir_f, dir_f, dir_f,
                  tok_b, tok_b, tok_b, dir_b, dir_b, dir_b, st],
        out_specs=[tok_f, tok_b, st],
        out_shape=[y_shape, y_shape, jax.ShapeDtypeStruct(s0.shape, F32)],
        scratch_shapes=[pltpu.VMEM((nr,) + RW_STATE_SHAPE, F32)],
        compiler_params=_params(2),
        name="rwscan",
    )(r, v, kk, lw, kd, bd, r, v, kk, lw, kd, bd, s0)
    return yf, yb, sfin


def _rw_output(yf, yb, g, gb, gn, hind):
    y = yf.astype(F32) + yb.astype(F32)
    inv_n = 1.0 / RW_HEAD_DIM
    mean = _head_sums(y, hind) * inv_n
    yc = y - mean
    var = _head_sums(yc * yc, hind) * inv_n
    yn = yc * lax.rsqrt(var + RW_GN_EPS) * gn[0:1, :] + gn[1:2, :]
    return yn * g.astype(F32) + gb.astype(F32)


def _glaprep_kernel(q_cur, q_prev, q_next, ald_ref, conv, up, bias, o_q, o_k, o_v, o_la):
    j = pl.program_id(1)
    x = q_cur[...].astype(F32)
    prev, nxt = _shifted(x, q_prev[...], q_next[...], j == 0, j == pl.num_programs(1) - 1)
    y = _silu(conv[0:1, :] * prev + conv[1:2, :] * x + conv[2:3, :] * nxt)
    o_q[...] = (y[:, 0:GLA_KW] * (GLA_DK ** -0.5)).astype(o_q.dtype)
    o_k[...] = y[:, GLA_KW:2 * GLA_KW].astype(o_k.dtype)
    o_v[...] = y[:, 2 * GLA_KW:].astype(o_v.dtype)
    ald = ald_ref[...]
    for di in range(2):
        o_la[di] = _log_sigmoid(_dot(ald, up[di]) + bias[di:di + 1, :]) / GLA_TAU


def _glaprep(qkv, ald, lp, *, tm):
    bsz, t, fq = qkv.shape
    prev_spec, next_spec = _halo_specs(fq, tm, t)
    tok = lambda f: pl.BlockSpec((None, tm, f), lambda b, j: (b, j, 0))
    small = [lp["gla_conv"], lp["gla_alpha_up_pad"], lp["gla_alpha_bias"]]
    return pl.pallas_call(
        _glaprep_kernel,
        grid=(bsz, t // tm),
        in_specs=[tok(fq), prev_spec, next_spec, tok(LANES)] + [_const_spec(s.shape) for s in small],
        out_specs=[tok(GLA_KW), tok(GLA_KW), tok(GLA_VW),
                   pl.BlockSpec((2, None, tm, GLA_KW), lambda b, j: (0, b, j, 0))],
        out_shape=[jax.ShapeDtypeStruct((bsz, t, GLA_KW), BF16),
                   jax.ShapeDtypeStruct((bsz, t, GLA_KW), BF16),
                   jax.ShapeDtypeStruct((bsz, t, GLA_VW), BF16),
                   jax.ShapeDtypeStruct((2, bsz, t, GLA_KW), F32)],
        compiler_params=_params(2),
        name="glaprep",
    )(qkv, qkv, qkv, ald, *small)


def _glascan_kernel(qf, kf, vf, laf, qb, kb, vb, lab, s0_ref, of_ref, ob_ref, sfin_ref, s_scr,
                    *, nb):
    j = pl.program_id(1)

    @pl.when(j == 0)
    def _():
        s_scr[...] = s0_ref[...]

    row = lax.broadcasted_iota(jnp.int32, (CHUNK, CHUNK), 0)
    col = lax.broadcasted_iota(jnp.int32, (CHUNK, CHUNK), 1)
    chains = []
    for di, (q_ref, k_ref, v_ref, la_ref, o_ref) in enumerate(
            ((qf, kf, vf, laf, of_ref), (qb, kb, vb, lab, ob_ref))):
        reverse = di == 1
        keep = _before(row, col, reverse, True)
        for ci in (reversed(range(nb)) if reverse else range(nb)):
            rows = slice(ci * CHUNK, (ci + 1) * CHUNK)
            cum, tot = _cum_parts(la_ref[rows, :], reverse)
            k = k_ref[rows, :].astype(F32)
            q_dec = (q_ref[rows, :].astype(F32) * jnp.exp(cum)).astype(BF16)
            k_inv = (k * jnp.exp(-cum)).astype(BF16)
            k_end = (k * jnp.exp(tot - cum)).astype(BF16)
            dec = jnp.exp(tot)
            v = v_ref[rows, :]
            for h in range(GLA_HEADS):
                sk = slice(h * GLA_DK, (h + 1) * GLA_DK)
                sv = slice(h * GLA_DV, (h + 1) * GLA_DV)
                chains.append(dict(di=di, h=h, rows=rows, sv=sv, keep=keep, o_ref=o_ref,
                                   q=q_dec[:, sk], ki=k_inv[:, sk], ke=k_end[:, sk],
                                   dec=dec[:, sk], v=v[:, sv]))
    for ch in chains:
        ch["sc"] = jnp.where(ch["keep"], _dot_nt(ch["q"], ch["ki"]), 0.0)
        ch["kv"] = _dot_tn(ch["v"], ch["ke"])
    state = {}
    for ch in chains:
        key = (ch["di"], ch["h"])
        s = state[key] if key in state else s_scr[ch["di"], ch["h"]]
        ch["s"] = s
        state[key] = s * ch["dec"] + ch["kv"]
    for (di, h), s in state.items():
        s_scr[di, h] = s
    for ch in chains:
        o = _dot(ch["sc"], ch["v"]) + _dot_nt(ch["q"], ch["s"])
        ch["o_ref"][ch["rows"], ch["sv"]] = o.astype(ch["o_ref"].dtype)

    @pl.when(j == pl.num_programs(1) - 1)
    def _():
        sfin_ref[...] = s_scr[...]


def _glascan(q, k, v, la, s0):
    bsz, t, _ = q.shape
    nb = min(GLA_BLOCK_CHUNKS, t // CHUNK)
    tb = nb * CHUNK
    assert t % tb == 0
    nblk = t // tb
    tok_f = lambda f: pl.BlockSpec((None, tb, f), lambda b, j: (b, j, 0))
    tok_b = lambda f: pl.BlockSpec((None, tb, f), lambda b, j: (b, nblk - 1 - j, 0))
    dir_f = pl.BlockSpec((None, None, tb, GLA_KW), lambda b, j: (0, b, j, 0))
    dir_b = pl.BlockSpec((None, None, tb, GLA_KW), lambda b, j: (1, b, nblk - 1 - j, 0))
    st = pl.BlockSpec((None, 2, GLA_HEADS, GLA_DV, GLA_DK), lambda b, j: (b, 0, 0, 0, 0))
    o_shape = jax.ShapeDtypeStruct((bsz, t, GLA_VW), BF16)
    return pl.pallas_call(
        functools.partial(_glascan_kernel, nb=nb),
        grid=(bsz, nblk),
        in_specs=[tok_f(GLA_KW), tok_f(GLA_KW), tok_f(GLA_VW), dir_f,
                  tok_b(GLA_KW), tok_b(GLA_KW), tok_b(GLA_VW), dir_b, st],
        out_specs=[tok_f(GLA_VW), tok_b(GLA_VW), st],
        out_shape=[o_shape, o_shape, jax.ShapeDtypeStruct(s0.shape, F32)],
        scratch_shapes=[pltpu.VMEM((2, GLA_HEADS, GLA_DV, GLA_DK), F32)],
        compiler_params=_params(2),
        name="glascan",
    )(q, k, v, la, q, k, v, la, s0)


def _glapost_kernel(of_ref, ob_ref, gate_ref, nw, y_ref, *, y_tr, a, k):
    o = of_ref[...].astype(F32) + ob_ref[...].astype(F32)
    gate = gate_ref[...].astype(F32)
    ys = []
    for h in range(GLA_HEADS):
        sv = slice(h * GLA_DV, (h + 1) * GLA_DV)
        ys.append(_rms(o[:, sv], nw[...]) * _silu(gate[:, sv]))
    _store_tok(y_ref, jnp.concatenate(ys, axis=1), y_tr, a, k)


def _glapost(of, ob, gate, lp, *, y_tr, a, tm):
    bsz, t, w = of.shape
    k = tm // a
    tok = pl.BlockSpec((None, tm, w), lambda b, j: (b, j, 0))
    out = pl.pallas_call(
        functools.partial(_glapost_kernel, y_tr=y_tr, a=a, k=k),
        grid=(bsz, t // tm),
        in_specs=[tok, tok, tok, _const_spec((1, GLA_DV))],
        out_specs=_tok_spec(w, tm, y_tr, a),
        out_shape=_tok_shape(bsz, t, w, y_tr, a, BF16),
        compiler_params=_params(2),
        name="glapost",
    )(of, ob, gate, lp["gla_norm_w"])
    return _tok_unview(out, y_tr, t, w)


def _merge_kernel(x_ref, yf_ref, yb_ref, g_ref, gb_ref, ygla_ref, mg_ref, mod_ref, gains, gn, hind,
                  w_rwo, w_glao, w_mo, w1, w2, o_ref, *, x_tr, a, k):
    d = x_ref.shape[-1]
    x = _load_tok(x_ref, x_tr, a, k)
    mg = mg_ref[...].astype(F32)
    yrw = _rw_output(yf_ref[...], yb_ref[...], g_ref[...], gb_ref[...], gn, hind[...])
    br = (_sigmoid(mg[:, :d]) * _dot(yrw, w_rwo[...])
          + _sigmoid(mg[:, d:]) * _dot(ygla_ref[...], w_glao[...]))
    m = _dot(br, w_mo[...])
    x1 = x + mod_ref[2:3, :] * _rms(m, gains[1:2, :])
    h2 = _rms(x1, gains[2:3, :]) * (1.0 + mod_ref[4:5, :]) + mod_ref[3:4, :]
    hid = jnp.maximum(_dot(h2, w1[...]), 0.0)
    f = _dot(hid * hid, w2[...])
    x2 = x1 + mod_ref[5:6, :] * _rms(f, gains[3:4, :])
    _store_tok(o_ref, x2, x_tr, a, k)


def _merge(x, rw_parts, ygla, mg, mods, mod_row, gains, gn, hind, ws, *, x_tr, a, tm):
    bsz, t, d = x.shape
    k = tm // a
    kern = functools.partial(_merge_kernel, x_tr=x_tr, a=a, k=k)
    rw_tok = _tok_spec(RW_WIDTH, tm, False, a)
    out = pl.pallas_call(
        kern,
        grid=(bsz, t // tm),
        in_specs=[_tok_spec(d, tm, x_tr, a), rw_tok, rw_tok, rw_tok, rw_tok,
                  _tok_spec(GLA_VW, tm, False, a), _tok_spec(2 * d, tm, False, a),
                  pl.BlockSpec((None, 6, d), lambda b, j: (mod_row(b), 0, 0)),
                  _const_spec(gains.shape), _const_spec(gn.shape), _const_spec(hind.shape)]
        + [_const_spec(w.shape) for w in ws],
        out_specs=_tok_spec(d, tm, x_tr, a),
        out_shape=_tok_shape(bsz, t, d, x_tr, a, F32),
        compiler_params=_params(2),
        name="merge",
    )(_tok_view(x, x_tr, a), *rw_parts, ygla, mg, mods, gains, gn, hind, *ws)
    return _tok_unview(out, x_tr, t, d)


def _layer_params(l, p):
    w_in = p["w_in"][l]
    g0 = RW_COLS
    g1 = g0 + GLA_QKV_W
    g2 = g1 + 2 * GLA_GATE_RANK
    g3 = g2 + GLA_VW
    bf = lambda w: w.astype(BF16)
    row = lambda w: w.reshape(1, -1)
    lp = {
        "w_in_parts": [
            bf(w_in[:, :g0]),
            bf(w_in[:, g0:g1]),
            bf(jnp.pad(w_in[:, g1:g2], ((0, 0), (0, LANES - 2 * GLA_GATE_RANK)))),
            bf(w_in[:, g2:g3]),
            bf(w_in[:, g3:]),
        ],
        "rw_mu": row(p["rw_mu"][l]),
        "rw_w0": p["rw_w0"][l],
        "rw_a0": p["rw_a0"][l],
        "rw_g_up": bf(p["rw_g_up"][l]),
        "rw_k_k": row(p["rw_k_k"][l]),
        "rw_k_a": row(p["rw_k_a"][l]),
        "rw_r_k": row(p["rw_r_k"][l]),
        "rw_gn": jnp.stack([p["rw_gn_w"][l], p["rw_gn_b"][l]]),
        "gla_conv": p["gla_conv"][l],
        "gla_alpha_bias": p["gla_alpha_bias"][l],
        "gla_norm_w": row(p["gla_norm_w"][l]),
        "merge_ws": [bf(p["rw_out"][l]), bf(p["gla_out"][l]), bf(p["merge_out"][l]),
                     bf(p["mlp_w1"][l]), bf(p["mlp_w2"][l])],
        "gains": jnp.stack([p["norm_mix_pre"][l], p["norm_mix_post"][l],
                            p["norm_ffn_pre"][l], p["norm_ffn_post"][l]]),
    }
    pad_dir = lambda w, r: jnp.stack([jnp.pad(w[di], ((di * r, LANES - (di + 1) * r), (0, 0)))
                                      for di in range(2)])
    lp["rw_w_up_pad"] = bf(pad_dir(p["rw_w_up"][l], p["rw_w_up"].shape[2]))
    lp["rw_a_up_pad"] = bf(pad_dir(p["rw_a_up"][l], p["rw_a_up"].shape[2]))
    lp["gla_alpha_up_pad"] = bf(pad_dir(p["gla_alpha_up"][l], GLA_GATE_RANK))
    if l > 0:
        lp["rw_vres_down"] = bf(p["rw_vres_down"][l - 1])
        lp["rw_vres_up"] = bf(p["rw_vres_up"][l - 1])
        lp["rw_vres_bias"] = row(p["rw_vres_bias"][l - 1])
    return lp


def _head_indicator():
    h = jnp.arange(RW_WIDTH) // RW_HEAD_DIM
    return (h[:, None] == h[None, :]).astype(BF16)


def _mixer(x, mods, mod_row, vfirst, s_rw, s_gla, lp, hind, *, p_col, need_out):
    bsz, t, d = x.shape
    tm = min(TOKEN_TILE, t)
    tm_merge = min(MERGE_TILE, t)
    if p_col is None:
        x_tr, g_tr, a_p, a_q = False, False, SUBLANES, SUBLANES
    else:
        x_tr, g_tr = p_col, True
        rows = t // GRID_W
        a_p, a_q = (rows, GRID_W) if p_col else (GRID_W, rows)
    (gq, gald, gg), mg, (r, v, kk, g, gb, lw, kd, bd) = _inproj(
        x, vfirst, mods, mod_row, lp, hind, x_tr=x_tr, g_tr=g_tr, a=a_p, tm=min(PROJ_TILE, t))
    yf, yb, s_rw_out = _rwscan(r, v, kk, lw, kd, bd, s_rw)
    q, k, gv, la = _glaprep(gq, gald, lp, tm=tm)
    of, ob, s_gla_out = _glascan(q, k, gv, la, s_gla)
    if not need_out:
        return None, v, s_rw_out, s_gla_out
    ygla = _glapost(of, ob, gg, lp, y_tr=g_tr, a=a_q, tm=tm)
    x_new = _merge(x, (yf, yb, g, gb), ygla, mg, mods, mod_row, lp["gains"], lp["rw_gn"], hind,
                   lp["merge_ws"], x_tr=x_tr, a=a_p, tm=tm_merge)
    return x_new, v, s_rw_out, s_gla_out


def kernel(x, c, ctx, c_ctx, w_in, rw_mu, rw_w0, rw_w_up, rw_a0, rw_a_up, rw_g_up, rw_k_k, rw_k_a,
           rw_r_k, rw_gn_w, rw_gn_b, rw_vres_down, rw_vres_up, rw_vres_bias, rw_out, gla_conv,
           gla_alpha_up, gla_alpha_bias, gla_norm_w, gla_out, merge_out, mlp_w1, mlp_w2, ada_w,
           ada_b, norm_mix_pre, norm_mix_post, norm_ffn_pre, norm_ffn_post):
    p = dict(w_in=w_in, rw_mu=rw_mu, rw_w0=rw_w0, rw_w_up=rw_w_up, rw_a0=rw_a0, rw_a_up=rw_a_up,
             rw_g_up=rw_g_up, rw_k_k=rw_k_k, rw_k_a=rw_k_a, rw_r_k=rw_r_k.reshape(rw_r_k.shape[0], -1),
             rw_gn_w=rw_gn_w, rw_gn_b=rw_gn_b, rw_vres_down=rw_vres_down, rw_vres_up=rw_vres_up,
             rw_vres_bias=rw_vres_bias, rw_out=rw_out, gla_conv=gla_conv, gla_alpha_up=gla_alpha_up,
             gla_alpha_bias=gla_alpha_bias, gla_norm_w=gla_norm_w, gla_out=gla_out,
             merge_out=merge_out, mlp_w1=mlp_w1, mlp_w2=mlp_w2, norm_mix_pre=norm_mix_pre,
             norm_mix_post=norm_mix_post, norm_ffn_pre=norm_ffn_pre, norm_ffn_post=norm_ffn_post)
    bsz, t, d = x.shape
    t_ctx = ctx.shape[1]
    depth = w_in.shape[0]
    assert bsz < SUBLANES and d % LANES == 0
    assert t % (GRID_W * SUBLANES) == 0 and t_ctx % CHUNK == 0
    assert t % min(TOKEN_TILE, t) == 0 and t_ctx % min(TOKEN_TILE, t_ctx) == 0

    cc = jnp.concatenate([c, c_ctx[None, :], jnp.zeros((SUBLANES - 1 - bsz, d), F32)], axis=0)
    mods = _ada_mods(cc, ada_w, ada_b).reshape(depth, SUBLANES, 6, d)
    hind = _head_indicator()
    lat_row = lambda b: b
    ctx_row = lambda b: bsz

    x_lat, x_ctx = x, ctx
    vf_lat = vf_ctx = None
    for l in range(depth):
        last = l == depth - 1
        lp = _layer_params(l, p)
        z_rw = jnp.zeros((bsz,) + RW_STATE_SHAPE, F32)
        z_gla = jnp.zeros((bsz, 2, GLA_HEADS, GLA_DV, GLA_DK), F32)
        x_ctx_new, v_ctx, s_rw, s_gla = _mixer(
            x_ctx, mods[l], ctx_row, vf_ctx, z_rw, z_gla, lp, hind, p_col=None, need_out=not last)
        x_lat, v_lat, _, _ = _mixer(
            x_lat, mods[l], lat_row, vf_lat, s_rw, s_gla, lp, hind,
            p_col=(l % 2 == 1), need_out=True)
        if l == 0:
            vf_lat, vf_ctx = v_lat, v_ctx
        if not last:
            x_ctx = x_ctx_new
    return x_lat
```

```python
import functools

import jax
import jax.numpy as jnp
from jax import lax
from jax.experimental import pallas as pl
from jax.experimental.pallas import tpu as pltpu

F32 = jnp.float32
BF16 = jnp.bfloat16

GRID_W = 64
RMS_EPS = 1e-6
RW_HEADS = 8
RW_HEAD_DIM = 64
RW_WIDTH = RW_HEADS * RW_HEAD_DIM
RW_GN_EPS = 64e-5
RW_COLS = 1920
GLA_HEADS = 4
GLA_DK = 64
GLA_DV = 128
GLA_KW = GLA_HEADS * GLA_DK
GLA_VW = GLA_HEADS * GLA_DV
GLA_QKV_W = 2 * GLA_KW + GLA_VW
GLA_GATE_RANK = 16
GLA_TAU = 16.0
CHUNK = 64
GLA_BLOCK_CHUNKS = 8
RW_SCAN_ROWS = 4
LANES = 128
SUBLANES = 8
HALO = 16
VMEM_LIMIT = 56 * 1024 * 1024
TOKEN_TILE = 1024
PROJ_TILE = 512
MERGE_TILE = 512
GLA_IN_DTYPE = F32


def _dot(a, b):
    return jnp.dot(a.astype(BF16), b.astype(BF16), preferred_element_type=F32)


def _dot_nt(a, b):
    return lax.dot_general(a.astype(BF16), b.astype(BF16), (((1,), (1,)), ((), ())),
                           preferred_element_type=F32)


def _dot_tn(a, b):
    return lax.dot_general(a.astype(BF16), b.astype(BF16), (((0,), (0,)), ((), ())),
                           preferred_element_type=F32)


def _split2(x):
    hi = x.astype(BF16)
    lo = (x - hi.astype(F32)).astype(BF16)
    return hi, lo


def _head_sums(x, ind):
    return jnp.dot(x.astype(BF16), ind, preferred_element_type=F32)


def _dot_ind_lhs(ind, x):
    hi, lo = _split2(x)
    return (jnp.dot(ind, hi, preferred_element_type=F32)
            + jnp.dot(ind, lo, preferred_element_type=F32))


def _sigmoid(x):
    return jax.nn.sigmoid(x)


def _silu(x):
    return x * jax.nn.sigmoid(x)


def _softplus(z):
    return jnp.maximum(z, 0.0) + jnp.log1p(jnp.exp(-jnp.abs(z)))


def _log_sigmoid(z):
    return -_softplus(-z)


def _rms(x, gain):
    return x * lax.rsqrt(jnp.mean(x * x, axis=-1, keepdims=True) + RMS_EPS) * gain


def _load_tok(ref, transposed, a, k):
    if not transposed:
        return ref[...]
    return jnp.concatenate([ref[:, i, :] for i in range(k)], axis=0)


def _store_tok(ref, val, transposed, a, k):
    val = val.astype(ref.dtype)
    if not transposed:
        ref[...] = val
    else:
        for i in range(k):
            ref[:, i, :] = val[i * a:(i + 1) * a, :]


def _tok_view(arr, transposed, a):
    if not transposed:
        return arr
    b, t, f = arr.shape
    return arr.reshape(b, a, t // a, f)


def _tok_unview(arr, transposed, t, f):
    if not transposed:
        return arr
    return arr.reshape(arr.shape[0], t, f)


def _tok_spec(f, tm, transposed, a):
    if not transposed:
        return pl.BlockSpec((None, tm, f), lambda b, j: (b, j, 0))
    return pl.BlockSpec((None, a, tm // a, f), lambda b, j: (b, 0, j, 0))


def _tok_shape(bsz, t, f, transposed, a, dtype):
    if not transposed:
        return jax.ShapeDtypeStruct((bsz, t, f), dtype)
    return jax.ShapeDtypeStruct((bsz, a, t // a, f), dtype)


def _const_spec(shape):
    nd = len(shape)
    return pl.BlockSpec(shape, lambda *_: (0,) * nd, pipeline_mode=pl.Buffered(1))


def _params(ndim):
    return pltpu.CompilerParams(dimension_semantics=("arbitrary",) * ndim,
                                vmem_limit_bytes=VMEM_LIMIT)


def _halo_specs(f, tm, t):
    sub = tm // HALO
    last = t // HALO - 1
    prev = pl.BlockSpec((None, HALO, f), lambda b, j: (b, jnp.maximum(j * sub - 1, 0), 0))
    nxt = pl.BlockSpec((None, HALO, f), lambda b, j: (b, jnp.minimum((j + 1) * sub, last), 0))
    return prev, nxt


def _shifted(cur, prev_blk, next_blk, first, last):
    tm = cur.shape[0]
    row = lax.broadcasted_iota(jnp.int32, cur.shape, 0)
    p_row = jnp.where(first, 0.0, prev_blk[HALO - 1:HALO, :].astype(F32))
    n_row = jnp.where(last, 0.0, next_blk[0:1, :].astype(F32))
    prev = jnp.where(row == 0, p_row, pltpu.roll(cur, 1, 0))
    nxt = jnp.where(row == tm - 1, n_row, pltpu.roll(cur, tm - 1, 0))
    return prev, nxt


def _ada_kernel(c_ref, w_ref, b_ref, o_ref):
    cc = c_ref[...]
    o_ref[...] = _dot(_silu(cc), w_ref[...]) + b_ref[...]


def _ada_mods(cc, ada_w, ada_b):
    nl, d, n6 = ada_w.shape
    tn = d
    return pl.pallas_call(
        _ada_kernel,
        grid=(nl, n6 // tn),
        in_specs=[pl.BlockSpec((SUBLANES, d), lambda l, n: (0, 0)),
                  pl.BlockSpec((None, d, tn), lambda l, n: (l, 0, n)),
                  pl.BlockSpec((None, 1, tn), lambda l, n: (l, 0, n))],
        out_specs=pl.BlockSpec((None, SUBLANES, tn), lambda l, n: (l, 0, n)),
        out_shape=jax.ShapeDtypeStruct((nl, SUBLANES, n6), F32),
        compiler_params=_params(2),
        name="ada",
    )(cc, ada_w, ada_b.reshape(nl, 1, n6))


def _inproj_kernel(*refs, has_vres, x_tr, g_tr, a, k):
    it = iter(refs)
    x_ref, xp_ref, xn_ref = next(it), next(it), next(it)
    vf_ref = next(it) if has_vres else None
    mod_ref, gain_ref, w_rw, w_q, w_gg, w_mg = (next(it) for _ in range(6))
    mu, w0, w_up, a0, a_up, g_up, k_k, k_a, r_k = (next(it) for _ in range(9))
    if has_vres:
        vdown, vup, vbias = next(it), next(it), next(it)
    hind = next(it)
    o_q, o_ald, o_gg, o_mg = (next(it) for _ in range(4))
    o_r, o_v, o_kk, o_g, o_gb, o_lw, o_kd, o_bd = (next(it) for _ in range(8))

    j = pl.program_id(1)
    keep_prev = jnp.where(j == 0, 0.0, 1.0)
    keep_next = jnp.where(j == pl.num_programs(1) - 1, 0.0, 1.0)
    x = _load_tok(x_ref, x_tr, a, k)
    tm = x.shape[0]
    xp = xp_ref[SUBLANES - 1] if x_tr else xp_ref[...]
    xn = xn_ref[0] if x_tr else xn_ref[...]
    modulate = lambda z: _rms(z, gain_ref[...]) * (1.0 + mod_ref[1:2, :]) + mod_ref[0:1, :]
    h = modulate(x)
    hb = h.astype(BF16)
    h_ext = jnp.concatenate([modulate(xp) * keep_prev, h, modulate(xn) * keep_next], axis=0)
    f_ext = jnp.dot(h_ext.astype(BF16), w_rw[...], preferred_element_type=F32)
    proj = lambda w: jnp.dot(hb, w[...], preferred_element_type=F32)
    _store_tok(o_mg, proj(w_mg), False, a, k)
    _store_tok(o_q, proj(w_q), g_tr, a, k)
    _store_tok(o_ald, f_ext[SUBLANES:SUBLANES + tm, RW_COLS:], g_tr, a, k)
    _store_tok(o_gg, proj(w_gg), g_tr, a, k)

    f = f_ext[SUBLANES:SUBLANES + tm, :RW_COLS]
    nbr = (f_ext[SUBLANES - 1:SUBLANES - 1 + tm, :RW_COLS]
           + f_ext[SUBLANES + 1:SUBLANES + 1 + tm, :RW_COLS])
    fs = f + mu[...] * (0.5 * nbr - f)
    w = RW_WIDTH
    r = fs[:, 0:w]
    kx = fs[:, w:2 * w]
    v = fs[:, 2 * w:3 * w]
    wd = fs[:, 3 * w:3 * w + LANES]
    ad = fs[:, 3 * w + LANES:3 * w + 2 * LANES]
    gd = fs[:, 3 * w + 2 * LANES:3 * w + 3 * LANES]
    if has_vres:
        vf = _load_tok(vf_ref, x_tr, a, k).astype(F32)
        mix = _sigmoid(vbias[...] + _dot(_dot(v, vdown[...]), vup[...]))
        v = v + (vf - v) * mix
    kk = kx * k_k[...]
    kk = kk * lax.rsqrt(_head_sums(kk * kk, hind[...]) + 1e-12)
    twd = jnp.tanh(wd)
    ksum = None
    for di in range(2):
        wlog = -_softplus(-(w0[di:di + 1, :] + _dot(twd, w_up[di]))) - 0.5
        o_lw[di] = -jnp.exp(wlog)
        lr = _sigmoid(a0[di:di + 1, :] + _dot(ad, a_up[di]))
        kd = kx * (1.0 + (lr - 1.0) * k_a[...])
        o_kd[di] = kd.astype(o_kd.dtype)
        o_bd[di] = (lr * kk).astype(o_bd.dtype)
        ksum = kd if ksum is None else ksum + kd
    g = _dot(_sigmoid(gd), g_up[...])
    bonus = _head_sums(r * ksum * r_k[...], hind[...]) * v
    o_r[...] = r.astype(o_r.dtype)
    o_v[...] = v.astype(o_v.dtype)
    o_kk[...] = kk.astype(o_kk.dtype)
    o_g[...] = g.astype(o_g.dtype)
    o_gb[...] = (bonus * g).astype(o_gb.dtype)


def _neighbour_specs(d, tm, t, x_tr, a):
    sub = tm // SUBLANES
    if not x_tr:
        last = t // SUBLANES - 1
        prev = pl.BlockSpec((None, SUBLANES, d), lambda b, j: (b, jnp.maximum(j * sub - 1, 0), 0))
        nxt = pl.BlockSpec((None, SUBLANES, d), lambda b, j: (b, jnp.minimum((j + 1) * sub, last), 0))
        return prev, nxt
    kb = tm // a // SUBLANES
    last = t // a // SUBLANES - 1
    shape = (None, SUBLANES, SUBLANES, d)
    prev = pl.BlockSpec(shape, lambda b, j: (b, a // SUBLANES - 1, jnp.maximum(j * kb - 1, 0), 0))
    nxt = pl.BlockSpec(shape, lambda b, j: (b, 0, jnp.minimum((j + 1) * kb, last), 0))
    return prev, nxt


def _inproj(x, vfirst, mods, mod_row, lp, hind, *, x_tr, g_tr, a, tm):
    bsz, t, d = x.shape
    w = RW_WIDTH
    k = tm // a
    has_vres = vfirst is not None
    kern = functools.partial(_inproj_kernel, has_vres=has_vres, x_tr=x_tr, g_tr=g_tr, a=a, k=k)
    xv = _tok_view(x, x_tr, a)
    prev_spec, next_spec = _neighbour_specs(d, tm, t, x_tr, a)
    args = [xv, xv, xv]
    in_specs = [_tok_spec(d, tm, x_tr, a), prev_spec, next_spec]
    if has_vres:
        args.append(_tok_view(vfirst, x_tr, a))
        in_specs.append(_tok_spec(w, tm, x_tr, a))
    args.append(mods)
    in_specs.append(pl.BlockSpec((None, 6, d), lambda b, j: (mod_row(b), 0, 0)))
    small = [lp["gains"][0:1]] + lp["w_in_parts"] + [
        lp["rw_mu"], lp["rw_w0"], lp["rw_w_up_pad"], lp["rw_a0"], lp["rw_a_up_pad"],
        lp["rw_g_up"], lp["rw_k_k"], lp["rw_k_a"], lp["rw_r_k"]]
    if has_vres:
        small += [lp["rw_vres_down"], lp["rw_vres_up"], lp["rw_vres_bias"]]
    small.append(hind)
    args += small
    in_specs += [_const_spec(s.shape) for s in small]
    gla_widths = (GLA_QKV_W, LANES, GLA_VW)
    tok = pl.BlockSpec((None, tm, w), lambda b, j: (b, j, 0))
    tok2 = pl.BlockSpec((2, None, tm, w), lambda b, j: (0, b, j, 0))
    s1 = jax.ShapeDtypeStruct((bsz, t, w), BF16)
    s2 = lambda dt: jax.ShapeDtypeStruct((2, bsz, t, w), dt)
    outs = pl.pallas_call(
        kern,
        grid=(bsz, t // tm),
        in_specs=in_specs,
        out_specs=[_tok_spec(f, tm, g_tr, a) for f in gla_widths]
        + [_tok_spec(2 * d, tm, False, a)] + [tok] * 5 + [tok2] * 3,
        out_shape=[_tok_shape(bsz, t, f, g_tr, a, GLA_IN_DTYPE) for f in gla_widths]
        + [_tok_shape(bsz, t, 2 * d, False, a, BF16)] + [s1] * 5 + [s2(F32), s2(BF16), s2(BF16)],
        compiler_params=_params(2),
        name="inproj",
    )(*args)
    gla = [_tok_unview(o, g_tr, t, f) for o, f in zip(outs[:3], gla_widths)]
    return gla, outs[3], outs[4:]


def _before(row, col, reverse, inclusive):
    if reverse:
        return (col >= row) if inclusive else (col > row)
    return (col <= row) if inclusive else (col < row)


def _cum_parts(lw, reverse):
    row = lax.broadcasted_iota(jnp.int32, (CHUNK, CHUNK), 0)
    col = lax.broadcasted_iota(jnp.int32, (CHUNK, CHUNK), 1)
    tri = jnp.where(_before(row, col, reverse, True), 1.0, 0.0).astype(BF16)
    cum = _dot_ind_lhs(tri, lw)
    tot = cum[0:1, :] if reverse else cum[CHUNK - 1:CHUNK, :]
    return cum, tot


def _rwscan_kernel(rf, vf, kkf, lwf, kdf, bdf, rb, vb, kkb, lwb, kdb, bdb, s0_ref,
                   yf_ref, yb_ref, sfin_ref, s_scr):
    c = pl.program_id(1)

    @pl.when(c == 0)
    def _():
        s_scr[...] = s0_ref[...]

    n = RW_HEAD_DIM
    pw = 2 * n
    lane = lax.broadcasted_iota(jnp.int32, (CHUNK, pw), 1)
    lo = lane < n
    row = lax.broadcasted_iota(jnp.int32, (CHUNK, pw), 0)
    eye2 = jnp.where(row == lane % n, 1.0, 0.0)
    grow = lax.broadcasted_iota(jnp.int32, (2 * CHUNK, 2 * pw), 0)
    gcol = lax.broadcasted_iota(jnp.int32, (2 * CHUNK, 2 * pw), 1)
    rp = grow % CHUNK
    cp = gcol % CHUNK
    bottom = grow // CHUNK

    def halves(z):
        zb = z.astype(BF16)
        zero = jnp.zeros_like(zb)
        return jnp.where(lo, zb, zero), jnp.where(lo, zero, zb)

    def bd(z):
        return jnp.concatenate(halves(z), axis=0)

    def pick(z):
        return jnp.where(lo, z[:n], z[n:])

    chains = []
    dirs = ((rf, vf, kkf, lwf, kdf, bdf, yf_ref), (rb, vb, kkb, lwb, kdb, bdb, yb_ref))
    for bi in range(s_scr.shape[0]):
        for di, (r_ref, v_ref, kk_ref, lw_ref, kd_ref, bd_ref, y_ref) in enumerate(dirs):
            reverse = di == 1
            lw = lw_ref[bi]
            cum, tot = _cum_parts(lw, reverse)
            e_neg = jnp.exp(-cum)
            e_end = jnp.exp(tot - cum)
            kd = kd_ref[bi].astype(F32)
            bdv = bd_ref[bi].astype(F32)
            ops = dict(
                rt=(r_ref[bi].astype(F32) * jnp.exp(cum)).astype(BF16),
                at=(kk_ref[bi].astype(F32) * jnp.exp(cum - lw)).astype(BF16),
                kt=(kd * e_neg).astype(BF16),
                bt=(bdv * e_neg).astype(BF16),
                kh=(kd * e_end).astype(BF16),
                bh=(bdv * e_end).astype(BF16),
                v=v_ref[bi],
                g_tot=jnp.exp(tot),
            )
            keep = (cp > rp - bottom) if reverse else (cp < rp + bottom)
            for hp in range(RW_HEADS // 2):
                sl = slice(hp * pw, (hp + 1) * pw)
                ch = {k: val[:, sl] for k, val in ops.items()}
                ch.update(bi=bi, di=di, hp=hp, sl=sl, keep=keep, y_ref=y_ref)
                chains.append(ch)

    for ch in chains:
        g = _dot_nt(jnp.concatenate([ch["at"], ch["rt"]], axis=0),
                    jnp.concatenate(halves(ch["bt"]) + halves(ch["kt"]), axis=0))
        g = jnp.where(ch["keep"], g, 0.0).astype(BF16)
        ch["l"] = g[:CHUNK, :pw]
        ch["ak"] = g[:CHUNK, pw:]
        ch["rbk"] = g[CHUNK:, :]
    for ch in chains:
        ch["p"] = _dot(ch["l"], bd(ch["l"]))
        ch["av"] = _dot(ch["ak"], bd(ch["v"]))
        ch["kv"] = pick(_dot_tn(ch["v"], ch["kh"]))
        ch["x"] = eye2 - ch["l"].astype(F32)
    for _ in range(4):
        for ch in chains:
            z = _dot(jnp.concatenate([ch["x"], ch["p"]], axis=0), bd(ch["p"]))
            ch["x"] = ch["x"] + z[:CHUNK]
            ch["p"] = z[CHUNK:]
    for ch in chains:
        ch["x"] = ch["x"] + _dot(ch["x"], bd(ch["p"]))
    for ch in chains:
        wu = _dot(ch["x"], jnp.concatenate([bd(ch["at"]), bd(ch["av"])], axis=1))
        ch["wm"] = wu[:, :pw].astype(BF16)
        ch["u0"] = -wu[:, pw:]
    for ch in chains:
        t = ch["u0"].T
        ch["u0t"] = jnp.concatenate([t[:n], t[n:]], axis=1)
    for ch in chains:
        s = s_scr[ch["bi"], ch["di"], ch["hp"]]
        ch["s"] = s
        pr = _dot_nt(jnp.concatenate([ch["wm"], ch["rt"]], axis=0), bd(s))
        ch["u"] = ch["u0"] - pr[:CHUNK]
        ch["rs"] = pr[CHUNK:]
        ch["ut"] = ch["u0t"] - _dot_nt(s, bd(ch["wm"]))
    for ch in chains:
        y = ch["rs"] + _dot(ch["rbk"], jnp.concatenate([bd(ch["u"]), bd(ch["v"])], axis=0))
        ch["y_ref"][ch["bi"], :, ch["sl"]] = y.astype(ch["y_ref"].dtype)
        s_scr[ch["bi"], ch["di"], ch["hp"]] = (ch["s"] * ch["g_tot"] + ch["kv"]
                                               + _dot(ch["ut"], bd(ch["bh"])))

    @pl.when(c == pl.num_programs(1) - 1)
    def _():
        sfin_ref[...] = s_scr[...]


RW_STATE_SHAPE = (2, RW_HEADS // 2, RW_HEAD_DIM, 2 * RW_HEAD_DIM)


def _rwscan(r, v, kk, lw, kd, bd, s0):
    bsz, t, w = r.shape
    nc = t // CHUNK
    nr = RW_SCAN_ROWS if bsz % RW_SCAN_ROWS == 0 else 1
    tok_f = pl.BlockSpec((nr, CHUNK, w), lambda b, c: (b, c, 0))
    tok_b = pl.BlockSpec((nr, CHUNK, w), lambda b, c: (b, nc - 1 - c, 0))
    dir_f = pl.BlockSpec((None, nr, CHUNK, w), lambda b, c: (0, b, c, 0))
    dir_b = pl.BlockSpec((None, nr, CHUNK, w), lambda b, c: (1, b, nc - 1 - c, 0))
    st = pl.BlockSpec((nr,) + RW_STATE_SHAPE, lambda b, c: (b, 0, 0, 0, 0))
    y_shape = jax.ShapeDtypeStruct((bsz, t, w), BF16)
    yf, yb, sfin = pl.pallas_call(
        _rwscan_kernel,
        grid=(bsz // nr, nc),
        in_specs=[tok_f, tok_f, tok_f, dir_f, dir_f, dir_f,
                  tok_b, tok_b, tok_b, dir_b, dir_b, dir_b, st],
        out_specs=[tok_f, tok_b, st],
        out_shape=[y_shape, y_shape, jax.ShapeDtypeStruct(s0.shape, F32)],
        scratch_shapes=[pltpu.VMEM((nr,) + RW_STATE_SHAPE, F32)],
        compiler_params=_params(2),
        name="rwscan",
    )(r, v, kk, lw, kd, bd, r, v, kk, lw, kd, bd, s0)
    return yf, yb, sfin


def _rw_output(yf, yb, g, gb, gn, hind):
    y = yf.astype(F32) + yb.astype(F32)
    inv_n = 1.0 / RW_HEAD_DIM
    mean = _head_sums(y, hind) * inv_n
    yc = y - mean
    var = _head_sums(yc * yc, hind) * inv_n
    yn = yc * lax.rsqrt(var + RW_GN_EPS) * gn[0:1, :] + gn[1:2, :]
    return yn * g.astype(F32) + gb.astype(F32)


GLA_STATE_SHAPE = (GLA_HEADS, GLA_DV, GLA_DK)


def _gla_block_scan(q, k, v, la, s_scr, nb, reverse, emit):
    row = lax.broadcasted_iota(jnp.int32, (CHUNK, CHUNK), 0)
    col = lax.broadcasted_iota(jnp.int32, (CHUNK, CHUNK), 1)
    keep = _before(row, col, reverse, True)
    chains = []
    for ci in (reversed(range(nb)) if reverse else range(nb)):
        rows = slice(ci * CHUNK, (ci + 1) * CHUNK)
        cum, tot = _cum_parts(la[rows, :], reverse)
        kc = k[rows, :].astype(F32)
        q_dec = (q[rows, :].astype(F32) * jnp.exp(cum)).astype(BF16)
        k_inv = (kc * jnp.exp(-cum)).astype(BF16)
        k_end = (kc * jnp.exp(tot - cum)).astype(BF16)
        dec = jnp.exp(tot)
        vc = v[rows, :]
        for h in range(GLA_HEADS):
            sk = slice(h * GLA_DK, (h + 1) * GLA_DK)
            sv = slice(h * GLA_DV, (h + 1) * GLA_DV)
            chains.append(dict(h=h, rows=rows, sv=sv, q=q_dec[:, sk], ki=k_inv[:, sk],
                               ke=k_end[:, sk], dec=dec[:, sk], v=vc[:, sv]))
    for ch in chains:
        ch["sc"] = jnp.where(keep, _dot_nt(ch["q"], ch["ki"]), 0.0)
        ch["kv"] = _dot_tn(ch["v"], ch["ke"])
    state = {}
    for ch in chains:
        s = state[ch["h"]] if ch["h"] in state else s_scr[ch["h"]]
        ch["s"] = s
        state[ch["h"]] = s * ch["dec"] + ch["kv"]
    for h, s in state.items():
        s_scr[h] = s
    for ch in chains:
        emit(ch["rows"], ch["sv"], _dot(ch["sc"], ch["v"]) + _dot_nt(ch["q"], ch["s"]))


def _glafwd_kernel(q_cur, q_prev, q_next, ald_ref, conv, up, bias, s0_ref,
                   o_q, o_k, o_v, o_la, o_of, sfin_ref, s_scr, *, nb):
    j = pl.program_id(1)

    @pl.when(j == 0)
    def _():
        s_scr[...] = s0_ref[...]

    x = q_cur[...].astype(F32)
    prev, nxt = _shifted(x, q_prev[...], q_next[...], j == 0, j == pl.num_programs(1) - 1)
    y = _silu(conv[0:1, :] * prev + conv[1:2, :] * x + conv[2:3, :] * nxt)
    q = (y[:, 0:GLA_KW] * (GLA_DK ** -0.5)).astype(BF16)
    k = y[:, GLA_KW:2 * GLA_KW].astype(BF16)
    v = y[:, 2 * GLA_KW:].astype(BF16)
    o_q[...] = q
    o_k[...] = k
    o_v[...] = v
    ald = ald_ref[...]
    log_alpha = lambda di: _log_sigmoid(_dot(ald, up[di]) + bias[di:di + 1, :]) / GLA_TAU
    o_la[...] = log_alpha(1)

    def emit(rows, lanes, o):
        o_of[rows, lanes] = o.astype(o_of.dtype)

    _gla_block_scan(q, k, v, log_alpha(0), s_scr, nb, False, emit)

    @pl.when(j == pl.num_programs(1) - 1)
    def _():
        sfin_ref[...] = s_scr[...]


def _glafwd(qkv, ald, s0, lp):
    bsz, t, fq = qkv.shape
    nb = min(GLA_BLOCK_CHUNKS, t // CHUNK)
    tb = nb * CHUNK
    assert t % tb == 0
    prev_spec, next_spec = _halo_specs(fq, tb, t)
    tok = lambda f: pl.BlockSpec((None, tb, f), lambda b, j: (b, j, 0))
    st = pl.BlockSpec((None,) + GLA_STATE_SHAPE, lambda b, j: (b, 0, 0, 0))
    small = [lp["gla_conv"], lp["gla_alpha_up_pad"], lp["gla_alpha_bias"]]
    act = lambda f, dt: jax.ShapeDtypeStruct((bsz, t, f), dt)
    return pl.pallas_call(
        functools.partial(_glafwd_kernel, nb=nb),
        grid=(bsz, t // tb),
        in_specs=[tok(fq), prev_spec, next_spec, tok(LANES)]
        + [_const_spec(s.shape) for s in small] + [st],
        out_specs=[tok(GLA_KW), tok(GLA_KW), tok(GLA_VW), tok(GLA_KW), tok(GLA_VW), st],
        out_shape=[act(GLA_KW, BF16), act(GLA_KW, BF16), act(GLA_VW, BF16), act(GLA_KW, F32),
                   act(GLA_VW, BF16), jax.ShapeDtypeStruct(s0.shape, F32)],
        scratch_shapes=[pltpu.VMEM(GLA_STATE_SHAPE, F32)],
        compiler_params=_params(2),
        name="glafwd",
    )(qkv, qkv, qkv, ald, *small, s0)


def _glabwd_kernel(q_ref, k_ref, v_ref, la_ref, of_ref, gate_ref, nw, s0_ref,
                   y_ref, sfin_ref, s_scr, o_scr, *, nb, y_tr, a, k):
    j = pl.program_id(1)

    @pl.when(j == 0)
    def _():
        s_scr[...] = s0_ref[...]

    def emit(rows, lanes, o):
        o_scr[rows, lanes] = o

    _gla_block_scan(q_ref, k_ref, v_ref, la_ref, s_scr, nb, True, emit)
    o = of_ref[...].astype(F32) + o_scr[...]
    gate = gate_ref[...].astype(F32)
    ys = []
    for h in range(GLA_HEADS):
        sv = slice(h * GLA_DV, (h + 1) * GLA_DV)
        ys.append(_rms(o[:, sv], nw[...]) * _silu(gate[:, sv]))
    _store_tok(y_ref, jnp.concatenate(ys, axis=1), y_tr, a, k)

    @pl.when(j == pl.num_programs(1) - 1)
    def _():
        sfin_ref[...] = s_scr[...]


def _glabwd(q, k, v, la, of, gate, s0, lp, *, y_tr, a):
    bsz, t, _ = q.shape
    w = GLA_VW
    tb = min(TOKEN_TILE, t)
    nb = tb // CHUNK
    nblk = t // tb
    kk = tb // a
    tok = lambda f: pl.BlockSpec((None, tb, f), lambda b, j: (b, nblk - 1 - j, 0))
    st = pl.BlockSpec((None,) + GLA_STATE_SHAPE, lambda b, j: (b, 0, 0, 0))
    if y_tr:
        y_spec = pl.BlockSpec((None, a, kk, w), lambda b, j: (b, 0, nblk - 1 - j, 0))
    else:
        y_spec = tok(w)
    y, sfin = pl.pallas_call(
        functools.partial(_glabwd_kernel, nb=nb, y_tr=y_tr, a=a, k=kk),
        grid=(bsz, nblk),
        in_specs=[tok(GLA_KW), tok(GLA_KW), tok(w), tok(GLA_KW), tok(w), tok(w),
                  _const_spec((1, GLA_DV)), st],
        out_specs=[y_spec, st],
        out_shape=[_tok_shape(bsz, t, w, y_tr, a, BF16), jax.ShapeDtypeStruct(s0.shape, F32)],
        scratch_shapes=[pltpu.VMEM(GLA_STATE_SHAPE, F32), pltpu.VMEM((tb, w), F32)],
        compiler_params=_params(2),
        name="glabwd",
    )(q, k, v, la, of, gate, lp["gla_norm_w"], s0)
    return _tok_unview(y, y_tr, t, w), sfin


def _merge_kernel(x_ref, yf_ref, yb_ref, g_ref, gb_ref, ygla_ref, mg_ref, mod_ref, gains, gn, hind,
                  w_rwo, w_glao, w_mo, w1, w2, o_ref, *, x_tr, a, k):
    d = x_ref.shape[-1]
    x = _load_tok(x_ref, x_tr, a, k)
    mg = mg_ref[...].astype(F32)
    yrw = _rw_output(yf_ref[...], yb_ref[...], g_ref[...], gb_ref[...], gn, hind[...])
    br = (_sigmoid(mg[:, :d]) * _dot(yrw, w_rwo[...])
          + _sigmoid(mg[:, d:]) * _dot(ygla_ref[...], w_glao[...]))
    m = _dot(br, w_mo[...])
    x1 = x + mod_ref[2:3, :] * _rms(m, gains[1:2, :])
    h2 = _rms(x1, gains[2:3, :]) * (1.0 + mod_ref[4:5, :]) + mod_ref[3:4, :]
    hid = jnp.maximum(_dot(h2, w1[...]), 0.0)
    f = _dot(hid * hid, w2[...])
    x2 = x1 + mod_ref[5:6, :] * _rms(f, gains[3:4, :])
    _store_tok(o_ref, x2, x_tr, a, k)


def _merge(x, rw_parts, ygla, mg, mods, mod_row, gains, gn, hind, ws, *, x_tr, a, tm):
    bsz, t, d = x.shape
    k = tm // a
    kern = functools.partial(_merge_kernel, x_tr=x_tr, a=a, k=k)
    rw_tok = _tok_spec(RW_WIDTH, tm, False, a)
    out = pl.pallas_call(
        kern,
        grid=(bsz, t // tm),
        in_specs=[_tok_spec(d, tm, x_tr, a), rw_tok, rw_tok, rw_tok, rw_tok,
                  _tok_spec(GLA_VW, tm, False, a), _tok_spec(2 * d, tm, False, a),
                  pl.BlockSpec((None, 6, d), lambda b, j: (mod_row(b), 0, 0)),
                  _const_spec(gains.shape), _const_spec(gn.shape), _const_spec(hind.shape)]
        + [_const_spec(w.shape) for w in ws],
        out_specs=_tok_spec(d, tm, x_tr, a),
        out_shape=_tok_shape(bsz, t, d, x_tr, a, F32),
        compiler_params=_params(2),
        name="merge",
    )(_tok_view(x, x_tr, a), *rw_parts, ygla, mg, mods, gains, gn, hind, *ws)
    return _tok_unview(out, x_tr, t, d)


def _layer_params(l, p):
    w_in = p["w_in"][l]
    g0 = RW_COLS
    g1 = g0 + GLA_QKV_W
    g2 = g1 + 2 * GLA_GATE_RANK
    g3 = g2 + GLA_VW
    bf = lambda w: w.astype(BF16)
    row = lambda w: w.reshape(1, -1)
    lp = {
        "w_in_parts": [
            bf(jnp.concatenate([w_in[:, :g0], jnp.pad(
                w_in[:, g1:g2], ((0, 0), (0, LANES - 2 * GLA_GATE_RANK)))], axis=1)),
            bf(w_in[:, g0:g1]),
            bf(w_in[:, g2:g3]),
            bf(w_in[:, g3:]),
        ],
        "rw_mu": row(p["rw_mu"][l]),
        "rw_w0": p["rw_w0"][l],
        "rw_a0": p["rw_a0"][l],
        "rw_g_up": bf(p["rw_g_up"][l]),
        "rw_k_k": row(p["rw_k_k"][l]),
        "rw_k_a": row(p["rw_k_a"][l]),
        "rw_r_k": row(p["rw_r_k"][l]),
        "rw_gn": jnp.stack([p["rw_gn_w"][l], p["rw_gn_b"][l]]),
        "gla_conv": p["gla_conv"][l],
        "gla_alpha_bias": p["gla_alpha_bias"][l],
        "gla_norm_w": row(p["gla_norm_w"][l]),
        "merge_ws": [bf(p["rw_out"][l]), bf(p["gla_out"][l]), bf(p["merge_out"][l]),
                     bf(p["mlp_w1"][l]), bf(p["mlp_w2"][l])],
        "gains": jnp.stack([p["norm_mix_pre"][l], p["norm_mix_post"][l],
                            p["norm_ffn_pre"][l], p["norm_ffn_post"][l]]),
    }
    pad_dir = lambda w, r: jnp.stack([jnp.pad(w[di], ((di * r, LANES - (di + 1) * r), (0, 0)))
                                      for di in range(2)])
    lp["rw_w_up_pad"] = bf(pad_dir(p["rw_w_up"][l], p["rw_w_up"].shape[2]))
    lp["rw_a_up_pad"] = bf(pad_dir(p["rw_a_up"][l], p["rw_a_up"].shape[2]))
    lp["gla_alpha_up_pad"] = bf(pad_dir(p["gla_alpha_up"][l], GLA_GATE_RANK))
    if l > 0:
        lp["rw_vres_down"] = bf(p["rw_vres_down"][l - 1])
        lp["rw_vres_up"] = bf(p["rw_vres_up"][l - 1])
        lp["rw_vres_bias"] = row(p["rw_vres_bias"][l - 1])
    return lp


def _head_indicator():
    h = jnp.arange(RW_WIDTH) // RW_HEAD_DIM
    return (h[:, None] == h[None, :]).astype(BF16)


def _mixer(x, mods, mod_row, vfirst, s_rw, s_gla, lp, hind, *, p_col, need_out):
    bsz, t, d = x.shape
    tm_merge = min(MERGE_TILE, t)
    if p_col is None:
        x_tr, g_tr, a_p, a_q = False, False, SUBLANES, SUBLANES
    else:
        x_tr, g_tr = p_col, True
        rows = t // GRID_W
        a_p, a_q = (rows, GRID_W) if p_col else (GRID_W, rows)
    (gq, gald, gg), mg, (r, v, kk, g, gb, lw, kd, bd) = _inproj(
        x, vfirst, mods, mod_row, lp, hind, x_tr=x_tr, g_tr=g_tr, a=a_p, tm=min(PROJ_TILE, t))
    yf, yb, s_rw_out = _rwscan(r, v, kk, lw, kd, bd, s_rw)
    q, k, gv, la_b, of, s_gla_f = _glafwd(gq, gald, s_gla[0], lp)
    ygla, s_gla_b = _glabwd(q, k, gv, la_b, of, gg, s_gla[1], lp, y_tr=g_tr, a=a_q)
    s_gla_out = (s_gla_f, s_gla_b)
    if not need_out:
        return None, v, s_rw_out, s_gla_out
    x_new = _merge(x, (yf, yb, g, gb), ygla, mg, mods, mod_row, lp["gains"], lp["rw_gn"], hind,
                   lp["merge_ws"], x_tr=x_tr, a=a_p, tm=tm_merge)
    return x_new, v, s_rw_out, s_gla_out


def kernel(x, c, ctx, c_ctx, w_in, rw_mu, rw_w0, rw_w_up, rw_a0, rw_a_up, rw_g_up, rw_k_k, rw_k_a,
           rw_r_k, rw_gn_w, rw_gn_b, rw_vres_down, rw_vres_up, rw_vres_bias, rw_out, gla_conv,
           gla_alpha_up, gla_alpha_bias, gla_norm_w, gla_out, merge_out, mlp_w1, mlp_w2, ada_w,
           ada_b, norm_mix_pre, norm_mix_post, norm_ffn_pre, norm_ffn_post):
    p = dict(w_in=w_in, rw_mu=rw_mu, rw_w0=rw_w0, rw_w_up=rw_w_up, rw_a0=rw_a0, rw_a_up=rw_a_up,
             rw_g_up=rw_g_up, rw_k_k=rw_k_k, rw_k_a=rw_k_a, rw_r_k=rw_r_k.reshape(rw_r_k.shape[0], -1),
             rw_gn_w=rw_gn_w, rw_gn_b=rw_gn_b, rw_vres_down=rw_vres_down, rw_vres_up=rw_vres_up,
             rw_vres_bias=rw_vres_bias, rw_out=rw_out, gla_conv=gla_conv, gla_alpha_up=gla_alpha_up,
             gla_alpha_bias=gla_alpha_bias, gla_norm_w=gla_norm_w, gla_out=gla_out,
             merge_out=merge_out, mlp_w1=mlp_w1, mlp_w2=mlp_w2, norm_mix_pre=norm_mix_pre,
             norm_mix_post=norm_mix_post, norm_ffn_pre=norm_ffn_pre, norm_ffn_post=norm_ffn_post)
    bsz, t, d = x.shape
    t_ctx = ctx.shape[1]
    depth = w_in.shape[0]
    assert bsz < SUBLANES and d % LANES == 0
    assert t % (GRID_W * SUBLANES) == 0 and t_ctx % CHUNK == 0
    assert t % min(TOKEN_TILE, t) == 0 and t_ctx % min(TOKEN_TILE, t_ctx) == 0

    cc = jnp.concatenate([c, c_ctx[None, :], jnp.zeros((SUBLANES - 1 - bsz, d), F32)], axis=0)
    mods = _ada_mods(cc, ada_w, ada_b).reshape(depth, SUBLANES, 6, d)
    hind = _head_indicator()
    lat_row = lambda b: b
    ctx_row = lambda b: bsz

    x_lat, x_ctx = x, ctx
    vf_lat = vf_ctx = None
    for l in range(depth):
        last = l == depth - 1
        lp = _layer_params(l, p)
        z_rw = jnp.zeros((bsz,) + RW_STATE_SHAPE, F32)
        z_gla = (jnp.zeros((bsz,) + GLA_STATE_SHAPE, F32),) * 2
        x_ctx_new, v_ctx, s_rw, s_gla = _mixer(
            x_ctx, mods[l], ctx_row, vf_ctx, z_rw, z_gla, lp, hind, p_col=None, need_out=not last)
        x_lat, v_lat, _, _ = _mixer(
            x_lat, mods[l], lat_row, vf_lat, s_rw, s_gla, lp, hind,
            p_col=(l % 2 == 1), need_out=True)
        if l == 0:
            vf_lat, vf_ctx = v_lat, v_ctx
        if not last:
            x_ctx = x_ctx_new
    return x_lat
```

```python
import functools

import jax
import jax.numpy as jnp
from jax import lax
from jax.experimental import pallas as pl
from jax.experimental.pallas import tpu as pltpu

F32 = jnp.float32
BF16 = jnp.bfloat16

GRID_W = 64
RMS_EPS = 1e-6
RW_HEADS = 8
RW_HEAD_DIM = 64
RW_WIDTH = RW_HEADS * RW_HEAD_DIM
RW_GN_EPS = 64e-5
RW_COLS = 1920
GLA_HEADS = 4
GLA_DK = 64
GLA_DV = 128
GLA_KW = GLA_HEADS * GLA_DK
GLA_VW = GLA_HEADS * GLA_DV
GLA_QKV_W = 2 * GLA_KW + GLA_VW
GLA_GATE_RANK = 16
GLA_TAU = 16.0
CHUNK = 64
GLA_BLOCK_CHUNKS = 16
RW_SCAN_ROWS = 4
LANES = 128
SUBLANES = 8
HALO = 16
VMEM_LIMIT = 56 * 1024 * 1024
TOKEN_TILE = 1024
PROJ_TILE = 512
MERGE_TILE = 512
GLA_IN_DTYPE = F32


def _dot(a, b):
    return jnp.dot(a.astype(BF16), b.astype(BF16), preferred_element_type=F32)


def _dot_nt(a, b):
    return lax.dot_general(a.astype(BF16), b.astype(BF16), (((1,), (1,)), ((), ())),
                           preferred_element_type=F32)


def _dot_tn(a, b):
    return lax.dot_general(a.astype(BF16), b.astype(BF16), (((0,), (0,)), ((), ())),
                           preferred_element_type=F32)


def _split2(x):
    hi = x.astype(BF16)
    lo = (x - hi.astype(F32)).astype(BF16)
    return hi, lo


def _head_sums(x, ind):
    return jnp.dot(x.astype(BF16), ind, preferred_element_type=F32)


def _dot_ind_lhs(ind, x):
    hi, lo = _split2(x)
    return (jnp.dot(ind, hi, preferred_element_type=F32)
            + jnp.dot(ind, lo, preferred_element_type=F32))


def _sigmoid(x):
    return jax.nn.sigmoid(x)


def _silu(x):
    return x * jax.nn.sigmoid(x)


def _softplus(z):
    return jnp.maximum(z, 0.0) + jnp.log(1.0 + jnp.exp(-jnp.abs(z)))


def _log_sigmoid(z):
    return -_softplus(-z)


def _rms(x, gain):
    return x * lax.rsqrt(jnp.mean(x * x, axis=-1, keepdims=True) + RMS_EPS) * gain


def _load_tok(ref, transposed, a, k):
    if not transposed:
        return ref[...]
    return jnp.concatenate([ref[:, i, :] for i in range(k)], axis=0)


def _store_tok(ref, val, transposed, a, k):
    val = val.astype(ref.dtype)
    if not transposed:
        ref[...] = val
    else:
        for i in range(k):
            ref[:, i, :] = val[i * a:(i + 1) * a, :]


def _tok_view(arr, transposed, a):
    if not transposed:
        return arr
    b, t, f = arr.shape
    return arr.reshape(b, a, t // a, f)


def _tok_unview(arr, transposed, t, f):
    if not transposed:
        return arr
    return arr.reshape(arr.shape[0], t, f)


def _tok_spec(f, tm, transposed, a):
    if not transposed:
        return pl.BlockSpec((None, tm, f), lambda b, j: (b, j, 0))
    return pl.BlockSpec((None, a, tm // a, f), lambda b, j: (b, 0, j, 0))


def _tok_shape(bsz, t, f, transposed, a, dtype):
    if not transposed:
        return jax.ShapeDtypeStruct((bsz, t, f), dtype)
    return jax.ShapeDtypeStruct((bsz, a, t // a, f), dtype)


def _const_spec(shape):
    nd = len(shape)
    return pl.BlockSpec(shape, lambda *_: (0,) * nd, pipeline_mode=pl.Buffered(1))


def _params(ndim):
    return pltpu.CompilerParams(dimension_semantics=("arbitrary",) * ndim,
                                vmem_limit_bytes=VMEM_LIMIT)


def _halo_specs(f, tm, t):
    sub = tm // HALO
    last = t // HALO - 1
    prev = pl.BlockSpec((None, HALO, f), lambda b, j: (b, jnp.maximum(j * sub - 1, 0), 0))
    nxt = pl.BlockSpec((None, HALO, f), lambda b, j: (b, jnp.minimum((j + 1) * sub, last), 0))
    return prev, nxt


def _shifted(cur, prev_blk, next_blk, first, last):
    tm = cur.shape[0]
    keep_prev = jnp.where(first, 0.0, 1.0)
    keep_next = jnp.where(last, 0.0, 1.0)
    ext = jnp.concatenate([cur, next_blk.astype(F32) * keep_next,
                           prev_blk.astype(F32) * keep_prev], axis=0)
    n = ext.shape[0]
    return pltpu.roll(ext, 1, 0)[0:tm], pltpu.roll(ext, n - 1, 0)[0:tm]


def _ada_kernel(c_ref, w_ref, b_ref, o_ref):
    cc = c_ref[...]
    o_ref[...] = _dot(_silu(cc), w_ref[...]) + b_ref[...]


def _ada_mods(cc, ada_w, ada_b):
    nl, d, n6 = ada_w.shape
    tn = d
    return pl.pallas_call(
        _ada_kernel,
        grid=(nl, n6 // tn),
        in_specs=[pl.BlockSpec((SUBLANES, d), lambda l, n: (0, 0)),
                  pl.BlockSpec((None, d, tn), lambda l, n: (l, 0, n)),
                  pl.BlockSpec((None, 1, tn), lambda l, n: (l, 0, n))],
        out_specs=pl.BlockSpec((None, SUBLANES, tn), lambda l, n: (l, 0, n)),
        out_shape=jax.ShapeDtypeStruct((nl, SUBLANES, n6), F32),
        compiler_params=_params(2),
        name="ada",
    )(cc, ada_w, ada_b.reshape(nl, 1, n6))


def _inproj_kernel(*refs, has_vres, x_tr, g_tr, a, k):
    it = iter(refs)
    x_ref, xp_ref, xn_ref = next(it), next(it), next(it)
    vf_ref = next(it) if has_vres else None
    mod_ref, gain_ref, w_rw, w_q, w_gg, w_mg = (next(it) for _ in range(6))
    mu, w0, w_up, a0, a_up, g_up, k_k, k_a, r_k = (next(it) for _ in range(9))
    if has_vres:
        vdown, vup, vbias = next(it), next(it), next(it)
    hind = next(it)
    o_q, o_ald, o_gg, o_mg = (next(it) for _ in range(4))
    o_r, o_v, o_kk, o_g, o_gb, o_lw, o_kd, o_bd = (next(it) for _ in range(8))

    j = pl.program_id(1)
    keep_prev = jnp.where(j == 0, 0.0, 1.0)
    keep_next = jnp.where(j == pl.num_programs(1) - 1, 0.0, 1.0)
    x = _load_tok(x_ref, x_tr, a, k)
    tm = x.shape[0]
    xp = xp_ref[SUBLANES - 1] if x_tr else xp_ref[...]
    xn = xn_ref[0] if x_tr else xn_ref[...]
    modulate = lambda z: _rms(z, gain_ref[...]) * (1.0 + mod_ref[1:2, :]) + mod_ref[0:1, :]
    h = modulate(x)
    hb = h.astype(BF16)
    h_ext = jnp.concatenate([modulate(xp) * keep_prev, h, modulate(xn) * keep_next], axis=0)
    f_ext = jnp.dot(h_ext.astype(BF16), w_rw[...], preferred_element_type=F32)
    proj = lambda w: jnp.dot(hb, w[...], preferred_element_type=F32)
    _store_tok(o_mg, proj(w_mg), False, a, k)
    _store_tok(o_q, proj(w_q), g_tr, a, k)
    _store_tok(o_ald, f_ext[SUBLANES:SUBLANES + tm, RW_COLS:], g_tr, a, k)
    _store_tok(o_gg, proj(w_gg), g_tr, a, k)

    f = f_ext[SUBLANES:SUBLANES + tm, :RW_COLS]
    nbr = (f_ext[SUBLANES - 1:SUBLANES - 1 + tm, :RW_COLS]
           + f_ext[SUBLANES + 1:SUBLANES + 1 + tm, :RW_COLS])
    fs = f + mu[...] * (0.5 * nbr - f)
    w = RW_WIDTH
    r = fs[:, 0:w]
    kx = fs[:, w:2 * w]
    v = fs[:, 2 * w:3 * w]
    wd = fs[:, 3 * w:3 * w + LANES]
    ad = fs[:, 3 * w + LANES:3 * w + 2 * LANES]
    gd = fs[:, 3 * w + 2 * LANES:3 * w + 3 * LANES]
    if has_vres:
        vf = _load_tok(vf_ref, x_tr, a, k).astype(F32)
        mix = _sigmoid(vbias[...] + _dot(_dot(v, vdown[...]), vup[...]))
        v = v + (vf - v) * mix
    kk = kx * k_k[...]
    kk = kk * lax.rsqrt(_head_sums(kk * kk, hind[...]) + 1e-12)
    twd = jnp.tanh(wd)
    ksum = None
    for di in range(2):
        wlog = -_softplus(-(w0[di:di + 1, :] + _dot(twd, w_up[di]))) - 0.5
        o_lw[di] = -jnp.exp(wlog)
        lr = _sigmoid(a0[di:di + 1, :] + _dot(ad, a_up[di]))
        kd = kx * (1.0 + (lr - 1.0) * k_a[...])
        o_kd[di] = kd.astype(o_kd.dtype)
        o_bd[di] = (lr * kk).astype(o_bd.dtype)
        ksum = kd if ksum is None else ksum + kd
    g = _dot(_sigmoid(gd), g_up[...])
    bonus = _head_sums(r * ksum * r_k[...], hind[...]) * v
    o_r[...] = r.astype(o_r.dtype)
    o_v[...] = v.astype(o_v.dtype)
    o_kk[...] = kk.astype(o_kk.dtype)
    o_g[...] = g.astype(o_g.dtype)
    o_gb[...] = (bonus * g).astype(o_gb.dtype)


def _neighbour_specs(d, tm, t, x_tr, a):
    sub = tm // SUBLANES
    if not x_tr:
        last = t // SUBLANES - 1
        prev = pl.BlockSpec((None, SUBLANES, d), lambda b, j: (b, jnp.maximum(j * sub - 1, 0), 0))
        nxt = pl.BlockSpec((None, SUBLANES, d), lambda b, j: (b, jnp.minimum((j + 1) * sub, last), 0))
        return prev, nxt
    kb = tm // a // SUBLANES
    last = t // a // SUBLANES - 1
    shape = (None, SUBLANES, SUBLANES, d)
    prev = pl.BlockSpec(shape, lambda b, j: (b, a // SUBLANES - 1, jnp.maximum(j * kb - 1, 0), 0))
    nxt = pl.BlockSpec(shape, lambda b, j: (b, 0, jnp.minimum((j + 1) * kb, last), 0))
    return prev, nxt


def _inproj(x, vfirst, mods, mod_row, lp, hind, *, x_tr, g_tr, a, tm):
    bsz, t, d = x.shape
    w = RW_WIDTH
    k = tm // a
    has_vres = vfirst is not None
    kern = functools.partial(_inproj_kernel, has_vres=has_vres, x_tr=x_tr, g_tr=g_tr, a=a, k=k)
    xv = _tok_view(x, x_tr, a)
    prev_spec, next_spec = _neighbour_specs(d, tm, t, x_tr, a)
    args = [xv, xv, xv]
    in_specs = [_tok_spec(d, tm, x_tr, a), prev_spec, next_spec]
    if has_vres:
        args.append(_tok_view(vfirst, x_tr, a))
        in_specs.append(_tok_spec(w, tm, x_tr, a))
    args.append(mods)
    in_specs.append(pl.BlockSpec((None, 6, d), lambda b, j: (mod_row(b), 0, 0)))
    small = [lp["gains"][0:1]] + lp["w_in_parts"] + [
        lp["rw_mu"], lp["rw_w0"], lp["rw_w_up_pad"], lp["rw_a0"], lp["rw_a_up_pad"],
        lp["rw_g_up"], lp["rw_k_k"], lp["rw_k_a"], lp["rw_r_k"]]
    if has_vres:
        small += [lp["rw_vres_down"], lp["rw_vres_up"], lp["rw_vres_bias"]]
    small.append(hind)
    args += small
    in_specs += [_const_spec(s.shape) for s in small]
    gla_widths = (GLA_QKV_W, LANES, GLA_VW)
    tok = pl.BlockSpec((None, tm, w), lambda b, j: (b, j, 0))
    tok2 = pl.BlockSpec((2, None, tm, w), lambda b, j: (0, b, j, 0))
    s1 = jax.ShapeDtypeStruct((bsz, t, w), BF16)
    s2 = lambda dt: jax.ShapeDtypeStruct((2, bsz, t, w), dt)
    outs = pl.pallas_call(
        kern,
        grid=(bsz, t // tm),
        in_specs=in_specs,
        out_specs=[_tok_spec(f, tm, g_tr, a) for f in gla_widths]
        + [_tok_spec(2 * d, tm, False, a)] + [tok] * 5 + [tok2] * 3,
        out_shape=[_tok_shape(bsz, t, f, g_tr, a, GLA_IN_DTYPE) for f in gla_widths]
        + [_tok_shape(bsz, t, 2 * d, False, a, BF16)] + [s1] * 5 + [s2(F32), s2(BF16), s2(BF16)],
        compiler_params=_params(2),
        name="inproj",
    )(*args)
    gla = [_tok_unview(o, g_tr, t, f) for o, f in zip(outs[:3], gla_widths)]
    return gla, outs[3], outs[4:]


def _before(row, col, reverse, inclusive):
    if reverse:
        return (col >= row) if inclusive else (col > row)
    return (col <= row) if inclusive else (col < row)


def _cum_parts(lw, reverse):
    row = lax.broadcasted_iota(jnp.int32, (CHUNK, CHUNK), 0)
    col = lax.broadcasted_iota(jnp.int32, (CHUNK, CHUNK), 1)
    tri = jnp.where(_before(row, col, reverse, True), 1.0, 0.0).astype(BF16)
    cum = _dot_ind_lhs(tri, lw)
    tot = cum[0:1, :] if reverse else cum[CHUNK - 1:CHUNK, :]
    return cum, tot


def _rwscan_kernel(rf, vf, kkf, lwf, kdf, bdf, rb, vb, kkb, lwb, kdb, bdb, s0_ref,
                   yf_ref, yb_ref, sfin_ref, s_scr):
    c = pl.program_id(1)

    @pl.when(c == 0)
    def _():
        s_scr[...] = s0_ref[...]

    n = RW_HEAD_DIM
    pw = 2 * n
    lane = lax.broadcasted_iota(jnp.int32, (CHUNK, pw), 1)
    lo = lane < n
    row = lax.broadcasted_iota(jnp.int32, (CHUNK, pw), 0)
    eye2 = jnp.where(row == lane % n, 1.0, 0.0)
    grow = lax.broadcasted_iota(jnp.int32, (2 * CHUNK, 2 * pw), 0)
    gcol = lax.broadcasted_iota(jnp.int32, (2 * CHUNK, 2 * pw), 1)
    rp = grow % CHUNK
    cp = gcol % CHUNK
    bottom = grow // CHUNK

    def halves(z):
        zb = z.astype(BF16)
        zero = jnp.zeros_like(zb)
        return jnp.where(lo, zb, zero), jnp.where(lo, zero, zb)

    def bd(z):
        return jnp.concatenate(halves(z), axis=0)

    def pick(z):
        return jnp.where(lo, z[:n], z[n:])

    chains = []
    dirs = ((rf, vf, kkf, lwf, kdf, bdf, yf_ref), (rb, vb, kkb, lwb, kdb, bdb, yb_ref))
    for bi in range(s_scr.shape[0]):
        for di, (r_ref, v_ref, kk_ref, lw_ref, kd_ref, bd_ref, y_ref) in enumerate(dirs):
            reverse = di == 1
            lw = lw_ref[bi]
            cum, tot = _cum_parts(lw, reverse)
            e_neg = jnp.exp(-cum)
            e_end = jnp.exp(tot - cum)
            kd = kd_ref[bi].astype(F32)
            bdv = bd_ref[bi].astype(F32)
            ops = dict(
                rt=(r_ref[bi].astype(F32) * jnp.exp(cum)).astype(BF16),
                at=(kk_ref[bi].astype(F32) * jnp.exp(cum - lw)).astype(BF16),
                kt=(kd * e_neg).astype(BF16),
                bt=(bdv * e_neg).astype(BF16),
                kh=(kd * e_end).astype(BF16),
                bh=(bdv * e_end).astype(BF16),
                v=v_ref[bi],
                g_tot=jnp.exp(tot),
            )
            keep = (cp > rp - bottom) if reverse else (cp < rp + bottom)
            for hp in range(RW_HEADS // 2):
                sl = slice(hp * pw, (hp + 1) * pw)
                ch = {k: val[:, sl] for k, val in ops.items()}
                ch.update(bi=bi, di=di, hp=hp, sl=sl, keep=keep, y_ref=y_ref)
                chains.append(ch)

    for ch in chains:
        g = _dot_nt(jnp.concatenate([ch["at"], ch["rt"]], axis=0),
                    jnp.concatenate(halves(ch["bt"]) + halves(ch["kt"]), axis=0))
        g = jnp.where(ch["keep"], g, 0.0).astype(BF16)
        ch["l"] = g[:CHUNK, :pw]
        ch["ak"] = g[:CHUNK, pw:]
        ch["rbk"] = g[CHUNK:, :]
    for ch in chains:
        ch["p"] = _dot(ch["l"], bd(ch["l"]))
        ch["av"] = _dot(ch["ak"], bd(ch["v"]))
        ch["kv"] = pick(_dot_tn(ch["v"], ch["kh"]))
        ch["x"] = eye2 - ch["l"].astype(F32)
    for _ in range(4):
        for ch in chains:
            z = _dot(jnp.concatenate([ch["x"], ch["p"]], axis=0), bd(ch["p"]))
            ch["x"] = ch["x"] + z[:CHUNK]
            ch["p"] = z[CHUNK:]
    for ch in chains:
        ch["x"] = ch["x"] + _dot(ch["x"], bd(ch["p"]))
    for ch in chains:
        wu = _dot(ch["x"], jnp.concatenate([bd(ch["at"]), bd(ch["av"])], axis=1))
        ch["wm"] = wu[:, :pw].astype(BF16)
        ch["u0"] = -wu[:, pw:]
    for ch in chains:
        t = ch["u0"].T
        ch["u0t"] = jnp.concatenate([t[:n], t[n:]], axis=1)
    for ch in chains:
        s = s_scr[ch["bi"], ch["di"], ch["hp"]]
        ch["s"] = s
        pr = _dot_nt(jnp.concatenate([ch["wm"], ch["rt"]], axis=0), bd(s))
        ch["u"] = ch["u0"] - pr[:CHUNK]
        ch["rs"] = pr[CHUNK:]
        ch["ut"] = ch["u0t"] - _dot_nt(s, bd(ch["wm"]))
    for ch in chains:
        y = ch["rs"] + _dot(ch["rbk"], jnp.concatenate([bd(ch["u"]), bd(ch["v"])], axis=0))
        ch["y_ref"][ch["bi"], :, ch["sl"]] = y.astype(ch["y_ref"].dtype)
        s_scr[ch["bi"], ch["di"], ch["hp"]] = (ch["s"] * ch["g_tot"] + ch["kv"]
                                               + _dot(ch["ut"], bd(ch["bh"])))

    @pl.when(c == pl.num_programs(1) - 1)
    def _():
        sfin_ref[...] = s_scr[...]


RW_STATE_SHAPE = (2, RW_HEADS // 2, RW_HEAD_DIM, 2 * RW_HEAD_DIM)


def _rwscan(r, v, kk, lw, kd, bd, s0):
    bsz, t, w = r.shape
    nc = t // CHUNK
    nr = RW_SCAN_ROWS if bsz % RW_SCAN_ROWS == 0 else 1
    tok_f = pl.BlockSpec((nr, CHUNK, w), lambda b, c: (b, c, 0))
    tok_b = pl.BlockSpec((nr, CHUNK, w), lambda b, c: (b, nc - 1 - c, 0))
    dir_f = pl.BlockSpec((None, nr, CHUNK, w), lambda b, c: (0, b, c, 0))
    dir_b = pl.BlockSpec((None, nr, CHUNK, w), lambda b, c: (1, b, nc - 1 - c, 0))
    st = pl.BlockSpec((nr,) + RW_STATE_SHAPE, lambda b, c: (b, 0, 0, 0, 0))
    y_shape = jax.ShapeDtypeStruct((bsz, t, w), BF16)
    yf, yb, sfin = pl.pallas_call(
        _rwscan_kernel,
        grid=(bsz // nr, nc),
        in_specs=[tok_f, tok_f, tok_f, dir_f, dir_f, dir_f,
                  tok_b, tok_b, tok_b, dir_b, dir_b, dir_b, st],
        out_specs=[tok_f, tok_b, st],
        out_shape=[y_shape, y_shape, jax.ShapeDtypeStruct(s0.shape, F32)],
        scratch_shapes=[pltpu.VMEM((nr,) + RW_STATE_SHAPE, F32)],
        compiler_params=_params(2),
        name="rwscan",
    )(r, v, kk, lw, kd, bd, r, v, kk, lw, kd, bd, s0)
    return yf, yb, sfin


def _rw_output(yf, yb, g, gb, gn, hind):
    y = yf.astype(F32) + yb.astype(F32)
    inv_n = 1.0 / RW_HEAD_DIM
    mean = _head_sums(y, hind) * inv_n
    yc = y - mean
    var = _head_sums(yc * yc, hind) * inv_n
    yn = yc * lax.rsqrt(var + RW_GN_EPS) * gn[0:1, :] + gn[1:2, :]
    return yn * g.astype(F32) + gb.astype(F32)


GLA_STATE_SHAPE = (GLA_HEADS, GLA_DV, GLA_DK)


def _gla_block_scan(q, k, v, la, s_scr, nb, reverse, emit):
    row = lax.broadcasted_iota(jnp.int32, (CHUNK, CHUNK), 0)
    col = lax.broadcasted_iota(jnp.int32, (CHUNK, CHUNK), 1)
    keep = _before(row, col, reverse, True)
    chains = []
    for ci in (reversed(range(nb)) if reverse else range(nb)):
        rows = slice(ci * CHUNK, (ci + 1) * CHUNK)
        cum, tot = _cum_parts(la[rows, :], reverse)
        kc = k[rows, :].astype(F32)
        q_dec = (q[rows, :].astype(F32) * jnp.exp(cum)).astype(BF16)
        k_inv = (kc * jnp.exp(-cum)).astype(BF16)
        k_end = (kc * jnp.exp(tot - cum)).astype(BF16)
        dec = jnp.exp(tot)
        vc = v[rows, :]
        for h in range(GLA_HEADS):
            sk = slice(h * GLA_DK, (h + 1) * GLA_DK)
            sv = slice(h * GLA_DV, (h + 1) * GLA_DV)
            chains.append(dict(h=h, rows=rows, sv=sv, q=q_dec[:, sk], ki=k_inv[:, sk],
                               ke=k_end[:, sk], dec=dec[:, sk], v=vc[:, sv]))
    for ch in chains:
        ch["sc"] = jnp.where(keep, _dot_nt(ch["q"], ch["ki"]), 0.0)
        ch["kv"] = _dot_tn(ch["v"], ch["ke"])
    state = {}
    for ch in chains:
        s = state[ch["h"]] if ch["h"] in state else s_scr[ch["h"]]
        ch["s"] = s
        state[ch["h"]] = s * ch["dec"] + ch["kv"]
    for h, s in state.items():
        s_scr[h] = s
    for ch in chains:
        emit(ch["rows"], ch["sv"], _dot(ch["sc"], ch["v"]) + _dot_nt(ch["q"], ch["s"]))


def _glafwd_kernel(q_cur, q_prev, q_next, ald_ref, conv, up, bias, s0_ref,
                   o_q, o_k, o_v, o_la, o_of, sfin_ref, s_scr, *, nb):
    j = pl.program_id(1)

    @pl.when(j == 0)
    def _():
        s_scr[...] = s0_ref[...]

    x = q_cur[...].astype(F32)
    prev, nxt = _shifted(x, q_prev[...], q_next[...], j == 0, j == pl.num_programs(1) - 1)
    y = _silu(conv[0:1, :] * prev + conv[1:2, :] * x + conv[2:3, :] * nxt)
    q = (y[:, 0:GLA_KW] * (GLA_DK ** -0.5)).astype(BF16)
    k = y[:, GLA_KW:2 * GLA_KW].astype(BF16)
    v = y[:, 2 * GLA_KW:].astype(BF16)
    o_q[...] = q
    o_k[...] = k
    o_v[...] = v
    ald = ald_ref[...]
    log_alpha = lambda di: _log_sigmoid(_dot(ald, up[di]) + bias[di:di + 1, :]) / GLA_TAU
    o_la[...] = log_alpha(1)

    def emit(rows, lanes, o):
        o_of[rows, lanes] = o.astype(o_of.dtype)

    _gla_block_scan(q, k, v, log_alpha(0), s_scr, nb, False, emit)

    @pl.when(j == pl.num_programs(1) - 1)
    def _():
        sfin_ref[...] = s_scr[...]


def _glafwd(qkv, ald, s0, lp):
    bsz, t, fq = qkv.shape
    nb = min(GLA_BLOCK_CHUNKS, t // CHUNK)
    tb = nb * CHUNK
    assert t % tb == 0
    prev_spec, next_spec = _halo_specs(fq, tb, t)
    tok = lambda f: pl.BlockSpec((None, tb, f), lambda b, j: (b, j, 0))
    st = pl.BlockSpec((None,) + GLA_STATE_SHAPE, lambda b, j: (b, 0, 0, 0))
    small = [lp["gla_conv"], lp["gla_alpha_up_pad"], lp["gla_alpha_bias"]]
    act = lambda f, dt: jax.ShapeDtypeStruct((bsz, t, f), dt)
    return pl.pallas_call(
        functools.partial(_glafwd_kernel, nb=nb),
        grid=(bsz, t // tb),
        in_specs=[tok(fq), prev_spec, next_spec, tok(LANES)]
        + [_const_spec(s.shape) for s in small] + [st],
        out_specs=[tok(GLA_KW), tok(GLA_KW), tok(GLA_VW), tok(GLA_KW), tok(GLA_VW), st],
        out_shape=[act(GLA_KW, BF16), act(GLA_KW, BF16), act(GLA_VW, BF16), act(GLA_KW, F32),
                   act(GLA_VW, BF16), jax.ShapeDtypeStruct(s0.shape, F32)],
        scratch_shapes=[pltpu.VMEM(GLA_STATE_SHAPE, F32)],
        compiler_params=_params(2),
        name="glafwd",
    )(qkv, qkv, qkv, ald, *small, s0)


def _glabwd_kernel(q_ref, k_ref, v_ref, la_ref, of_ref, gate_ref, nw, s0_ref,
                   y_ref, sfin_ref, s_scr, o_scr, *, nb, y_tr, a, k):
    j = pl.program_id(1)

    @pl.when(j == 0)
    def _():
        s_scr[...] = s0_ref[...]

    def emit(rows, lanes, o):
        o_scr[rows, lanes] = o

    _gla_block_scan(q_ref, k_ref, v_ref, la_ref, s_scr, nb, True, emit)
    o = of_ref[...].astype(F32) + o_scr[...]
    gate = gate_ref[...].astype(F32)
    ys = []
    for h in range(GLA_HEADS):
        sv = slice(h * GLA_DV, (h + 1) * GLA_DV)
        ys.append(_rms(o[:, sv], nw[...]) * _silu(gate[:, sv]))
    _store_tok(y_ref, jnp.concatenate(ys, axis=1), y_tr, a, k)

    @pl.when(j == pl.num_programs(1) - 1)
    def _():
        sfin_ref[...] = s_scr[...]


def _glabwd(q, k, v, la, of, gate, s0, lp, *, y_tr, a):
    bsz, t, _ = q.shape
    w = GLA_VW
    tb = min(TOKEN_TILE, t)
    nb = tb // CHUNK
    nblk = t // tb
    kk = tb // a
    tok = lambda f: pl.BlockSpec((None, tb, f), lambda b, j: (b, nblk - 1 - j, 0))
    st = pl.BlockSpec((None,) + GLA_STATE_SHAPE, lambda b, j: (b, 0, 0, 0))
    if y_tr:
        y_spec = pl.BlockSpec((None, a, kk, w), lambda b, j: (b, 0, nblk - 1 - j, 0))
    else:
        y_spec = tok(w)
    y, sfin = pl.pallas_call(
        functools.partial(_glabwd_kernel, nb=nb, y_tr=y_tr, a=a, k=kk),
        grid=(bsz, nblk),
        in_specs=[tok(GLA_KW), tok(GLA_KW), tok(w), tok(GLA_KW), tok(w), tok(w),
                  _const_spec((1, GLA_DV)), st],
        out_specs=[y_spec, st],
        out_shape=[_tok_shape(bsz, t, w, y_tr, a, BF16), jax.ShapeDtypeStruct(s0.shape, F32)],
        scratch_shapes=[pltpu.VMEM(GLA_STATE_SHAPE, F32), pltpu.VMEM((tb, w), F32)],
        compiler_params=_params(2),
        name="glabwd",
    )(q, k, v, la, of, gate, lp["gla_norm_w"], s0)
    return _tok_unview(y, y_tr, t, w), sfin


def _merge_kernel(x_ref, yf_ref, yb_ref, g_ref, gb_ref, ygla_ref, mg_ref, mod_ref, gains, gn, hind,
                  w_rwo, w_glao, w_mo, w1, w2, o_ref, *, x_tr, a, k):
    d = x_ref.shape[-1]
    x = _load_tok(x_ref, x_tr, a, k)
    mg = mg_ref[...].astype(F32)
    yrw = _rw_output(yf_ref[...], yb_ref[...], g_ref[...], gb_ref[...], gn, hind[...])
    br = (_sigmoid(mg[:, :d]) * _dot(yrw, w_rwo[...])
          + _sigmoid(mg[:, d:]) * _dot(ygla_ref[...], w_glao[...]))
    m = _dot(br, w_mo[...])
    x1 = x + mod_ref[2:3, :] * _rms(m, gains[1:2, :])
    h2 = _rms(x1, gains[2:3, :]) * (1.0 + mod_ref[4:5, :]) + mod_ref[3:4, :]
    hid = jnp.maximum(_dot(h2, w1[...]), 0.0)
    f = _dot(hid * hid, w2[...])
    x2 = x1 + mod_ref[5:6, :] * _rms(f, gains[3:4, :])
    _store_tok(o_ref, x2, x_tr, a, k)


def _merge(x, rw_parts, ygla, mg, mods, mod_row, gains, gn, hind, ws, *, x_tr, a, tm):
    bsz, t, d = x.shape
    k = tm // a
    kern = functools.partial(_merge_kernel, x_tr=x_tr, a=a, k=k)
    rw_tok = _tok_spec(RW_WIDTH, tm, False, a)
    out = pl.pallas_call(
        kern,
        grid=(bsz, t // tm),
        in_specs=[_tok_spec(d, tm, x_tr, a), rw_tok, rw_tok, rw_tok, rw_tok,
                  _tok_spec(GLA_VW, tm, False, a), _tok_spec(2 * d, tm, False, a),
                  pl.BlockSpec((None, 6, d), lambda b, j: (mod_row(b), 0, 0)),
                  _const_spec(gains.shape), _const_spec(gn.shape), _const_spec(hind.shape)]
        + [_const_spec(w.shape) for w in ws],
        out_specs=_tok_spec(d, tm, x_tr, a),
        out_shape=_tok_shape(bsz, t, d, x_tr, a, F32),
        compiler_params=_params(2),
        name="merge",
    )(_tok_view(x, x_tr, a), *rw_parts, ygla, mg, mods, gains, gn, hind, *ws)
    return _tok_unview(out, x_tr, t, d)


def _layer_params(l, p):
    w_in = p["w_in"][l]
    g0 = RW_COLS
    g1 = g0 + GLA_QKV_W
    g2 = g1 + 2 * GLA_GATE_RANK
    g3 = g2 + GLA_VW
    bf = lambda w: w.astype(BF16)
    row = lambda w: w.reshape(1, -1)
    lp = {
        "w_in_parts": [
            bf(jnp.concatenate([w_in[:, :g0], jnp.pad(
                w_in[:, g1:g2], ((0, 0), (0, LANES - 2 * GLA_GATE_RANK)))], axis=1)),
            bf(w_in[:, g0:g1]),
            bf(w_in[:, g2:g3]),
            bf(w_in[:, g3:]),
        ],
        "rw_mu": row(p["rw_mu"][l]),
        "rw_w0": p["rw_w0"][l],
        "rw_a0": p["rw_a0"][l],
        "rw_g_up": bf(p["rw_g_up"][l]),
        "rw_k_k": row(p["rw_k_k"][l]),
        "rw_k_a": row(p["rw_k_a"][l]),
        "rw_r_k": row(p["rw_r_k"][l]),
        "rw_gn": jnp.stack([p["rw_gn_w"][l], p["rw_gn_b"][l]]),
        "gla_conv": p["gla_conv"][l],
        "gla_alpha_bias": p["gla_alpha_bias"][l],
        "gla_norm_w": row(p["gla_norm_w"][l]),
        "merge_ws": [bf(p["rw_out"][l]), bf(p["gla_out"][l]), bf(p["merge_out"][l]),
                     bf(p["mlp_w1"][l]), bf(p["mlp_w2"][l])],
        "gains": jnp.stack([p["norm_mix_pre"][l], p["norm_mix_post"][l],
                            p["norm_ffn_pre"][l], p["norm_ffn_post"][l]]),
    }
    pad_dir = lambda w, r: jnp.stack([jnp.pad(w[di], ((di * r, LANES - (di + 1) * r), (0, 0)))
                                      for di in range(2)])
    lp["rw_w_up_pad"] = bf(pad_dir(p["rw_w_up"][l], p["rw_w_up"].shape[2]))
    lp["rw_a_up_pad"] = bf(pad_dir(p["rw_a_up"][l], p["rw_a_up"].shape[2]))
    lp["gla_alpha_up_pad"] = bf(pad_dir(p["gla_alpha_up"][l], GLA_GATE_RANK))
    if l > 0:
        lp["rw_vres_down"] = bf(p["rw_vres_down"][l - 1])
        lp["rw_vres_up"] = bf(p["rw_vres_up"][l - 1])
        lp["rw_vres_bias"] = row(p["rw_vres_bias"][l - 1])
    return lp


def _head_indicator():
    h = jnp.arange(RW_WIDTH) // RW_HEAD_DIM
    return (h[:, None] == h[None, :]).astype(BF16)


def _mixer(x, mods, mod_row, vfirst, s_rw, s_gla, lp, hind, *, p_col, need_out):
    bsz, t, d = x.shape
    tm_merge = min(MERGE_TILE, t)
    if p_col is None:
        x_tr, g_tr, a_p, a_q = False, False, SUBLANES, SUBLANES
    else:
        x_tr, g_tr = p_col, True
        rows = t // GRID_W
        a_p, a_q = (rows, GRID_W) if p_col else (GRID_W, rows)
    (gq, gald, gg), mg, (r, v, kk, g, gb, lw, kd, bd) = _inproj(
        x, vfirst, mods, mod_row, lp, hind, x_tr=x_tr, g_tr=g_tr, a=a_p, tm=min(PROJ_TILE, t))
    yf, yb, s_rw_out = _rwscan(r, v, kk, lw, kd, bd, s_rw)
    q, k, gv, la_b, of, s_gla_f = _glafwd(gq, gald, s_gla[0], lp)
    ygla, s_gla_b = _glabwd(q, k, gv, la_b, of, gg, s_gla[1], lp, y_tr=g_tr, a=a_q)
    s_gla_out = (s_gla_f, s_gla_b)
    if not need_out:
        return None, v, s_rw_out, s_gla_out
    x_new = _merge(x, (yf, yb, g, gb), ygla, mg, mods, mod_row, lp["gains"], lp["rw_gn"], hind,
                   lp["merge_ws"], x_tr=x_tr, a=a_p, tm=tm_merge)
    return x_new, v, s_rw_out, s_gla_out


def kernel(x, c, ctx, c_ctx, w_in, rw_mu, rw_w0, rw_w_up, rw_a0, rw_a_up, rw_g_up, rw_k_k, rw_k_a,
           rw_r_k, rw_gn_w, rw_gn_b, rw_vres_down, rw_vres_up, rw_vres_bias, rw_out, gla_conv,
           gla_alpha_up, gla_alpha_bias, gla_norm_w, gla_out, merge_out, mlp_w1, mlp_w2, ada_w,
           ada_b, norm_mix_pre, norm_mix_post, norm_ffn_pre, norm_ffn_post):
    p = dict(w_in=w_in, rw_mu=rw_mu, rw_w0=rw_w0, rw_w_up=rw_w_up, rw_a0=rw_a0, rw_a_up=rw_a_up,
             rw_g_up=rw_g_up, rw_k_k=rw_k_k, rw_k_a=rw_k_a, rw_r_k=rw_r_k.reshape(rw_r_k.shape[0], -1),
             rw_gn_w=rw_gn_w, rw_gn_b=rw_gn_b, rw_vres_down=rw_vres_down, rw_vres_up=rw_vres_up,
             rw_vres_bias=rw_vres_bias, rw_out=rw_out, gla_conv=gla_conv, gla_alpha_up=gla_alpha_up,
             gla_alpha_bias=gla_alpha_bias, gla_norm_w=gla_norm_w, gla_out=gla_out,
             merge_out=merge_out, mlp_w1=mlp_w1, mlp_w2=mlp_w2, norm_mix_pre=norm_mix_pre,
             norm_mix_post=norm_mix_post, norm_ffn_pre=norm_ffn_pre, norm_ffn_post=norm_ffn_post)
    bsz, t, d = x.shape
    t_ctx = ctx.shape[1]
    depth = w_in.shape[0]
    assert bsz < SUBLANES and d % LANES == 0
    assert t % (GRID_W * SUBLANES) == 0 and t_ctx % CHUNK == 0
    assert t % min(TOKEN_TILE, t) == 0 and t_ctx % min(TOKEN_TILE, t_ctx) == 0

    cc = jnp.concatenate([c, c_ctx[None, :], jnp.zeros((SUBLANES - 1 - bsz, d), F32)], axis=0)
    mods = _ada_mods(cc, ada_w, ada_b).reshape(depth, SUBLANES, 6, d)
    hind = _head_indicator()
    lat_row = lambda b: b
    ctx_row = lambda b: bsz

    x_lat, x_ctx = x, ctx
    vf_lat = vf_ctx = None
    for l in range(depth):
        last = l == depth - 1
        lp = _layer_params(l, p)
        z_rw = jnp.zeros((bsz,) + RW_STATE_SHAPE, F32)
        z_gla = (jnp.zeros((bsz,) + GLA_STATE_SHAPE, F32),) * 2
        x_ctx_new, v_ctx, s_rw, s_gla = _mixer(
            x_ctx, mods[l], ctx_row, vf_ctx, z_rw, z_gla, lp, hind, p_col=None, need_out=not last)
        x_lat, v_lat, _, _ = _mixer(
            x_lat, mods[l], lat_row, vf_lat, s_rw, s_gla, lp, hind,
            p_col=(l % 2 == 1), need_out=True)
        if l == 0:
            vf_lat, vf_ctx = v_lat, v_ctx
        if not last:
            x_ctx = x_ctx_new
    return x_lat
```

```python
import functools
from typing import NamedTuple

import jax
import jax.numpy as jnp
from jax import lax
from jax.experimental import pallas as pl
from jax.experimental.pallas import tpu as pltpu

F32 = jnp.float32
BF16 = jnp.bfloat16

GRID_W = 64
RMS_EPS = 1e-6
RW_HEADS = 8
RW_HEAD_DIM = 64
RW_WIDTH = RW_HEADS * RW_HEAD_DIM
RW_GN_EPS = 64e-5
RW_COLS = 1920
GLA_HEADS = 4
GLA_DK = 64
GLA_DV = 128
GLA_KW = GLA_HEADS * GLA_DK
GLA_VW = GLA_HEADS * GLA_DV
GLA_QKV_W = 2 * GLA_KW + GLA_VW
GLA_GATE_RANK = 16
GLA_TAU = 16.0
CHUNK = 64
GLA_BLOCK_CHUNKS = 16
RW_SCAN_ROWS = 4
LANES = 128
SUBLANES = 8
HALO = 16
VMEM_LIMIT = 56 * 1024 * 1024
TOKEN_TILE = 1024
PROJ_TILE = 512
MERGE_TILE = 512
GLA_IN_DTYPE = F32


def _dot(a, b):
    return jnp.dot(a.astype(BF16), b.astype(BF16), preferred_element_type=F32)


def _dot_nt(a, b):
    return lax.dot_general(a.astype(BF16), b.astype(BF16), (((1,), (1,)), ((), ())),
                           preferred_element_type=F32)


def _dot_tn(a, b):
    return lax.dot_general(a.astype(BF16), b.astype(BF16), (((0,), (0,)), ((), ())),
                           preferred_element_type=F32)


def _split2(x):
    hi = x.astype(BF16)
    lo = (x - hi.astype(F32)).astype(BF16)
    return hi, lo


def _head_sums(x, ind):
    return jnp.dot(x.astype(BF16), ind, preferred_element_type=F32)


def _dot_ind_lhs(ind, x):
    hi, lo = _split2(x)
    return (jnp.dot(ind, hi, preferred_element_type=F32)
            + jnp.dot(ind, lo, preferred_element_type=F32))


def _sigmoid(x):
    return jax.nn.sigmoid(x)


def _silu(x):
    return x * jax.nn.sigmoid(x)


def _softplus(z):
    return jnp.maximum(z, 0.0) + jnp.log(1.0 + jnp.exp(-jnp.abs(z)))


def _log_sigmoid(z):
    return -_softplus(-z)


def _rms(x, gain):
    return x * lax.rsqrt(jnp.mean(x * x, axis=-1, keepdims=True) + RMS_EPS) * gain


def _load_tok(ref, transposed, a, k):
    if not transposed:
        return ref[...]
    return jnp.concatenate([ref[:, i, :] for i in range(k)], axis=0)


def _store_tok(ref, val, transposed, a, k):
    val = val.astype(ref.dtype)
    if not transposed:
        ref[...] = val
    else:
        for i in range(k):
            ref[:, i, :] = val[i * a:(i + 1) * a, :]


def _tok_view(arr, transposed, a):
    if not transposed:
        return arr
    b, t, f = arr.shape
    return arr.reshape(b, a, t // a, f)


def _tok_unview(arr, transposed, t, f):
    if not transposed:
        return arr
    return arr.reshape(arr.shape[0], t, f)


def _tok_spec(f, tm, transposed, a):
    if not transposed:
        return pl.BlockSpec((None, tm, f), lambda b, j: (b, j, 0))
    return pl.BlockSpec((None, a, tm // a, f), lambda b, j: (b, 0, j, 0))


def _tok_shape(bsz, t, f, transposed, a, dtype):
    if not transposed:
        return jax.ShapeDtypeStruct((bsz, t, f), dtype)
    return jax.ShapeDtypeStruct((bsz, a, t // a, f), dtype)


def _const_spec(shape):
    nd = len(shape)
    return pl.BlockSpec(shape, lambda *_: (0,) * nd, pipeline_mode=pl.Buffered(1))


class _Stacked(NamedTuple):
    arr: jax.Array
    layer: int


def _warg(w):
    return w.arr if isinstance(w, _Stacked) else w


def _wspec(w):
    if not isinstance(w, _Stacked):
        return _const_spec(w.shape)
    shape = w.arr.shape[1:]
    index = (w.layer,) + (0,) * len(shape)
    return pl.BlockSpec((None,) + shape, lambda *_: index, pipeline_mode=pl.Buffered(1))


def _params(ndim):
    return pltpu.CompilerParams(dimension_semantics=("arbitrary",) * ndim,
                                vmem_limit_bytes=VMEM_LIMIT)


def _halo_specs(f, tm, t):
    sub = tm // HALO
    last = t // HALO - 1
    prev = pl.BlockSpec((None, HALO, f), lambda b, j: (b, jnp.maximum(j * sub - 1, 0), 0))
    nxt = pl.BlockSpec((None, HALO, f), lambda b, j: (b, jnp.minimum((j + 1) * sub, last), 0))
    return prev, nxt


def _shifted(cur, prev_blk, next_blk, first, last):
    tm = cur.shape[0]
    keep_prev = jnp.where(first, 0.0, 1.0)
    keep_next = jnp.where(last, 0.0, 1.0)
    ext = jnp.concatenate([cur, next_blk.astype(F32) * keep_next,
                           prev_blk.astype(F32) * keep_prev], axis=0)
    n = ext.shape[0]
    return pltpu.roll(ext, 1, 0)[0:tm], pltpu.roll(ext, n - 1, 0)[0:tm]


def _ada_kernel(c_ref, w_ref, b_ref, o_ref):
    cc = c_ref[...]
    o_ref[...] = _dot(_silu(cc), w_ref[...]) + b_ref[...]


def _ada_mods(cc, ada_w, ada_b):
    nl, d, n6 = ada_w.shape
    tn = d
    return pl.pallas_call(
        _ada_kernel,
        grid=(nl, n6 // tn),
        in_specs=[pl.BlockSpec((SUBLANES, d), lambda l, n: (0, 0)),
                  pl.BlockSpec((None, d, tn), lambda l, n: (l, 0, n)),
                  pl.BlockSpec((None, 1, tn), lambda l, n: (l, 0, n))],
        out_specs=pl.BlockSpec((None, SUBLANES, tn), lambda l, n: (l, 0, n)),
        out_shape=jax.ShapeDtypeStruct((nl, SUBLANES, n6), F32),
        compiler_params=_params(2),
        name="ada",
    )(cc, ada_w, ada_b.reshape(nl, 1, n6))


def _inproj_kernel(*refs, has_vres, x_tr, g_tr, a, k):
    it = iter(refs)
    x_ref, xp_ref, xn_ref = next(it), next(it), next(it)
    vf_ref = next(it) if has_vres else None
    mod_ref, gain_ref, w_rw, w_q, w_gg, w_mg = (next(it) for _ in range(6))
    mu, w0, w_up, a0, a_up, g_up, k_k, k_a, r_k = (next(it) for _ in range(9))
    if has_vres:
        vdown, vup, vbias = next(it), next(it), next(it)
    hind = next(it)
    o_q, o_ald, o_gg, o_mg = (next(it) for _ in range(4))
    o_r, o_v, o_kk, o_g, o_gb, o_lw, o_kd, o_bd = (next(it) for _ in range(8))

    j = pl.program_id(1)
    keep_prev = jnp.where(j == 0, 0.0, 1.0)
    keep_next = jnp.where(j == pl.num_programs(1) - 1, 0.0, 1.0)
    x = _load_tok(x_ref, x_tr, a, k)
    tm = x.shape[0]
    xp = xp_ref[SUBLANES - 1] if x_tr else xp_ref[...]
    xn = xn_ref[0] if x_tr else xn_ref[...]
    modulate = lambda z: _rms(z, gain_ref[...]) * (1.0 + mod_ref[1:2, :]) + mod_ref[0:1, :]
    h = modulate(x)
    hb = h.astype(BF16)
    h_ext = jnp.concatenate([modulate(xp) * keep_prev, h, modulate(xn) * keep_next], axis=0)
    f_ext = jnp.dot(h_ext.astype(BF16), w_rw[...], preferred_element_type=F32)
    proj = lambda w: jnp.dot(hb, w[...], preferred_element_type=F32)
    _store_tok(o_mg, proj(w_mg), False, a, k)
    _store_tok(o_q, proj(w_q), g_tr, a, k)
    _store_tok(o_ald, f_ext[SUBLANES:SUBLANES + tm, RW_COLS:], g_tr, a, k)
    _store_tok(o_gg, proj(w_gg), g_tr, a, k)

    f = f_ext[SUBLANES:SUBLANES + tm, :RW_COLS]
    nbr = (f_ext[SUBLANES - 1:SUBLANES - 1 + tm, :RW_COLS]
           + f_ext[SUBLANES + 1:SUBLANES + 1 + tm, :RW_COLS])
    fs = f + mu[...] * (0.5 * nbr - f)
    w = RW_WIDTH
    r = fs[:, 0:w]
    kx = fs[:, w:2 * w]
    v = fs[:, 2 * w:3 * w]
    wd = fs[:, 3 * w:3 * w + LANES]
    ad = fs[:, 3 * w + LANES:3 * w + 2 * LANES]
    gd = fs[:, 3 * w + 2 * LANES:3 * w + 3 * LANES]
    if has_vres:
        vf = _load_tok(vf_ref, x_tr, a, k).astype(F32)
        mix = _sigmoid(vbias[...] + _dot(_dot(v, vdown[...]), vup[...]))
        v = v + (vf - v) * mix
    kk = kx * k_k[...]
    kk = kk * lax.rsqrt(_head_sums(kk * kk, hind[...]) + 1e-12)
    twd = jnp.tanh(wd)
    ksum = None
    for di in range(2):
        wlog = -_softplus(-(w0[di:di + 1, :] + _dot(twd, w_up[di]))) - 0.5
        o_lw[di] = -jnp.exp(wlog)
        lr = _sigmoid(a0[di:di + 1, :] + _dot(ad, a_up[di]))
        kd = kx * (1.0 + (lr - 1.0) * k_a[...])
        o_kd[di] = kd.astype(o_kd.dtype)
        o_bd[di] = (lr * kk).astype(o_bd.dtype)
        ksum = kd if ksum is None else ksum + kd
    g = _dot(_sigmoid(gd), g_up[...])
    bonus = _head_sums(r * ksum * r_k[...], hind[...]) * v
    o_r[...] = r.astype(o_r.dtype)
    o_v[...] = v.astype(o_v.dtype)
    o_kk[...] = kk.astype(o_kk.dtype)
    o_g[...] = g.astype(o_g.dtype)
    o_gb[...] = (bonus * g).astype(o_gb.dtype)


def _neighbour_specs(d, tm, t, x_tr, a):
    sub = tm // SUBLANES
    if not x_tr:
        last = t // SUBLANES - 1
        prev = pl.BlockSpec((None, SUBLANES, d), lambda b, j: (b, jnp.maximum(j * sub - 1, 0), 0))
        nxt = pl.BlockSpec((None, SUBLANES, d), lambda b, j: (b, jnp.minimum((j + 1) * sub, last), 0))
        return prev, nxt
    kb = tm // a // SUBLANES
    last = t // a // SUBLANES - 1
    shape = (None, SUBLANES, SUBLANES, d)
    prev = pl.BlockSpec(shape, lambda b, j: (b, a // SUBLANES - 1, jnp.maximum(j * kb - 1, 0), 0))
    nxt = pl.BlockSpec(shape, lambda b, j: (b, 0, jnp.minimum((j + 1) * kb, last), 0))
    return prev, nxt


def _inproj(x, vfirst, mods, mod_row, lp, hind, *, x_tr, g_tr, a, tm):
    bsz, t, d = x.shape
    w = RW_WIDTH
    k = tm // a
    has_vres = vfirst is not None
    kern = functools.partial(_inproj_kernel, has_vres=has_vres, x_tr=x_tr, g_tr=g_tr, a=a, k=k)
    xv = _tok_view(x, x_tr, a)
    prev_spec, next_spec = _neighbour_specs(d, tm, t, x_tr, a)
    args = [xv, xv, xv]
    in_specs = [_tok_spec(d, tm, x_tr, a), prev_spec, next_spec]
    if has_vres:
        args.append(_tok_view(vfirst, x_tr, a))
        in_specs.append(_tok_spec(w, tm, x_tr, a))
    args.append(mods)
    in_specs.append(pl.BlockSpec((None, 6, d), lambda b, j: (mod_row(b), 0, 0)))
    small = [lp["gains"][0:1]] + lp["w_in_parts"] + [
        lp["rw_mu"], lp["rw_w0"], lp["rw_w_up_pad"], lp["rw_a0"], lp["rw_a_up_pad"],
        lp["rw_g_up"], lp["rw_k_k"], lp["rw_k_a"], lp["rw_r_k"]]
    if has_vres:
        small += [lp["rw_vres_down"], lp["rw_vres_up"], lp["rw_vres_bias"]]
    small.append(hind)
    args += [_warg(s) for s in small]
    in_specs += [_wspec(s) for s in small]
    gla_widths = (GLA_QKV_W, LANES, GLA_VW)
    tok = pl.BlockSpec((None, tm, w), lambda b, j: (b, j, 0))
    tok2 = pl.BlockSpec((2, None, tm, w), lambda b, j: (0, b, j, 0))
    s1 = jax.ShapeDtypeStruct((bsz, t, w), BF16)
    s2 = lambda dt: jax.ShapeDtypeStruct((2, bsz, t, w), dt)
    outs = pl.pallas_call(
        kern,
        grid=(bsz, t // tm),
        in_specs=in_specs,
        out_specs=[_tok_spec(f, tm, g_tr, a) for f in gla_widths]
        + [_tok_spec(2 * d, tm, False, a)] + [tok] * 5 + [tok2] * 3,
        out_shape=[_tok_shape(bsz, t, f, g_tr, a, GLA_IN_DTYPE) for f in gla_widths]
        + [_tok_shape(bsz, t, 2 * d, False, a, BF16)] + [s1] * 5 + [s2(F32), s2(BF16), s2(BF16)],
        compiler_params=_params(2),
        name="inproj",
    )(*args)
    gla = [_tok_unview(o, g_tr, t, f) for o, f in zip(outs[:3], gla_widths)]
    return gla, outs[3], outs[4:]


def _before(row, col, reverse, inclusive):
    if reverse:
        return (col >= row) if inclusive else (col > row)
    return (col <= row) if inclusive else (col < row)


def _cum_parts(lw, reverse):
    row = lax.broadcasted_iota(jnp.int32, (CHUNK, CHUNK), 0)
    col = lax.broadcasted_iota(jnp.int32, (CHUNK, CHUNK), 1)
    tri = jnp.where(_before(row, col, reverse, True), 1.0, 0.0).astype(BF16)
    cum = _dot_ind_lhs(tri, lw)
    tot = cum[0:1, :] if reverse else cum[CHUNK - 1:CHUNK, :]
    return cum, tot


def _rwscan_kernel(rf, vf, kkf, lwf, kdf, bdf, rb, vb, kkb, lwb, kdb, bdb, s0_ref,
                   yf_ref, yb_ref, sfin_ref, s_scr):
    c = pl.program_id(1)

    @pl.when(c == 0)
    def _():
        s_scr[...] = s0_ref[...]

    n = RW_HEAD_DIM
    pw = 2 * n
    lane = lax.broadcasted_iota(jnp.int32, (CHUNK, pw), 1)
    lo = lane < n
    row = lax.broadcasted_iota(jnp.int32, (CHUNK, pw), 0)
    eye2 = jnp.where(row == lane % n, 1.0, 0.0)
    grow = lax.broadcasted_iota(jnp.int32, (2 * CHUNK, 2 * pw), 0)
    gcol = lax.broadcasted_iota(jnp.int32, (2 * CHUNK, 2 * pw), 1)
    rp = grow % CHUNK
    cp = gcol % CHUNK
    bottom = grow // CHUNK

    def halves(z):
        zb = z.astype(BF16)
        zero = jnp.zeros_like(zb)
        return jnp.where(lo, zb, zero), jnp.where(lo, zero, zb)

    def bd(z):
        return jnp.concatenate(halves(z), axis=0)

    def pick(z):
        return jnp.where(lo, z[:n], z[n:])

    chains = []
    dirs = ((rf, vf, kkf, lwf, kdf, bdf, yf_ref), (rb, vb, kkb, lwb, kdb, bdb, yb_ref))
    for bi in range(s_scr.shape[0]):
        for di, (r_ref, v_ref, kk_ref, lw_ref, kd_ref, bd_ref, y_ref) in enumerate(dirs):
            reverse = di == 1
            lw = lw_ref[bi]
            cum, tot = _cum_parts(lw, reverse)
            e_neg = jnp.exp(-cum)
            e_end = jnp.exp(tot - cum)
            kd = kd_ref[bi].astype(F32)
            bdv = bd_ref[bi].astype(F32)
            ops = dict(
                rt=(r_ref[bi].astype(F32) * jnp.exp(cum)).astype(BF16),
                at=(kk_ref[bi].astype(F32) * jnp.exp(cum - lw)).astype(BF16),
                kt=(kd * e_neg).astype(BF16),
                bt=(bdv * e_neg).astype(BF16),
                kh=(kd * e_end).astype(BF16),
                bh=(bdv * e_end).astype(BF16),
                v=v_ref[bi],
                g_tot=jnp.exp(tot),
            )
            keep = (cp > rp - bottom) if reverse else (cp < rp + bottom)
            for hp in range(RW_HEADS // 2):
                sl = slice(hp * pw, (hp + 1) * pw)
                ch = {k: val[:, sl] for k, val in ops.items()}
                ch.update(bi=bi, di=di, hp=hp, sl=sl, keep=keep, y_ref=y_ref)
                chains.append(ch)

    for ch in chains:
        g = _dot_nt(jnp.concatenate([ch["at"], ch["rt"]], axis=0),
                    jnp.concatenate(halves(ch["bt"]) + halves(ch["kt"]), axis=0))
        g = jnp.where(ch["keep"], g, 0.0).astype(BF16)
        ch["l"] = g[:CHUNK, :pw]
        ch["ak"] = g[:CHUNK, pw:]
        ch["rbk"] = g[CHUNK:, :]
    for ch in chains:
        ch["p"] = _dot(ch["l"], bd(ch["l"]))
        ch["av"] = _dot(ch["ak"], bd(ch["v"]))
        ch["kv"] = pick(_dot_tn(ch["v"], ch["kh"]))
        ch["x"] = eye2 - ch["l"].astype(F32)
    for _ in range(4):
        for ch in chains:
            z = _dot(jnp.concatenate([ch["x"], ch["p"]], axis=0), bd(ch["p"]))
            ch["x"] = ch["x"] + z[:CHUNK]
            ch["p"] = z[CHUNK:]
    for ch in chains:
        ch["x"] = ch["x"] + _dot(ch["x"], bd(ch["p"]))
    for ch in chains:
        wu = _dot(ch["x"], jnp.concatenate([bd(ch["at"]), bd(ch["av"])], axis=1))
        ch["wm"] = wu[:, :pw].astype(BF16)
        ch["u0"] = -wu[:, pw:]
    for ch in chains:
        t = ch["u0"].T
        ch["u0t"] = jnp.concatenate([t[:n], t[n:]], axis=1)
    for ch in chains:
        s = s_scr[ch["bi"], ch["di"], ch["hp"]]
        ch["s"] = s
        pr = _dot_nt(jnp.concatenate([ch["wm"], ch["rt"]], axis=0), bd(s))
        ch["u"] = ch["u0"] - pr[:CHUNK]
        ch["rs"] = pr[CHUNK:]
        ch["ut"] = ch["u0t"] - _dot_nt(s, bd(ch["wm"]))
    for ch in chains:
        y = ch["rs"] + _dot(ch["rbk"], jnp.concatenate([bd(ch["u"]), bd(ch["v"])], axis=0))
        ch["y_ref"][ch["bi"], :, ch["sl"]] = y.astype(ch["y_ref"].dtype)
        s_scr[ch["bi"], ch["di"], ch["hp"]] = (ch["s"] * ch["g_tot"] + ch["kv"]
                                               + _dot(ch["ut"], bd(ch["bh"])))

    @pl.when(c == pl.num_programs(1) - 1)
    def _():
        sfin_ref[...] = s_scr[...]


RW_STATE_SHAPE = (2, RW_HEADS // 2, RW_HEAD_DIM, 2 * RW_HEAD_DIM)


def _rwscan(r, v, kk, lw, kd, bd, s0):
    bsz, t, w = r.shape
    nc = t // CHUNK
    nr = RW_SCAN_ROWS if bsz % RW_SCAN_ROWS == 0 else 1
    tok_f = pl.BlockSpec((nr, CHUNK, w), lambda b, c: (b, c, 0))
    tok_b = pl.BlockSpec((nr, CHUNK, w), lambda b, c: (b, nc - 1 - c, 0))
    dir_f = pl.BlockSpec((None, nr, CHUNK, w), lambda b, c: (0, b, c, 0))
    dir_b = pl.BlockSpec((None, nr, CHUNK, w), lambda b, c: (1, b, nc - 1 - c, 0))
    st = pl.BlockSpec((nr,) + RW_STATE_SHAPE, lambda b, c: (b, 0, 0, 0, 0))
    y_shape = jax.ShapeDtypeStruct((bsz, t, w), BF16)
    yf, yb, sfin = pl.pallas_call(
        _rwscan_kernel,
        grid=(bsz // nr, nc),
        in_specs=[tok_f, tok_f, tok_f, dir_f, dir_f, dir_f,
                  tok_b, tok_b, tok_b, dir_b, dir_b, dir_b, st],
        out_specs=[tok_f, tok_b, st],
        out_shape=[y_shape, y_shape, jax.ShapeDtypeStruct(s0.shape, F32)],
        scratch_shapes=[pltpu.VMEM((nr,) + RW_STATE_SHAPE, F32)],
        compiler_params=_params(2),
        name="rwscan",
    )(r, v, kk, lw, kd, bd, r, v, kk, lw, kd, bd, s0)
    return yf, yb, sfin


def _rw_output(yf, yb, g, gb, gn, hind):
    y = yf.astype(F32) + yb.astype(F32)
    inv_n = 1.0 / RW_HEAD_DIM
    mean = _head_sums(y, hind) * inv_n
    yc = y - mean
    var = _head_sums(yc * yc, hind) * inv_n
    yn = yc * lax.rsqrt(var + RW_GN_EPS) * gn[0:1, :] + gn[1:2, :]
    return yn * g.astype(F32) + gb.astype(F32)


GLA_STATE_SHAPE = (GLA_HEADS, GLA_DV, GLA_DK)


def _gla_block_scan(q, k, v, la, s_scr, nb, reverse, emit):
    row = lax.broadcasted_iota(jnp.int32, (CHUNK, CHUNK), 0)
    col = lax.broadcasted_iota(jnp.int32, (CHUNK, CHUNK), 1)
    keep = _before(row, col, reverse, True)
    chains = []
    for ci in (reversed(range(nb)) if reverse else range(nb)):
        rows = slice(ci * CHUNK, (ci + 1) * CHUNK)
        cum, tot = _cum_parts(la[rows, :], reverse)
        kc = k[rows, :].astype(F32)
        q_dec = (q[rows, :].astype(F32) * jnp.exp(cum)).astype(BF16)
        k_inv = (kc * jnp.exp(-cum)).astype(BF16)
        k_end = (kc * jnp.exp(tot - cum)).astype(BF16)
        dec = jnp.exp(tot)
        vc = v[rows, :]
        for h in range(GLA_HEADS):
            sk = slice(h * GLA_DK, (h + 1) * GLA_DK)
            sv = slice(h * GLA_DV, (h + 1) * GLA_DV)
            chains.append(dict(h=h, rows=rows, sv=sv, q=q_dec[:, sk], ki=k_inv[:, sk],
                               ke=k_end[:, sk], dec=dec[:, sk], v=vc[:, sv]))
    for ch in chains:
        ch["sc"] = jnp.where(keep, _dot_nt(ch["q"], ch["ki"]), 0.0)
        ch["kv"] = _dot_tn(ch["v"], ch["ke"])
    state = {}
    for ch in chains:
        s = state[ch["h"]] if ch["h"] in state else s_scr[ch["h"]]
        ch["s"] = s
        state[ch["h"]] = s * ch["dec"] + ch["kv"]
    for h, s in state.items():
        s_scr[h] = s
    for ch in chains:
        emit(ch["rows"], ch["sv"], _dot(ch["sc"], ch["v"]) + _dot_nt(ch["q"], ch["s"]))


def _glafwd_kernel(q_cur, q_prev, q_next, ald_ref, conv, up, bias, s0_ref,
                   o_q, o_k, o_v, o_la, o_of, sfin_ref, s_scr, *, nb):
    j = pl.program_id(1)

    @pl.when(j == 0)
    def _():
        s_scr[...] = s0_ref[...]

    x = q_cur[...].astype(F32)
    prev, nxt = _shifted(x, q_prev[...], q_next[...], j == 0, j == pl.num_programs(1) - 1)
    y = _silu(conv[0:1, :] * prev + conv[1:2, :] * x + conv[2:3, :] * nxt)
    q = (y[:, 0:GLA_KW] * (GLA_DK ** -0.5)).astype(BF16)
    k = y[:, GLA_KW:2 * GLA_KW].astype(BF16)
    v = y[:, 2 * GLA_KW:].astype(BF16)
    o_q[...] = q
    o_k[...] = k
    o_v[...] = v
    ald = ald_ref[...]
    log_alpha = lambda di: _log_sigmoid(_dot(ald, up[di]) + bias[di:di + 1, :]) / GLA_TAU
    o_la[...] = log_alpha(1)

    def emit(rows, lanes, o):
        o_of[rows, lanes] = o.astype(o_of.dtype)

    _gla_block_scan(q, k, v, log_alpha(0), s_scr, nb, False, emit)

    @pl.when(j == pl.num_programs(1) - 1)
    def _():
        sfin_ref[...] = s_scr[...]


def _glafwd(qkv, ald, s0, lp):
    bsz, t, fq = qkv.shape
    nb = min(GLA_BLOCK_CHUNKS, t // CHUNK)
    tb = nb * CHUNK
    assert t % tb == 0
    prev_spec, next_spec = _halo_specs(fq, tb, t)
    tok = lambda f: pl.BlockSpec((None, tb, f), lambda b, j: (b, j, 0))
    st = pl.BlockSpec((None,) + GLA_STATE_SHAPE, lambda b, j: (b, 0, 0, 0))
    small = [lp["gla_conv"], lp["gla_alpha_up_pad"], lp["gla_alpha_bias"]]
    act = lambda f, dt: jax.ShapeDtypeStruct((bsz, t, f), dt)
    return pl.pallas_call(
        functools.partial(_glafwd_kernel, nb=nb),
        grid=(bsz, t // tb),
        in_specs=[tok(fq), prev_spec, next_spec, tok(LANES)]
        + [_const_spec(s.shape) for s in small] + [st],
        out_specs=[tok(GLA_KW), tok(GLA_KW), tok(GLA_VW), tok(GLA_KW), tok(GLA_VW), st],
        out_shape=[act(GLA_KW, BF16), act(GLA_KW, BF16), act(GLA_VW, BF16), act(GLA_KW, F32),
                   act(GLA_VW, BF16), jax.ShapeDtypeStruct(s0.shape, F32)],
        scratch_shapes=[pltpu.VMEM(GLA_STATE_SHAPE, F32)],
        compiler_params=_params(2),
        name="glafwd",
    )(qkv, qkv, qkv, ald, *small, s0)


def _glabwd_kernel(q_ref, k_ref, v_ref, la_ref, of_ref, gate_ref, nw, s0_ref,
                   y_ref, sfin_ref, s_scr, o_scr, *, nb, y_tr, a, k):
    j = pl.program_id(1)

    @pl.when(j == 0)
    def _():
        s_scr[...] = s0_ref[...]

    def emit(rows, lanes, o):
        o_scr[rows, lanes] = o

    _gla_block_scan(q_ref, k_ref, v_ref, la_ref, s_scr, nb, True, emit)
    o = of_ref[...].astype(F32) + o_scr[...]
    gate = gate_ref[...].astype(F32)
    ys = []
    for h in range(GLA_HEADS):
        sv = slice(h * GLA_DV, (h + 1) * GLA_DV)
        ys.append(_rms(o[:, sv], nw[...]) * _silu(gate[:, sv]))
    _store_tok(y_ref, jnp.concatenate(ys, axis=1), y_tr, a, k)

    @pl.when(j == pl.num_programs(1) - 1)
    def _():
        sfin_ref[...] = s_scr[...]


def _glabwd(q, k, v, la, of, gate, s0, lp, *, y_tr, a):
    bsz, t, _ = q.shape
    w = GLA_VW
    tb = min(TOKEN_TILE, t)
    nb = tb // CHUNK
    nblk = t // tb
    kk = tb // a
    tok = lambda f: pl.BlockSpec((None, tb, f), lambda b, j: (b, nblk - 1 - j, 0))
    st = pl.BlockSpec((None,) + GLA_STATE_SHAPE, lambda b, j: (b, 0, 0, 0))
    if y_tr:
        y_spec = pl.BlockSpec((None, a, kk, w), lambda b, j: (b, 0, nblk - 1 - j, 0))
    else:
        y_spec = tok(w)
    y, sfin = pl.pallas_call(
        functools.partial(_glabwd_kernel, nb=nb, y_tr=y_tr, a=a, k=kk),
        grid=(bsz, nblk),
        in_specs=[tok(GLA_KW), tok(GLA_KW), tok(w), tok(GLA_KW), tok(w), tok(w),
                  _const_spec((1, GLA_DV)), st],
        out_specs=[y_spec, st],
        out_shape=[_tok_shape(bsz, t, w, y_tr, a, BF16), jax.ShapeDtypeStruct(s0.shape, F32)],
        scratch_shapes=[pltpu.VMEM(GLA_STATE_SHAPE, F32), pltpu.VMEM((tb, w), F32)],
        compiler_params=_params(2),
        name="glabwd",
    )(q, k, v, la, of, gate, lp["gla_norm_w"], s0)
    return _tok_unview(y, y_tr, t, w), sfin


def _merge_kernel(x_ref, yf_ref, yb_ref, g_ref, gb_ref, ygla_ref, mg_ref, mod_ref, gains, gn, hind,
                  w_rwo, w_glao, w_mo, w1, w2, o_ref, *, x_tr, a, k):
    d = x_ref.shape[-1]
    x = _load_tok(x_ref, x_tr, a, k)
    mg = mg_ref[...].astype(F32)
    yrw = _rw_output(yf_ref[...], yb_ref[...], g_ref[...], gb_ref[...], gn, hind[...])
    br = (_sigmoid(mg[:, :d]) * _dot(yrw, w_rwo[...])
          + _sigmoid(mg[:, d:]) * _dot(ygla_ref[...], w_glao[...]))
    m = _dot(br, w_mo[...])
    x1 = x + mod_ref[2:3, :] * _rms(m, gains[1:2, :])
    h2 = _rms(x1, gains[2:3, :]) * (1.0 + mod_ref[4:5, :]) + mod_ref[3:4, :]
    hid = jnp.maximum(_dot(h2, w1[...]), 0.0)
    f = _dot(hid * hid, w2[...])
    x2 = x1 + mod_ref[5:6, :] * _rms(f, gains[3:4, :])
    _store_tok(o_ref, x2, x_tr, a, k)


def _merge(x, rw_parts, ygla, mg, mods, mod_row, gains, gn, hind, ws, *, x_tr, a, tm):
    bsz, t, d = x.shape
    k = tm // a
    kern = functools.partial(_merge_kernel, x_tr=x_tr, a=a, k=k)
    rw_tok = _tok_spec(RW_WIDTH, tm, False, a)
    out = pl.pallas_call(
        kern,
        grid=(bsz, t // tm),
        in_specs=[_tok_spec(d, tm, x_tr, a), rw_tok, rw_tok, rw_tok, rw_tok,
                  _tok_spec(GLA_VW, tm, False, a), _tok_spec(2 * d, tm, False, a),
                  pl.BlockSpec((None, 6, d), lambda b, j: (mod_row(b), 0, 0)),
                  _const_spec(gains.shape), _const_spec(gn.shape), _const_spec(hind.shape)]
        + [_wspec(w) for w in ws],
        out_specs=_tok_spec(d, tm, x_tr, a),
        out_shape=_tok_shape(bsz, t, d, x_tr, a, F32),
        compiler_params=_params(2),
        name="merge",
    )(_tok_view(x, x_tr, a), *rw_parts, ygla, mg, mods, gains, gn, hind, *[_warg(w) for w in ws])
    return _tok_unview(out, x_tr, t, d)


def _stacked_weights(p):
    w_in = p["w_in"]
    g0 = RW_COLS
    g1 = g0 + GLA_QKV_W
    g2 = g1 + 2 * GLA_GATE_RANK
    g3 = g2 + GLA_VW
    bf = lambda w: w.astype(BF16)
    return {
        "w_in_parts": [
            bf(jnp.concatenate([w_in[:, :, :g0], jnp.pad(
                w_in[:, :, g1:g2], ((0, 0), (0, 0), (0, LANES - 2 * GLA_GATE_RANK)))], axis=2)),
            bf(w_in[:, :, g0:g1]),
            bf(w_in[:, :, g2:g3]),
            bf(w_in[:, :, g3:]),
        ],
        "merge_ws": [bf(p[name]) for name in ("rw_out", "gla_out", "merge_out", "mlp_w1", "mlp_w2")],
    }


def _layer_params(l, p, stacked):
    bf = lambda w: w.astype(BF16)
    row = lambda w: w.reshape(1, -1)
    lp = {
        "w_in_parts": [_Stacked(w, l) for w in stacked["w_in_parts"]],
        "rw_mu": row(p["rw_mu"][l]),
        "rw_w0": p["rw_w0"][l],
        "rw_a0": p["rw_a0"][l],
        "rw_g_up": bf(p["rw_g_up"][l]),
        "rw_k_k": row(p["rw_k_k"][l]),
        "rw_k_a": row(p["rw_k_a"][l]),
        "rw_r_k": row(p["rw_r_k"][l]),
        "rw_gn": jnp.stack([p["rw_gn_w"][l], p["rw_gn_b"][l]]),
        "gla_conv": p["gla_conv"][l],
        "gla_alpha_bias": p["gla_alpha_bias"][l],
        "gla_norm_w": row(p["gla_norm_w"][l]),
        "merge_ws": [_Stacked(w, l) for w in stacked["merge_ws"]],
        "gains": jnp.stack([p["norm_mix_pre"][l], p["norm_mix_post"][l],
                            p["norm_ffn_pre"][l], p["norm_ffn_post"][l]]),
    }
    pad_dir = lambda w, r: jnp.stack([jnp.pad(w[di], ((di * r, LANES - (di + 1) * r), (0, 0)))
                                      for di in range(2)])
    lp["rw_w_up_pad"] = bf(pad_dir(p["rw_w_up"][l], p["rw_w_up"].shape[2]))
    lp["rw_a_up_pad"] = bf(pad_dir(p["rw_a_up"][l], p["rw_a_up"].shape[2]))
    lp["gla_alpha_up_pad"] = bf(pad_dir(p["gla_alpha_up"][l], GLA_GATE_RANK))
    if l > 0:
        lp["rw_vres_down"] = bf(p["rw_vres_down"][l - 1])
        lp["rw_vres_up"] = bf(p["rw_vres_up"][l - 1])
        lp["rw_vres_bias"] = row(p["rw_vres_bias"][l - 1])
    return lp


def _head_indicator():
    h = jnp.arange(RW_WIDTH) // RW_HEAD_DIM
    return (h[:, None] == h[None, :]).astype(BF16)


def _mixer(x, mods, mod_row, vfirst, s_rw, s_gla, lp, hind, *, p_col, need_out):
    bsz, t, d = x.shape
    tm_merge = min(MERGE_TILE, t)
    if p_col is None:
        x_tr, g_tr, a_p, a_q = False, False, SUBLANES, SUBLANES
    else:
        x_tr, g_tr = p_col, True
        rows = t // GRID_W
        a_p, a_q = (rows, GRID_W) if p_col else (GRID_W, rows)
    (gq, gald, gg), mg, (r, v, kk, g, gb, lw, kd, bd) = _inproj(
        x, vfirst, mods, mod_row, lp, hind, x_tr=x_tr, g_tr=g_tr, a=a_p, tm=min(PROJ_TILE, t))
    yf, yb, s_rw_out = _rwscan(r, v, kk, lw, kd, bd, s_rw)
    q, k, gv, la_b, of, s_gla_f = _glafwd(gq, gald, s_gla[0], lp)
    ygla, s_gla_b = _glabwd(q, k, gv, la_b, of, gg, s_gla[1], lp, y_tr=g_tr, a=a_q)
    s_gla_out = (s_gla_f, s_gla_b)
    if not need_out:
        return None, v, s_rw_out, s_gla_out
    x_new = _merge(x, (yf, yb, g, gb), ygla, mg, mods, mod_row, lp["gains"], lp["rw_gn"], hind,
                   lp["merge_ws"], x_tr=x_tr, a=a_p, tm=tm_merge)
    return x_new, v, s_rw_out, s_gla_out


def kernel(x, c, ctx, c_ctx, w_in, rw_mu, rw_w0, rw_w_up, rw_a0, rw_a_up, rw_g_up, rw_k_k, rw_k_a,
           rw_r_k, rw_gn_w, rw_gn_b, rw_vres_down, rw_vres_up, rw_vres_bias, rw_out, gla_conv,
           gla_alpha_up, gla_alpha_bias, gla_norm_w, gla_out, merge_out, mlp_w1, mlp_w2, ada_w,
           ada_b, norm_mix_pre, norm_mix_post, norm_ffn_pre, norm_ffn_post):
    p = dict(w_in=w_in, rw_mu=rw_mu, rw_w0=rw_w0, rw_w_up=rw_w_up, rw_a0=rw_a0, rw_a_up=rw_a_up,
             rw_g_up=rw_g_up, rw_k_k=rw_k_k, rw_k_a=rw_k_a, rw_r_k=rw_r_k.reshape(rw_r_k.shape[0], -1),
             rw_gn_w=rw_gn_w, rw_gn_b=rw_gn_b, rw_vres_down=rw_vres_down, rw_vres_up=rw_vres_up,
             rw_vres_bias=rw_vres_bias, rw_out=rw_out, gla_conv=gla_conv, gla_alpha_up=gla_alpha_up,
             gla_alpha_bias=gla_alpha_bias, gla_norm_w=gla_norm_w, gla_out=gla_out,
             merge_out=merge_out, mlp_w1=mlp_w1, mlp_w2=mlp_w2, norm_mix_pre=norm_mix_pre,
             norm_mix_post=norm_mix_post, norm_ffn_pre=norm_ffn_pre, norm_ffn_post=norm_ffn_post)
    bsz, t, d = x.shape
    t_ctx = ctx.shape[1]
    depth = w_in.shape[0]
    assert bsz < SUBLANES and d % LANES == 0
    assert t % (GRID_W * SUBLANES) == 0 and t_ctx % CHUNK == 0
    assert t % min(TOKEN_TILE, t) == 0 and t_ctx % min(TOKEN_TILE, t_ctx) == 0

    cc = jnp.concatenate([c, c_ctx[None, :], jnp.zeros((SUBLANES - 1 - bsz, d), F32)], axis=0)
    mods = _ada_mods(cc, ada_w, ada_b).reshape(depth, SUBLANES, 6, d)
    hind = _head_indicator()
    stacked = _stacked_weights(p)
    lat_row = lambda b: b
    ctx_row = lambda b: bsz

    x_lat, x_ctx = x, ctx
    vf_lat = vf_ctx = None
    for l in range(depth):
        last = l == depth - 1
        lp = _layer_params(l, p, stacked)
        z_rw = jnp.zeros((bsz,) + RW_STATE_SHAPE, F32)
        z_gla = (jnp.zeros((bsz,) + GLA_STATE_SHAPE, F32),) * 2
        x_ctx_new, v_ctx, s_rw, s_gla = _mixer(
            x_ctx, mods[l], ctx_row, vf_ctx, z_rw, z_gla, lp, hind, p_col=None, need_out=not last)
        x_lat, v_lat, _, _ = _mixer(
            x_lat, mods[l], lat_row, vf_lat, s_rw, s_gla, lp, hind,
            p_col=(l % 2 == 1), need_out=True)
        if l == 0:
            vf_lat, vf_ctx = v_lat, v_ctx
        if not last:
            x_ctx = x_ctx_new
    return x_lat
```

```python
import functools
from typing import NamedTuple

import jax
import jax.numpy as jnp
from jax import lax
from jax.experimental import pallas as pl
from jax.experimental.pallas import tpu as pltpu

F32 = jnp.float32
BF16 = jnp.bfloat16

GRID_W = 64
RMS_EPS = 1e-6
RW_HEADS = 8
RW_HEAD_DIM = 64
RW_WIDTH = RW_HEADS * RW_HEAD_DIM
RW_GN_EPS = 64e-5
RW_COLS = 1920
GLA_HEADS = 4
GLA_DK = 64
GLA_DV = 128
GLA_KW = GLA_HEADS * GLA_DK
GLA_VW = GLA_HEADS * GLA_DV
GLA_QKV_W = 2 * GLA_KW + GLA_VW
GLA_GATE_RANK = 16
GLA_TAU = 16.0
CHUNK = 64
GLA_BLOCK_CHUNKS = 16
RW_SCAN_ROWS = 4
RW_SCAN_CHUNKS = 2
LANES = 128
SUBLANES = 8
HALO = 16
VMEM_LIMIT = 56 * 1024 * 1024
TOKEN_TILE = 1024
PROJ_TILE = 512
MERGE_TILE = 512
GLA_IN_DTYPE = F32


def _dot(a, b):
    return jnp.dot(a.astype(BF16), b.astype(BF16), preferred_element_type=F32)


def _dot_nt(a, b):
    return lax.dot_general(a.astype(BF16), b.astype(BF16), (((1,), (1,)), ((), ())),
                           preferred_element_type=F32)


def _dot_tn(a, b):
    return lax.dot_general(a.astype(BF16), b.astype(BF16), (((0,), (0,)), ((), ())),
                           preferred_element_type=F32)


def _split2(x):
    hi = x.astype(BF16)
    lo = (x - hi.astype(F32)).astype(BF16)
    return hi, lo


def _head_sums(x, ind):
    return jnp.dot(x.astype(BF16), ind, preferred_element_type=F32)


def _dot_ind_lhs(ind, x):
    hi, lo = _split2(x)
    return (jnp.dot(ind, hi, preferred_element_type=F32)
            + jnp.dot(ind, lo, preferred_element_type=F32))


def _sigmoid(x):
    return jax.nn.sigmoid(x)


def _silu(x):
    return x * jax.nn.sigmoid(x)


def _softplus(z):
    return jnp.maximum(z, 0.0) + jnp.log(1.0 + jnp.exp(-jnp.abs(z)))


def _log_sigmoid(z):
    return -_softplus(-z)


def _rms(x, gain):
    return x * lax.rsqrt(jnp.mean(x * x, axis=-1, keepdims=True) + RMS_EPS) * gain


def _load_tok(ref, transposed, a, k):
    if not transposed:
        return ref[...]
    return jnp.concatenate([ref[:, i, :] for i in range(k)], axis=0)


def _store_tok(ref, val, transposed, a, k):
    val = val.astype(ref.dtype)
    if not transposed:
        ref[...] = val
    else:
        for i in range(k):
            ref[:, i, :] = val[i * a:(i + 1) * a, :]


def _tok_view(arr, transposed, a):
    if not transposed:
        return arr
    b, t, f = arr.shape
    return arr.reshape(b, a, t // a, f)


def _tok_unview(arr, transposed, t, f):
    if not transposed:
        return arr
    return arr.reshape(arr.shape[0], t, f)


def _tok_spec(f, tm, transposed, a):
    if not transposed:
        return pl.BlockSpec((None, tm, f), lambda b, j: (b, j, 0))
    return pl.BlockSpec((None, a, tm // a, f), lambda b, j: (b, 0, j, 0))


def _tok_shape(bsz, t, f, transposed, a, dtype):
    if not transposed:
        return jax.ShapeDtypeStruct((bsz, t, f), dtype)
    return jax.ShapeDtypeStruct((bsz, a, t // a, f), dtype)


def _const_spec(shape):
    nd = len(shape)
    return pl.BlockSpec(shape, lambda *_: (0,) * nd, pipeline_mode=pl.Buffered(1))


class _Stacked(NamedTuple):
    arr: jax.Array
    layer: int


def _warg(w):
    return w.arr if isinstance(w, _Stacked) else w


def _wspec(w):
    if not isinstance(w, _Stacked):
        return _const_spec(w.shape)
    shape = w.arr.shape[1:]
    index = (w.layer,) + (0,) * len(shape)
    return pl.BlockSpec((None,) + shape, lambda *_: index, pipeline_mode=pl.Buffered(1))


def _params(ndim):
    return pltpu.CompilerParams(dimension_semantics=("arbitrary",) * ndim,
                                vmem_limit_bytes=VMEM_LIMIT)


def _halo_specs(f, tm, t):
    sub = tm // HALO
    last = t // HALO - 1
    prev = pl.BlockSpec((None, HALO, f), lambda b, j: (b, jnp.maximum(j * sub - 1, 0), 0))
    nxt = pl.BlockSpec((None, HALO, f), lambda b, j: (b, jnp.minimum((j + 1) * sub, last), 0))
    return prev, nxt


def _shifted(cur, prev_blk, next_blk, first, last):
    tm = cur.shape[0]
    keep_prev = jnp.where(first, 0.0, 1.0)
    keep_next = jnp.where(last, 0.0, 1.0)
    ext = jnp.concatenate([cur, next_blk.astype(F32) * keep_next,
                           prev_blk.astype(F32) * keep_prev], axis=0)
    n = ext.shape[0]
    return pltpu.roll(ext, 1, 0)[0:tm], pltpu.roll(ext, n - 1, 0)[0:tm]


def _ada_kernel(c_ref, w_ref, b_ref, o_ref):
    cc = c_ref[...]
    o_ref[...] = _dot(_silu(cc), w_ref[...]) + b_ref[...]


def _ada_mods(cc, ada_w, ada_b):
    nl, d, n6 = ada_w.shape
    tn = d
    return pl.pallas_call(
        _ada_kernel,
        grid=(nl, n6 // tn),
        in_specs=[pl.BlockSpec((SUBLANES, d), lambda l, n: (0, 0)),
                  pl.BlockSpec((None, d, tn), lambda l, n: (l, 0, n)),
                  pl.BlockSpec((None, 1, tn), lambda l, n: (l, 0, n))],
        out_specs=pl.BlockSpec((None, SUBLANES, tn), lambda l, n: (l, 0, n)),
        out_shape=jax.ShapeDtypeStruct((nl, SUBLANES, n6), F32),
        compiler_params=_params(2),
        name="ada",
    )(cc, ada_w, ada_b.reshape(nl, 1, n6))


def _inproj_kernel(*refs, has_vres, x_tr, g_tr, a, k):
    it = iter(refs)
    x_ref, xp_ref, xn_ref = next(it), next(it), next(it)
    vf_ref = next(it) if has_vres else None
    mod_ref, gain_ref, w_rw, w_q, w_gg, w_mg = (next(it) for _ in range(6))
    mu, w0, w_up, a0, a_up, g_up, k_k, k_a, r_k = (next(it) for _ in range(9))
    if has_vres:
        vdown, vup, vbias = next(it), next(it), next(it)
    hind = next(it)
    o_q, o_ald, o_gg, o_mg = (next(it) for _ in range(4))
    o_r, o_v, o_kk, o_g, o_gb, o_lw, o_kd, o_bd = (next(it) for _ in range(8))

    j = pl.program_id(1)
    keep_prev = jnp.where(j == 0, 0.0, 1.0)
    keep_next = jnp.where(j == pl.num_programs(1) - 1, 0.0, 1.0)
    x = _load_tok(x_ref, x_tr, a, k)
    tm = x.shape[0]
    xp = xp_ref[SUBLANES - 1] if x_tr else xp_ref[...]
    xn = xn_ref[0] if x_tr else xn_ref[...]
    modulate = lambda z: _rms(z, gain_ref[...]) * (1.0 + mod_ref[1:2, :]) + mod_ref[0:1, :]
    h = modulate(x)
    hb = h.astype(BF16)
    h_ext = jnp.concatenate([modulate(xp) * keep_prev, h, modulate(xn) * keep_next], axis=0)
    f_ext = jnp.dot(h_ext.astype(BF16), w_rw[...], preferred_element_type=F32)
    proj = lambda w: jnp.dot(hb, w[...], preferred_element_type=F32)
    _store_tok(o_mg, proj(w_mg), False, a, k)
    _store_tok(o_q, proj(w_q), g_tr, a, k)
    _store_tok(o_ald, f_ext[SUBLANES:SUBLANES + tm, RW_COLS:], g_tr, a, k)
    _store_tok(o_gg, proj(w_gg), g_tr, a, k)

    f = f_ext[SUBLANES:SUBLANES + tm, :RW_COLS]
    nbr = (f_ext[SUBLANES - 1:SUBLANES - 1 + tm, :RW_COLS]
           + f_ext[SUBLANES + 1:SUBLANES + 1 + tm, :RW_COLS])
    fs = f + mu[...] * (0.5 * nbr - f)
    w = RW_WIDTH
    r = fs[:, 0:w]
    kx = fs[:, w:2 * w]
    v = fs[:, 2 * w:3 * w]
    wd = fs[:, 3 * w:3 * w + LANES]
    ad = fs[:, 3 * w + LANES:3 * w + 2 * LANES]
    gd = fs[:, 3 * w + 2 * LANES:3 * w + 3 * LANES]
    if has_vres:
        vf = _load_tok(vf_ref, x_tr, a, k).astype(F32)
        mix = _sigmoid(vbias[...] + _dot(_dot(v, vdown[...]), vup[...]))
        v = v + (vf - v) * mix
    kk = kx * k_k[...]
    kk = kk * lax.rsqrt(_head_sums(kk * kk, hind[...]) + 1e-12)
    twd = jnp.tanh(wd)
    ksum = None
    for di in range(2):
        wlog = -_softplus(-(w0[di:di + 1, :] + _dot(twd, w_up[di]))) - 0.5
        o_lw[di] = -jnp.exp(wlog)
        lr = _sigmoid(a0[di:di + 1, :] + _dot(ad, a_up[di]))
        kd = kx * (1.0 + (lr - 1.0) * k_a[...])
        o_kd[di] = kd.astype(o_kd.dtype)
        o_bd[di] = (lr * kk).astype(o_bd.dtype)
        ksum = kd if ksum is None else ksum + kd
    g = _dot(_sigmoid(gd), g_up[...])
    bonus = _head_sums(r * ksum * r_k[...], hind[...]) * v
    o_r[...] = r.astype(o_r.dtype)
    o_v[...] = v.astype(o_v.dtype)
    o_kk[...] = kk.astype(o_kk.dtype)
    o_g[...] = g.astype(o_g.dtype)
    o_gb[...] = (bonus * g).astype(o_gb.dtype)


def _neighbour_specs(d, tm, t, x_tr, a):
    sub = tm // SUBLANES
    if not x_tr:
        last = t // SUBLANES - 1
        prev = pl.BlockSpec((None, SUBLANES, d), lambda b, j: (b, jnp.maximum(j * sub - 1, 0), 0))
        nxt = pl.BlockSpec((None, SUBLANES, d), lambda b, j: (b, jnp.minimum((j + 1) * sub, last), 0))
        return prev, nxt
    kb = tm // a // SUBLANES
    last = t // a // SUBLANES - 1
    shape = (None, SUBLANES, SUBLANES, d)
    prev = pl.BlockSpec(shape, lambda b, j: (b, a // SUBLANES - 1, jnp.maximum(j * kb - 1, 0), 0))
    nxt = pl.BlockSpec(shape, lambda b, j: (b, 0, jnp.minimum((j + 1) * kb, last), 0))
    return prev, nxt


def _inproj(x, vfirst, mods, mod_row, lp, hind, *, x_tr, g_tr, a, tm):
    bsz, t, d = x.shape
    w = RW_WIDTH
    k = tm // a
    has_vres = vfirst is not None
    kern = functools.partial(_inproj_kernel, has_vres=has_vres, x_tr=x_tr, g_tr=g_tr, a=a, k=k)
    xv = _tok_view(x, x_tr, a)
    prev_spec, next_spec = _neighbour_specs(d, tm, t, x_tr, a)
    args = [xv, xv, xv]
    in_specs = [_tok_spec(d, tm, x_tr, a), prev_spec, next_spec]
    if has_vres:
        args.append(_tok_view(vfirst, x_tr, a))
        in_specs.append(_tok_spec(w, tm, x_tr, a))
    args.append(mods)
    in_specs.append(pl.BlockSpec((None, 6, d), lambda b, j: (mod_row(b), 0, 0)))
    small = [lp["gains"][0:1]] + lp["w_in_parts"] + [
        lp["rw_mu"], lp["rw_w0"], lp["rw_w_up_pad"], lp["rw_a0"], lp["rw_a_up_pad"],
        lp["rw_g_up"], lp["rw_k_k"], lp["rw_k_a"], lp["rw_r_k"]]
    if has_vres:
        small += [lp["rw_vres_down"], lp["rw_vres_up"], lp["rw_vres_bias"]]
    small.append(hind)
    args += [_warg(s) for s in small]
    in_specs += [_wspec(s) for s in small]
    gla_widths = (GLA_QKV_W, LANES, GLA_VW)
    tok = pl.BlockSpec((None, tm, w), lambda b, j: (b, j, 0))
    tok2 = pl.BlockSpec((2, None, tm, w), lambda b, j: (0, b, j, 0))
    s1 = jax.ShapeDtypeStruct((bsz, t, w), BF16)
    s2 = lambda dt: jax.ShapeDtypeStruct((2, bsz, t, w), dt)
    outs = pl.pallas_call(
        kern,
        grid=(bsz, t // tm),
        in_specs=in_specs,
        out_specs=[_tok_spec(f, tm, g_tr, a) for f in gla_widths]
        + [_tok_spec(2 * d, tm, False, a)] + [tok] * 5 + [tok2] * 3,
        out_shape=[_tok_shape(bsz, t, f, g_tr, a, GLA_IN_DTYPE) for f in gla_widths]
        + [_tok_shape(bsz, t, 2 * d, False, a, BF16)] + [s1] * 5 + [s2(F32), s2(BF16), s2(BF16)],
        compiler_params=_params(2),
        name="inproj",
    )(*args)
    gla = [_tok_unview(o, g_tr, t, f) for o, f in zip(outs[:3], gla_widths)]
    return gla, outs[3], outs[4:]


def _before(row, col, reverse, inclusive):
    if reverse:
        return (col >= row) if inclusive else (col > row)
    return (col <= row) if inclusive else (col < row)


def _cum_parts(lw, reverse):
    row = lax.broadcasted_iota(jnp.int32, (CHUNK, CHUNK), 0)
    col = lax.broadcasted_iota(jnp.int32, (CHUNK, CHUNK), 1)
    tri = jnp.where(_before(row, col, reverse, True), 1.0, 0.0).astype(BF16)
    cum = _dot_ind_lhs(tri, lw)
    tot = cum[0:1, :] if reverse else cum[CHUNK - 1:CHUNK, :]
    return cum, tot


def _rwscan_kernel(rf, vf, kkf, lwf, kdf, bdf, rb, vb, kkb, lwb, kdb, bdb, s0_ref,
                   yf_ref, yb_ref, sfin_ref, s_scr):
    c = pl.program_id(1)

    @pl.when(c == 0)
    def _():
        s_scr[...] = s0_ref[...]

    n = RW_HEAD_DIM
    pw = 2 * n
    lane = lax.broadcasted_iota(jnp.int32, (CHUNK, pw), 1)
    lo = lane < n
    row = lax.broadcasted_iota(jnp.int32, (CHUNK, pw), 0)
    eye2 = jnp.where(row == lane % n, 1.0, 0.0)
    grow = lax.broadcasted_iota(jnp.int32, (2 * CHUNK, 2 * pw), 0)
    gcol = lax.broadcasted_iota(jnp.int32, (2 * CHUNK, 2 * pw), 1)
    rp = grow % CHUNK
    cp = gcol % CHUNK
    bottom = grow // CHUNK

    def halves(z):
        zb = z.astype(BF16)
        zero = jnp.zeros_like(zb)
        return jnp.where(lo, zb, zero), jnp.where(lo, zero, zb)

    def bd(z):
        return jnp.concatenate(halves(z), axis=0)

    def pick(z):
        return jnp.where(lo, z[:n], z[n:])

    chains = []
    dirs = ((rf, vf, kkf, lwf, kdf, bdf, yf_ref), (rb, vb, kkb, lwb, kdb, bdb, yb_ref))
    nsub = rf.shape[1] // CHUNK
    for bi in range(s_scr.shape[0]):
        for di, (r_ref, v_ref, kk_ref, lw_ref, kd_ref, bd_ref, y_ref) in enumerate(dirs):
            reverse = di == 1
            order = list(reversed(range(nsub))) if reverse else list(range(nsub))
            for pos, ci in enumerate(order):
                rows = slice(ci * CHUNK, (ci + 1) * CHUNK)
                lw = lw_ref[bi, rows, :]
                cum, tot = _cum_parts(lw, reverse)
                e_neg = jnp.exp(-cum)
                e_end = jnp.exp(tot - cum)
                kd = kd_ref[bi, rows, :].astype(F32)
                bdv = bd_ref[bi, rows, :].astype(F32)
                ops = dict(
                    rt=(r_ref[bi, rows, :].astype(F32) * jnp.exp(cum)).astype(BF16),
                    at=(kk_ref[bi, rows, :].astype(F32) * jnp.exp(cum - lw)).astype(BF16),
                    kt=(kd * e_neg).astype(BF16),
                    bt=(bdv * e_neg).astype(BF16),
                    kh=(kd * e_end).astype(BF16),
                    bh=(bdv * e_end).astype(BF16),
                    v=v_ref[bi, rows, :],
                    g_tot=jnp.exp(tot),
                )
                keep = (cp > rp - bottom) if reverse else (cp < rp + bottom)
                for hp in range(RW_HEADS // 2):
                    sl = slice(hp * pw, (hp + 1) * pw)
                    ch = {k: val[:, sl] for k, val in ops.items()}
                    ch.update(bi=bi, di=di, hp=hp, pos=pos, rows=rows, sl=sl, keep=keep,
                              y_ref=y_ref)
                    chains.append(ch)

    for ch in chains:
        g = _dot_nt(jnp.concatenate([ch["at"], ch["rt"]], axis=0),
                    jnp.concatenate(halves(ch["bt"]) + halves(ch["kt"]), axis=0))
        g = jnp.where(ch["keep"], g, 0.0).astype(BF16)
        ch["l"] = g[:CHUNK, :pw]
        ch["ak"] = g[:CHUNK, pw:]
        ch["rbk"] = g[CHUNK:, :]
    for ch in chains:
        ch["p"] = _dot(ch["l"], bd(ch["l"]))
        ch["av"] = _dot(ch["ak"], bd(ch["v"]))
        ch["kv"] = pick(_dot_tn(ch["v"], ch["kh"]))
        ch["x"] = eye2 - ch["l"].astype(F32)
    for _ in range(4):
        for ch in chains:
            z = _dot(jnp.concatenate([ch["x"], ch["p"]], axis=0), bd(ch["p"]))
            ch["x"] = ch["x"] + z[:CHUNK]
            ch["p"] = z[CHUNK:]
    for ch in chains:
        ch["x"] = ch["x"] + _dot(ch["x"], bd(ch["p"]))
    for ch in chains:
        wu = _dot(ch["x"], jnp.concatenate([bd(ch["at"]), bd(ch["av"])], axis=1))
        ch["wm"] = wu[:, :pw].astype(BF16)
        ch["u0"] = -wu[:, pw:]
    for ch in chains:
        t = ch["u0"].T
        ch["u0t"] = jnp.concatenate([t[:n], t[n:]], axis=1)
    state = {}
    for pos in range(nsub):
        cur = [ch for ch in chains if ch["pos"] == pos]
        for ch in cur:
            key = (ch["bi"], ch["di"], ch["hp"])
            s = state[key] if key in state else s_scr[key]
            ch["s"] = s
            pr = _dot_nt(jnp.concatenate([ch["wm"], ch["rt"]], axis=0), bd(s))
            ch["u"] = ch["u0"] - pr[:CHUNK]
            ch["rs"] = pr[CHUNK:]
            ch["ut"] = ch["u0t"] - _dot_nt(s, bd(ch["wm"]))
        for ch in cur:
            y = ch["rs"] + _dot(ch["rbk"], jnp.concatenate([bd(ch["u"]), bd(ch["v"])], axis=0))
            ch["y_ref"][ch["bi"], ch["rows"], ch["sl"]] = y.astype(ch["y_ref"].dtype)
            state[(ch["bi"], ch["di"], ch["hp"])] = (ch["s"] * ch["g_tot"] + ch["kv"]
                                                      + _dot(ch["ut"], bd(ch["bh"])))
    for key, s in state.items():
        s_scr[key] = s

    @pl.when(c == pl.num_programs(1) - 1)
    def _():
        sfin_ref[...] = s_scr[...]


RW_STATE_SHAPE = (2, RW_HEADS // 2, RW_HEAD_DIM, 2 * RW_HEAD_DIM)


def _rwscan(r, v, kk, lw, kd, bd, s0):
    bsz, t, w = r.shape
    nsub = RW_SCAN_CHUNKS if (t // CHUNK) % RW_SCAN_CHUNKS == 0 else 1
    tb = nsub * CHUNK
    nc = t // tb
    nr = RW_SCAN_ROWS if bsz % RW_SCAN_ROWS == 0 else 1
    tok_f = pl.BlockSpec((nr, tb, w), lambda b, c: (b, c, 0))
    tok_b = pl.BlockSpec((nr, tb, w), lambda b, c: (b, nc - 1 - c, 0))
    dir_f = pl.BlockSpec((None, nr, tb, w), lambda b, c: (0, b, c, 0))
    dir_b = pl.BlockSpec((None, nr, tb, w), lambda b, c: (1, b, nc - 1 - c, 0))
    st = pl.BlockSpec((nr,) + RW_STATE_SHAPE, lambda b, c: (b, 0, 0, 0, 0))
    y_shape = jax.ShapeDtypeStruct((bsz, t, w), BF16)
    yf, yb, sfin = pl.pallas_call(
        _rwscan_kernel,
        grid=(bsz // nr, nc),
        in_specs=[tok_f, tok_f, tok_f, dir_f, dir_f, dir_f,
                  tok_b, tok_b, tok_b, dir_b, dir_b, dir_b, st],
        out_specs=[tok_f, tok_b, st],
        out_shape=[y_shape, y_shape, jax.ShapeDtypeStruct(s0.shape, F32)],
        scratch_shapes=[pltpu.VMEM((nr,) + RW_STATE_SHAPE, F32)],
        compiler_params=_params(2),
        name="rwscan",
    )(r, v, kk, lw, kd, bd, r, v, kk, lw, kd, bd, s0)
    return yf, yb, sfin


def _rw_output(yf, yb, g, gb, gn, hind):
    y = yf.astype(F32) + yb.astype(F32)
    inv_n = 1.0 / RW_HEAD_DIM
    mean = _head_sums(y, hind) * inv_n
    yc = y - mean
    var = _head_sums(yc * yc, hind) * inv_n
    yn = yc * lax.rsqrt(var + RW_GN_EPS) * gn[0:1, :] + gn[1:2, :]
    return yn * g.astype(F32) + gb.astype(F32)


GLA_STATE_SHAPE = (GLA_HEADS, GLA_DV, GLA_DK)


def _gla_block_scan(q, k, v, la, s_scr, nb, reverse, emit):
    row = lax.broadcasted_iota(jnp.int32, (CHUNK, CHUNK), 0)
    col = lax.broadcasted_iota(jnp.int32, (CHUNK, CHUNK), 1)
    keep = _before(row, col, reverse, True)
    chains = []
    for ci in (reversed(range(nb)) if reverse else range(nb)):
        rows = slice(ci * CHUNK, (ci + 1) * CHUNK)
        cum, tot = _cum_parts(la[rows, :], reverse)
        kc = k[rows, :].astype(F32)
        q_dec = (q[rows, :].astype(F32) * jnp.exp(cum)).astype(BF16)
        k_inv = (kc * jnp.exp(-cum)).astype(BF16)
        k_end = (kc * jnp.exp(tot - cum)).astype(BF16)
        dec = jnp.exp(tot)
        vc = v[rows, :]
        for h in range(GLA_HEADS):
            sk = slice(h * GLA_DK, (h + 1) * GLA_DK)
            sv = slice(h * GLA_DV, (h + 1) * GLA_DV)
            chains.append(dict(h=h, rows=rows, sv=sv, q=q_dec[:, sk], ki=k_inv[:, sk],
                               ke=k_end[:, sk], dec=dec[:, sk], v=vc[:, sv]))
    for ch in chains:
        ch["sc"] = jnp.where(keep, _dot_nt(ch["q"], ch["ki"]), 0.0)
        ch["kv"] = _dot_tn(ch["v"], ch["ke"])
    state = {}
    for ch in chains:
        s = state[ch["h"]] if ch["h"] in state else s_scr[ch["h"]]
        ch["s"] = s
        state[ch["h"]] = s * ch["dec"] + ch["kv"]
    for h, s in state.items():
        s_scr[h] = s
    for ch in chains:
        emit(ch["rows"], ch["sv"], _dot(ch["sc"], ch["v"]) + _dot_nt(ch["q"], ch["s"]))


def _glafwd_kernel(q_cur, q_prev, q_next, ald_ref, conv, up, bias, s0_ref,
                   o_q, o_k, o_v, o_la, o_of, sfin_ref, s_scr, *, nb):
    j = pl.program_id(1)

    @pl.when(j == 0)
    def _():
        s_scr[...] = s0_ref[...]

    x = q_cur[...].astype(F32)
    prev, nxt = _shifted(x, q_prev[...], q_next[...], j == 0, j == pl.num_programs(1) - 1)
    y = _silu(conv[0:1, :] * prev + conv[1:2, :] * x + conv[2:3, :] * nxt)
    q = (y[:, 0:GLA_KW] * (GLA_DK ** -0.5)).astype(BF16)
    k = y[:, GLA_KW:2 * GLA_KW].astype(BF16)
    v = y[:, 2 * GLA_KW:].astype(BF16)
    o_q[...] = q
    o_k[...] = k
    o_v[...] = v
    ald = ald_ref[...]
    log_alpha = lambda di: _log_sigmoid(_dot(ald, up[di]) + bias[di:di + 1, :]) / GLA_TAU
    o_la[...] = log_alpha(1)

    def emit(rows, lanes, o):
        o_of[rows, lanes] = o.astype(o_of.dtype)

    _gla_block_scan(q, k, v, log_alpha(0), s_scr, nb, False, emit)

    @pl.when(j == pl.num_programs(1) - 1)
    def _():
        sfin_ref[...] = s_scr[...]


def _glafwd(qkv, ald, s0, lp):
    bsz, t, fq = qkv.shape
    nb = min(GLA_BLOCK_CHUNKS, t // CHUNK)
    tb = nb * CHUNK
    assert t % tb == 0
    prev_spec, next_spec = _halo_specs(fq, tb, t)
    tok = lambda f: pl.BlockSpec((None, tb, f), lambda b, j: (b, j, 0))
    st = pl.BlockSpec((None,) + GLA_STATE_SHAPE, lambda b, j: (b, 0, 0, 0))
    small = [lp["gla_conv"], lp["gla_alpha_up_pad"], lp["gla_alpha_bias"]]
    act = lambda f, dt: jax.ShapeDtypeStruct((bsz, t, f), dt)
    return pl.pallas_call(
        functools.partial(_glafwd_kernel, nb=nb),
        grid=(bsz, t // tb),
        in_specs=[tok(fq), prev_spec, next_spec, tok(LANES)]
        + [_const_spec(s.shape) for s in small] + [st],
        out_specs=[tok(GLA_KW), tok(GLA_KW), tok(GLA_VW), tok(GLA_KW), tok(GLA_VW), st],
        out_shape=[act(GLA_KW, BF16), act(GLA_KW, BF16), act(GLA_VW, BF16), act(GLA_KW, F32),
                   act(GLA_VW, BF16), jax.ShapeDtypeStruct(s0.shape, F32)],
        scratch_shapes=[pltpu.VMEM(GLA_STATE_SHAPE, F32)],
        compiler_params=_params(2),
        name="glafwd",
    )(qkv, qkv, qkv, ald, *small, s0)


def _glabwd_kernel(q_ref, k_ref, v_ref, la_ref, of_ref, gate_ref, nw, s0_ref,
                   y_ref, sfin_ref, s_scr, o_scr, *, nb, y_tr, a, k):
    j = pl.program_id(1)

    @pl.when(j == 0)
    def _():
        s_scr[...] = s0_ref[...]

    def emit(rows, lanes, o):
        o_scr[rows, lanes] = o

    _gla_block_scan(q_ref, k_ref, v_ref, la_ref, s_scr, nb, True, emit)
    o = of_ref[...].astype(F32) + o_scr[...]
    gate = gate_ref[...].astype(F32)
    ys = []
    for h in range(GLA_HEADS):
        sv = slice(h * GLA_DV, (h + 1) * GLA_DV)
        ys.append(_rms(o[:, sv], nw[...]) * _silu(gate[:, sv]))
    _store_tok(y_ref, jnp.concatenate(ys, axis=1), y_tr, a, k)

    @pl.when(j == pl.num_programs(1) - 1)
    def _():
        sfin_ref[...] = s_scr[...]


def _glabwd(q, k, v, la, of, gate, s0, lp, *, y_tr, a):
    bsz, t, _ = q.shape
    w = GLA_VW
    tb = min(TOKEN_TILE, t)
    nb = tb // CHUNK
    nblk = t // tb
    kk = tb // a
    tok = lambda f: pl.BlockSpec((None, tb, f), lambda b, j: (b, nblk - 1 - j, 0))
    st = pl.BlockSpec((None,) + GLA_STATE_SHAPE, lambda b, j: (b, 0, 0, 0))
    if y_tr:
        y_spec = pl.BlockSpec((None, a, kk, w), lambda b, j: (b, 0, nblk - 1 - j, 0))
    else:
        y_spec = tok(w)
    y, sfin = pl.pallas_call(
        functools.partial(_glabwd_kernel, nb=nb, y_tr=y_tr, a=a, k=kk),
        grid=(bsz, nblk),
        in_specs=[tok(GLA_KW), tok(GLA_KW), tok(w), tok(GLA_KW), tok(w), tok(w),
                  _const_spec((1, GLA_DV)), st],
        out_specs=[y_spec, st],
        out_shape=[_tok_shape(bsz, t, w, y_tr, a, BF16), jax.ShapeDtypeStruct(s0.shape, F32)],
        scratch_shapes=[pltpu.VMEM(GLA_STATE_SHAPE, F32), pltpu.VMEM((tb, w), F32)],
        compiler_params=_params(2),
        name="glabwd",
    )(q, k, v, la, of, gate, lp["gla_norm_w"], s0)
    return _tok_unview(y, y_tr, t, w), sfin


def _merge_kernel(x_ref, yf_ref, yb_ref, g_ref, gb_ref, ygla_ref, mg_ref, mod_ref, gains, gn, hind,
                  w_rwo, w_glao, w_mo, w1, w2, o_ref, *, x_tr, a, k):
    d = x_ref.shape[-1]
    x = _load_tok(x_ref, x_tr, a, k)
    mg = mg_ref[...].astype(F32)
    yrw = _rw_output(yf_ref[...], yb_ref[...], g_ref[...], gb_ref[...], gn, hind[...])
    br = (_sigmoid(mg[:, :d]) * _dot(yrw, w_rwo[...])
          + _sigmoid(mg[:, d:]) * _dot(ygla_ref[...], w_glao[...]))
    m = _dot(br, w_mo[...])
    x1 = x + mod_ref[2:3, :] * _rms(m, gains[1:2, :])
    h2 = _rms(x1, gains[2:3, :]) * (1.0 + mod_ref[4:5, :]) + mod_ref[3:4, :]
    hid = jnp.maximum(_dot(h2, w1[...]), 0.0)
    f = _dot(hid * hid, w2[...])
    x2 = x1 + mod_ref[5:6, :] * _rms(f, gains[3:4, :])
    _store_tok(o_ref, x2, x_tr, a, k)


def _merge(x, rw_parts, ygla, mg, mods, mod_row, gains, gn, hind, ws, *, x_tr, a, tm):
    bsz, t, d = x.shape
    k = tm // a
    kern = functools.partial(_merge_kernel, x_tr=x_tr, a=a, k=k)
    rw_tok = _tok_spec(RW_WIDTH, tm, False, a)
    out = pl.pallas_call(
        kern,
        grid=(bsz, t // tm),
        in_specs=[_tok_spec(d, tm, x_tr, a), rw_tok, rw_tok, rw_tok, rw_tok,
                  _tok_spec(GLA_VW, tm, False, a), _tok_spec(2 * d, tm, False, a),
                  pl.BlockSpec((None, 6, d), lambda b, j: (mod_row(b), 0, 0)),
                  _const_spec(gains.shape), _const_spec(gn.shape), _const_spec(hind.shape)]
        + [_wspec(w) for w in ws],
        out_specs=_tok_spec(d, tm, x_tr, a),
        out_shape=_tok_shape(bsz, t, d, x_tr, a, F32),
        compiler_params=_params(2),
        name="merge",
    )(_tok_view(x, x_tr, a), *rw_parts, ygla, mg, mods, gains, gn, hind, *[_warg(w) for w in ws])
    return _tok_unview(out, x_tr, t, d)


def _stacked_weights(p):
    w_in = p["w_in"]
    g0 = RW_COLS
    g1 = g0 + GLA_QKV_W
    g2 = g1 + 2 * GLA_GATE_RANK
    g3 = g2 + GLA_VW
    bf = lambda w: w.astype(BF16)
    return {
        "w_in_parts": [
            bf(jnp.concatenate([w_in[:, :, :g0], jnp.pad(
                w_in[:, :, g1:g2], ((0, 0), (0, 0), (0, LANES - 2 * GLA_GATE_RANK)))], axis=2)),
            bf(w_in[:, :, g0:g1]),
            bf(w_in[:, :, g2:g3]),
            bf(w_in[:, :, g3:]),
        ],
        "merge_ws": [bf(p[name]) for name in ("rw_out", "gla_out", "merge_out", "mlp_w1", "mlp_w2")],
    }


def _layer_params(l, p, stacked):
    bf = lambda w: w.astype(BF16)
    row = lambda w: w.reshape(1, -1)
    lp = {
        "w_in_parts": [_Stacked(w, l) for w in stacked["w_in_parts"]],
        "rw_mu": row(p["rw_mu"][l]),
        "rw_w0": p["rw_w0"][l],
        "rw_a0": p["rw_a0"][l],
        "rw_g_up": bf(p["rw_g_up"][l]),
        "rw_k_k": row(p["rw_k_k"][l]),
        "rw_k_a": row(p["rw_k_a"][l]),
        "rw_r_k": row(p["rw_r_k"][l]),
        "rw_gn": jnp.stack([p["rw_gn_w"][l], p["rw_gn_b"][l]]),
        "gla_conv": p["gla_conv"][l],
        "gla_alpha_bias": p["gla_alpha_bias"][l],
        "gla_norm_w": row(p["gla_norm_w"][l]),
        "merge_ws": [_Stacked(w, l) for w in stacked["merge_ws"]],
        "gains": jnp.stack([p["norm_mix_pre"][l], p["norm_mix_post"][l],
                            p["norm_ffn_pre"][l], p["norm_ffn_post"][l]]),
    }
    pad_dir = lambda w, r: jnp.stack([jnp.pad(w[di], ((di * r, LANES - (di + 1) * r), (0, 0)))
                                      for di in range(2)])
    lp["rw_w_up_pad"] = bf(pad_dir(p["rw_w_up"][l], p["rw_w_up"].shape[2]))
    lp["rw_a_up_pad"] = bf(pad_dir(p["rw_a_up"][l], p["rw_a_up"].shape[2]))
    lp["gla_alpha_up_pad"] = bf(pad_dir(p["gla_alpha_up"][l], GLA_GATE_RANK))
    if l > 0:
        lp["rw_vres_down"] = bf(p["rw_vres_down"][l - 1])
        lp["rw_vres_up"] = bf(p["rw_vres_up"][l - 1])
        lp["rw_vres_bias"] = row(p["rw_vres_bias"][l - 1])
    return lp


def _head_indicator():
    h = jnp.arange(RW_WIDTH) // RW_HEAD_DIM
    return (h[:, None] == h[None, :]).astype(BF16)


def _mixer(x, mods, mod_row, vfirst, s_rw, s_gla, lp, hind, *, p_col, need_out):
    bsz, t, d = x.shape
    tm_merge = min(MERGE_TILE, t)
    if p_col is None:
        x_tr, g_tr, a_p, a_q = False, False, SUBLANES, SUBLANES
    else:
        x_tr, g_tr = p_col, True
        rows = t // GRID_W
        a_p, a_q = (rows, GRID_W) if p_col else (GRID_W, rows)
    (gq, gald, gg), mg, (r, v, kk, g, gb, lw, kd, bd) = _inproj(
        x, vfirst, mods, mod_row, lp, hind, x_tr=x_tr, g_tr=g_tr, a=a_p, tm=min(PROJ_TILE, t))
    yf, yb, s_rw_out = _rwscan(r, v, kk, lw, kd, bd, s_rw)
    q, k, gv, la_b, of, s_gla_f = _glafwd(gq, gald, s_gla[0], lp)
    ygla, s_gla_b = _glabwd(q, k, gv, la_b, of, gg, s_gla[1], lp, y_tr=g_tr, a=a_q)
    s_gla_out = (s_gla_f, s_gla_b)
    if not need_out:
        return None, v, s_rw_out, s_gla_out
    x_new = _merge(x, (yf, yb, g, gb), ygla, mg, mods, mod_row, lp["gains"], lp["rw_gn"], hind,
                   lp["merge_ws"], x_tr=x_tr, a=a_p, tm=tm_merge)
    return x_new, v, s_rw_out, s_gla_out


def kernel(x, c, ctx, c_ctx, w_in, rw_mu, rw_w0, rw_w_up, rw_a0, rw_a_up, rw_g_up, rw_k_k, rw_k_a,
           rw_r_k, rw_gn_w, rw_gn_b, rw_vres_down, rw_vres_up, rw_vres_bias, rw_out, gla_conv,
           gla_alpha_up, gla_alpha_bias, gla_norm_w, gla_out, merge_out, mlp_w1, mlp_w2, ada_w,
           ada_b, norm_mix_pre, norm_mix_post, norm_ffn_pre, norm_ffn_post):
    p = dict(w_in=w_in, rw_mu=rw_mu, rw_w0=rw_w0, rw_w_up=rw_w_up, rw_a0=rw_a0, rw_a_up=rw_a_up,
             rw_g_up=rw_g_up, rw_k_k=rw_k_k, rw_k_a=rw_k_a, rw_r_k=rw_r_k.reshape(rw_r_k.shape[0], -1),
             rw_gn_w=rw_gn_w, rw_gn_b=rw_gn_b, rw_vres_down=rw_vres_down, rw_vres_up=rw_vres_up,
             rw_vres_bias=rw_vres_bias, rw_out=rw_out, gla_conv=gla_conv, gla_alpha_up=gla_alpha_up,
             gla_alpha_bias=gla_alpha_bias, gla_norm_w=gla_norm_w, gla_out=gla_out,
             merge_out=merge_out, mlp_w1=mlp_w1, mlp_w2=mlp_w2, norm_mix_pre=norm_mix_pre,
             norm_mix_post=norm_mix_post, norm_ffn_pre=norm_ffn_pre, norm_ffn_post=norm_ffn_post)
    bsz, t, d = x.shape
    t_ctx = ctx.shape[1]
    depth = w_in.shape[0]
    assert bsz < SUBLANES and d % LANES == 0
    assert t % (GRID_W * SUBLANES) == 0 and t_ctx % CHUNK == 0
    assert t % min(TOKEN_TILE, t) == 0 and t_ctx % min(TOKEN_TILE, t_ctx) == 0

    cc = jnp.concatenate([c, c_ctx[None, :], jnp.zeros((SUBLANES - 1 - bsz, d), F32)], axis=0)
    mods = _ada_mods(cc, ada_w, ada_b).reshape(depth, SUBLANES, 6, d)
    hind = _head_indicator()
    stacked = _stacked_weights(p)
    lat_row = lambda b: b
    ctx_row = lambda b: bsz

    x_lat, x_ctx = x, ctx
    vf_lat = vf_ctx = None
    for l in range(depth):
        last = l == depth - 1
        lp = _layer_params(l, p, stacked)
        z_rw = jnp.zeros((bsz,) + RW_STATE_SHAPE, F32)
        z_gla = (jnp.zeros((bsz,) + GLA_STATE_SHAPE, F32),) * 2
        x_ctx_new, v_ctx, s_rw, s_gla = _mixer(
            x_ctx, mods[l], ctx_row, vf_ctx, z_rw, z_gla, lp, hind, p_col=None, need_out=not last)
        x_lat, v_lat, _, _ = _mixer(
            x_lat, mods[l], lat_row, vf_lat, s_rw, s_gla, lp, hind,
            p_col=(l % 2 == 1), need_out=True)
        if l == 0:
            vf_lat, vf_ctx = v_lat, v_ctx
        if not last:
            x_ctx = x_ctx_new
    return x_lat
```

```python
import functools
from typing import NamedTuple

import jax
import jax.numpy as jnp
from jax import lax
from jax.experimental import pallas as pl
from jax.experimental.pallas import tpu as pltpu

F32 = jnp.float32
BF16 = jnp.bfloat16

GRID_W = 64
RMS_EPS = 1e-6
RW_HEADS = 8
RW_HEAD_DIM = 64
RW_WIDTH = RW_HEADS * RW_HEAD_DIM
RW_GN_EPS = 64e-5
RW_COLS = 1920
GLA_HEADS = 4
GLA_DK = 64
GLA_DV = 128
GLA_KW = GLA_HEADS * GLA_DK
GLA_VW = GLA_HEADS * GLA_DV
GLA_QKV_W = 2 * GLA_KW + GLA_VW
GLA_GATE_RANK = 16
GLA_TAU = 16.0
CHUNK = 64
GLA_BLOCK_CHUNKS = 16
RW_SCAN_ROWS = 4
RW_SCAN_CHUNKS = 2
LANES = 128
SUBLANES = 8
HALO = 16
VMEM_LIMIT = 56 * 1024 * 1024
TOKEN_TILE = 1024
PROJ_TILE = 512
MERGE_TILE = 512
MLP_CHUNKS = 4
GLA_IN_DTYPE = F32


def _dot(a, b):
    return jnp.dot(a.astype(BF16), b.astype(BF16), preferred_element_type=F32)


def _dot_nt(a, b):
    return lax.dot_general(a.astype(BF16), b.astype(BF16), (((1,), (1,)), ((), ())),
                           preferred_element_type=F32)


def _dot_tn(a, b):
    return lax.dot_general(a.astype(BF16), b.astype(BF16), (((0,), (0,)), ((), ())),
                           preferred_element_type=F32)


def _split2(x):
    hi = x.astype(BF16)
    lo = (x - hi.astype(F32)).astype(BF16)
    return hi, lo


def _head_sums(x, ind):
    return jnp.dot(x.astype(BF16), ind, preferred_element_type=F32)


def _dot_ind_lhs(ind, x):
    hi, lo = _split2(x)
    return (jnp.dot(ind, hi, preferred_element_type=F32)
            + jnp.dot(ind, lo, preferred_element_type=F32))


def _sigmoid(x):
    return jax.nn.sigmoid(x)


def _silu(x):
    return x * jax.nn.sigmoid(x)


def _softplus(z):
    return jnp.maximum(z, 0.0) + jnp.log(1.0 + jnp.exp(-jnp.abs(z)))


def _log_sigmoid(z):
    return -_softplus(-z)


def _rms(x, gain):
    return x * lax.rsqrt(jnp.mean(x * x, axis=-1, keepdims=True) + RMS_EPS) * gain


def _load_tok(ref, transposed, a, k):
    if not transposed:
        return ref[...]
    return jnp.concatenate([ref[:, i, :] for i in range(k)], axis=0)


def _store_tok(ref, val, transposed, a, k):
    val = val.astype(ref.dtype)
    if not transposed:
        ref[...] = val
    else:
        for i in range(k):
            ref[:, i, :] = val[i * a:(i + 1) * a, :]


def _tok_view(arr, transposed, a):
    if not transposed:
        return arr
    b, t, f = arr.shape
    return arr.reshape(b, a, t // a, f)


def _tok_unview(arr, transposed, t, f):
    if not transposed:
        return arr
    return arr.reshape(arr.shape[0], t, f)


def _tok_spec(f, tm, transposed, a):
    if not transposed:
        return pl.BlockSpec((None, tm, f), lambda b, j: (b, j, 0))
    return pl.BlockSpec((None, a, tm // a, f), lambda b, j: (b, 0, j, 0))


def _tok_shape(bsz, t, f, transposed, a, dtype):
    if not transposed:
        return jax.ShapeDtypeStruct((bsz, t, f), dtype)
    return jax.ShapeDtypeStruct((bsz, a, t // a, f), dtype)


def _const_spec(shape):
    nd = len(shape)
    return pl.BlockSpec(shape, lambda *_: (0,) * nd, pipeline_mode=pl.Buffered(1))


class _Stacked(NamedTuple):
    arr: jax.Array
    layer: int


def _warg(w):
    return w.arr if isinstance(w, _Stacked) else w


def _wspec(w):
    if not isinstance(w, _Stacked):
        return _const_spec(w.shape)
    shape = w.arr.shape[1:]
    index = (w.layer,) + (0,) * len(shape)
    return pl.BlockSpec((None,) + shape, lambda *_: index, pipeline_mode=pl.Buffered(1))


def _params(ndim):
    return pltpu.CompilerParams(dimension_semantics=("arbitrary",) * ndim,
                                vmem_limit_bytes=VMEM_LIMIT)


def _halo_specs(f, tm, t):
    sub = tm // HALO
    last = t // HALO - 1
    prev = pl.BlockSpec((None, HALO, f), lambda b, j: (b, jnp.maximum(j * sub - 1, 0), 0))
    nxt = pl.BlockSpec((None, HALO, f), lambda b, j: (b, jnp.minimum((j + 1) * sub, last), 0))
    return prev, nxt


def _shifted(cur, prev_blk, next_blk, first, last):
    tm = cur.shape[0]
    keep_prev = jnp.where(first, 0.0, 1.0)
    keep_next = jnp.where(last, 0.0, 1.0)
    ext = jnp.concatenate([cur, next_blk.astype(F32) * keep_next,
                           prev_blk.astype(F32) * keep_prev], axis=0)
    n = ext.shape[0]
    return pltpu.roll(ext, 1, 0)[0:tm], pltpu.roll(ext, n - 1, 0)[0:tm]


def _ada_kernel(c_ref, w_ref, b_ref, o_ref):
    cc = c_ref[...]
    o_ref[...] = _dot(_silu(cc), w_ref[...]) + b_ref[...]


def _ada_mods(cc, ada_w, ada_b):
    nl, d, n6 = ada_w.shape
    tn = d
    return pl.pallas_call(
        _ada_kernel,
        grid=(nl, n6 // tn),
        in_specs=[pl.BlockSpec((SUBLANES, d), lambda l, n: (0, 0)),
                  pl.BlockSpec((None, d, tn), lambda l, n: (l, 0, n)),
                  pl.BlockSpec((None, 1, tn), lambda l, n: (l, 0, n))],
        out_specs=pl.BlockSpec((None, SUBLANES, tn), lambda l, n: (l, 0, n)),
        out_shape=jax.ShapeDtypeStruct((nl, SUBLANES, n6), F32),
        compiler_params=_params(2),
        name="ada",
    )(cc, ada_w, ada_b.reshape(nl, 1, n6))


def _inproj_kernel(*refs, has_vres, x_tr, g_tr, a, k):
    it = iter(refs)
    x_ref, xp_ref, xn_ref = next(it), next(it), next(it)
    vf_ref = next(it) if has_vres else None
    mod_ref, gain_ref, w_rw, w_q, w_gg, w_mg = (next(it) for _ in range(6))
    mu, w0, w_up, a0, a_up, g_up, k_k, k_a, r_k = (next(it) for _ in range(9))
    if has_vres:
        vdown, vup, vbias = next(it), next(it), next(it)
    hind = next(it)
    o_q, o_ald, o_gg, o_mg = (next(it) for _ in range(4))
    o_r, o_v, o_kk, o_g, o_gb, o_lw, o_kd, o_bd = (next(it) for _ in range(8))

    j = pl.program_id(1)
    keep_prev = jnp.where(j == 0, 0.0, 1.0)
    keep_next = jnp.where(j == pl.num_programs(1) - 1, 0.0, 1.0)
    x = _load_tok(x_ref, x_tr, a, k)
    tm = x.shape[0]
    xp = xp_ref[SUBLANES - 1] if x_tr else xp_ref[...]
    xn = xn_ref[0] if x_tr else xn_ref[...]
    modulate = lambda z: _rms(z, gain_ref[...]) * (1.0 + mod_ref[1:2, :]) + mod_ref[0:1, :]
    h = modulate(x)
    hb = h.astype(BF16)
    h_ext = jnp.concatenate([modulate(xp) * keep_prev, h, modulate(xn) * keep_next], axis=0)
    f_ext = jnp.dot(h_ext.astype(BF16), w_rw[...], preferred_element_type=F32)
    proj = lambda w: jnp.dot(hb, w[...], preferred_element_type=F32)
    _store_tok(o_mg, proj(w_mg), False, a, k)
    _store_tok(o_q, proj(w_q), g_tr, a, k)
    _store_tok(o_ald, f_ext[SUBLANES:SUBLANES + tm, RW_COLS:], g_tr, a, k)
    _store_tok(o_gg, proj(w_gg), g_tr, a, k)

    f = f_ext[SUBLANES:SUBLANES + tm, :RW_COLS]
    nbr = (f_ext[SUBLANES - 1:SUBLANES - 1 + tm, :RW_COLS]
           + f_ext[SUBLANES + 1:SUBLANES + 1 + tm, :RW_COLS])
    fs = f + mu[...] * (0.5 * nbr - f)
    w = RW_WIDTH
    r = fs[:, 0:w]
    kx = fs[:, w:2 * w]
    v = fs[:, 2 * w:3 * w]
    wd = fs[:, 3 * w:3 * w + LANES]
    ad = fs[:, 3 * w + LANES:3 * w + 2 * LANES]
    gd = fs[:, 3 * w + 2 * LANES:3 * w + 3 * LANES]
    if has_vres:
        vf = _load_tok(vf_ref, x_tr, a, k).astype(F32)
        mix = _sigmoid(vbias[...] + _dot(_dot(v, vdown[...]), vup[...]))
        v = v + (vf - v) * mix
    kk = kx * k_k[...]
    kk = kk * lax.rsqrt(_head_sums(kk * kk, hind[...]) + 1e-12)
    twd = jnp.tanh(wd)
    ksum = None
    for di in range(2):
        wlog = -_softplus(-(w0[di:di + 1, :] + _dot(twd, w_up[di]))) - 0.5
        o_lw[di] = -jnp.exp(wlog)
        lr = _sigmoid(a0[di:di + 1, :] + _dot(ad, a_up[di]))
        kd = kx * (1.0 + (lr - 1.0) * k_a[...])
        o_kd[di] = kd.astype(o_kd.dtype)
        o_bd[di] = (lr * kk).astype(o_bd.dtype)
        ksum = kd if ksum is None else ksum + kd
    g = _dot(_sigmoid(gd), g_up[...])
    bonus = _head_sums(r * ksum * r_k[...], hind[...]) * v
    o_r[...] = r.astype(o_r.dtype)
    o_v[...] = v.astype(o_v.dtype)
    o_kk[...] = kk.astype(o_kk.dtype)
    o_g[...] = g.astype(o_g.dtype)
    o_gb[...] = (bonus * g).astype(o_gb.dtype)


def _neighbour_specs(d, tm, t, x_tr, a):
    sub = tm // SUBLANES
    if not x_tr:
        last = t // SUBLANES - 1
        prev = pl.BlockSpec((None, SUBLANES, d), lambda b, j: (b, jnp.maximum(j * sub - 1, 0), 0))
        nxt = pl.BlockSpec((None, SUBLANES, d), lambda b, j: (b, jnp.minimum((j + 1) * sub, last), 0))
        return prev, nxt
    kb = tm // a // SUBLANES
    last = t // a // SUBLANES - 1
    shape = (None, SUBLANES, SUBLANES, d)
    prev = pl.BlockSpec(shape, lambda b, j: (b, a // SUBLANES - 1, jnp.maximum(j * kb - 1, 0), 0))
    nxt = pl.BlockSpec(shape, lambda b, j: (b, 0, jnp.minimum((j + 1) * kb, last), 0))
    return prev, nxt


def _inproj(x, vfirst, mods, mod_row, lp, hind, *, x_tr, g_tr, a, tm):
    bsz, t, d = x.shape
    w = RW_WIDTH
    k = tm // a
    has_vres = vfirst is not None
    kern = functools.partial(_inproj_kernel, has_vres=has_vres, x_tr=x_tr, g_tr=g_tr, a=a, k=k)
    xv = _tok_view(x, x_tr, a)
    prev_spec, next_spec = _neighbour_specs(d, tm, t, x_tr, a)
    args = [xv, xv, xv]
    in_specs = [_tok_spec(d, tm, x_tr, a), prev_spec, next_spec]
    if has_vres:
        args.append(_tok_view(vfirst, x_tr, a))
        in_specs.append(_tok_spec(w, tm, x_tr, a))
    args.append(mods)
    in_specs.append(pl.BlockSpec((None, 6, d), lambda b, j: (mod_row(b), 0, 0)))
    small = [lp["gains"][0:1]] + lp["w_in_parts"] + [
        lp["rw_mu"], lp["rw_w0"], lp["rw_w_up_pad"], lp["rw_a0"], lp["rw_a_up_pad"],
        lp["rw_g_up"], lp["rw_k_k"], lp["rw_k_a"], lp["rw_r_k"]]
    if has_vres:
        small += [lp["rw_vres_down"], lp["rw_vres_up"], lp["rw_vres_bias"]]
    small.append(hind)
    args += [_warg(s) for s in small]
    in_specs += [_wspec(s) for s in small]
    gla_widths = (GLA_QKV_W, LANES, GLA_VW)
    tok = pl.BlockSpec((None, tm, w), lambda b, j: (b, j, 0))
    tok2 = pl.BlockSpec((2, None, tm, w), lambda b, j: (0, b, j, 0))
    s1 = jax.ShapeDtypeStruct((bsz, t, w), BF16)
    s2 = lambda dt: jax.ShapeDtypeStruct((2, bsz, t, w), dt)
    outs = pl.pallas_call(
        kern,
        grid=(bsz, t // tm),
        in_specs=in_specs,
        out_specs=[_tok_spec(f, tm, g_tr, a) for f in gla_widths]
        + [_tok_spec(2 * d, tm, False, a)] + [tok] * 5 + [tok2] * 3,
        out_shape=[_tok_shape(bsz, t, f, g_tr, a, GLA_IN_DTYPE) for f in gla_widths]
        + [_tok_shape(bsz, t, 2 * d, False, a, BF16)] + [s1] * 5 + [s2(F32), s2(BF16), s2(BF16)],
        compiler_params=_params(2),
        name="inproj",
    )(*args)
    gla = [_tok_unview(o, g_tr, t, f) for o, f in zip(outs[:3], gla_widths)]
    return gla, outs[3], outs[4:]


def _before(row, col, reverse, inclusive):
    if reverse:
        return (col >= row) if inclusive else (col > row)
    return (col <= row) if inclusive else (col < row)


def _cum_parts(lw, reverse):
    row = lax.broadcasted_iota(jnp.int32, (CHUNK, CHUNK), 0)
    col = lax.broadcasted_iota(jnp.int32, (CHUNK, CHUNK), 1)
    tri = jnp.where(_before(row, col, reverse, True), 1.0, 0.0).astype(BF16)
    cum = _dot_ind_lhs(tri, lw)
    tot = cum[0:1, :] if reverse else cum[CHUNK - 1:CHUNK, :]
    return cum, tot


def _rwscan_kernel(rf, vf, kkf, lwf, kdf, bdf, rb, vb, kkb, lwb, kdb, bdb, s0_ref,
                   yf_ref, yb_ref, sfin_ref, s_scr):
    c = pl.program_id(1)

    @pl.when(c == 0)
    def _():
        s_scr[...] = s0_ref[...]

    n = RW_HEAD_DIM
    pw = 2 * n
    lane = lax.broadcasted_iota(jnp.int32, (CHUNK, pw), 1)
    lo = lane < n
    row = lax.broadcasted_iota(jnp.int32, (CHUNK, pw), 0)
    eye2 = jnp.where(row == lane % n, 1.0, 0.0)
    grow = lax.broadcasted_iota(jnp.int32, (2 * CHUNK, 2 * pw), 0)
    gcol = lax.broadcasted_iota(jnp.int32, (2 * CHUNK, 2 * pw), 1)
    rp = grow % CHUNK
    cp = gcol % CHUNK
    bottom = grow // CHUNK

    def halves(z):
        zb = z.astype(BF16)
        zero = jnp.zeros_like(zb)
        return jnp.where(lo, zb, zero), jnp.where(lo, zero, zb)

    def bd(z):
        return jnp.concatenate(halves(z), axis=0)

    def pick(z):
        return jnp.where(lo, z[:n], z[n:])

    chains = []
    dirs = ((rf, vf, kkf, lwf, kdf, bdf, yf_ref), (rb, vb, kkb, lwb, kdb, bdb, yb_ref))
    nsub = rf.shape[1] // CHUNK
    for bi in range(s_scr.shape[0]):
        for di, (r_ref, v_ref, kk_ref, lw_ref, kd_ref, bd_ref, y_ref) in enumerate(dirs):
            reverse = di == 1
            order = list(reversed(range(nsub))) if reverse else list(range(nsub))
            for pos, ci in enumerate(order):
                rows = slice(ci * CHUNK, (ci + 1) * CHUNK)
                lw = lw_ref[bi, rows, :]
                cum, tot = _cum_parts(lw, reverse)
                e_neg = jnp.exp(-cum)
                e_end = jnp.exp(tot - cum)
                kd = kd_ref[bi, rows, :].astype(F32)
                bdv = bd_ref[bi, rows, :].astype(F32)
                ops = dict(
                    rt=(r_ref[bi, rows, :].astype(F32) * jnp.exp(cum)).astype(BF16),
                    at=(kk_ref[bi, rows, :].astype(F32) * jnp.exp(cum - lw)).astype(BF16),
                    kt=(kd * e_neg).astype(BF16),
                    bt=(bdv * e_neg).astype(BF16),
                    kh=(kd * e_end).astype(BF16),
                    bh=(bdv * e_end).astype(BF16),
                    v=v_ref[bi, rows, :],
                    g_tot=jnp.exp(tot),
                )
                keep = (cp > rp - bottom) if reverse else (cp < rp + bottom)
                for hp in range(RW_HEADS // 2):
                    sl = slice(hp * pw, (hp + 1) * pw)
                    ch = {k: val[:, sl] for k, val in ops.items()}
                    ch.update(bi=bi, di=di, hp=hp, pos=pos, rows=rows, sl=sl, keep=keep,
                              y_ref=y_ref)
                    chains.append(ch)

    for ch in chains:
        g = _dot_nt(jnp.concatenate([ch["at"], ch["rt"]], axis=0),
                    jnp.concatenate(halves(ch["bt"]) + halves(ch["kt"]), axis=0))
        g = jnp.where(ch["keep"], g, 0.0).astype(BF16)
        ch["l"] = g[:CHUNK, :pw]
        ch["ak"] = g[:CHUNK, pw:]
        ch["rbk"] = g[CHUNK:, :]
    for ch in chains:
        ch["p"] = _dot(ch["l"], bd(ch["l"]))
        ch["av"] = _dot(ch["ak"], bd(ch["v"]))
        ch["kv"] = pick(_dot_tn(ch["v"], ch["kh"]))
        ch["x"] = eye2 - ch["l"].astype(F32)
    for _ in range(4):
        for ch in chains:
            z = _dot(jnp.concatenate([ch["x"], ch["p"]], axis=0), bd(ch["p"]))
            ch["x"] = ch["x"] + z[:CHUNK]
            ch["p"] = z[CHUNK:]
    for ch in chains:
        ch["x"] = ch["x"] + _dot(ch["x"], bd(ch["p"]))
    for ch in chains:
        wu = _dot(ch["x"], jnp.concatenate([bd(ch["at"]), bd(ch["av"])], axis=1))
        ch["wm"] = wu[:, :pw].astype(BF16)
        ch["u0"] = -wu[:, pw:]
    for ch in chains:
        t = ch["u0"].T
        ch["u0t"] = jnp.concatenate([t[:n], t[n:]], axis=1)
    state = {}
    for pos in range(nsub):
        cur = [ch for ch in chains if ch["pos"] == pos]
        for ch in cur:
            key = (ch["bi"], ch["di"], ch["hp"])
            s = state[key] if key in state else s_scr[key]
            ch["s"] = s
            pr = _dot_nt(jnp.concatenate([ch["wm"], ch["rt"]], axis=0), bd(s))
            ch["u"] = ch["u0"] - pr[:CHUNK]
            ch["rs"] = pr[CHUNK:]
            ch["ut"] = ch["u0t"] - _dot_nt(s, bd(ch["wm"]))
        for ch in cur:
            y = ch["rs"] + _dot(ch["rbk"], jnp.concatenate([bd(ch["u"]), bd(ch["v"])], axis=0))
            ch["y_ref"][ch["bi"], ch["rows"], ch["sl"]] = y.astype(ch["y_ref"].dtype)
            state[(ch["bi"], ch["di"], ch["hp"])] = (ch["s"] * ch["g_tot"] + ch["kv"]
                                                      + _dot(ch["ut"], bd(ch["bh"])))
    for key, s in state.items():
        s_scr[key] = s

    @pl.when(c == pl.num_programs(1) - 1)
    def _():
        sfin_ref[...] = s_scr[...]


RW_STATE_SHAPE = (2, RW_HEADS // 2, RW_HEAD_DIM, 2 * RW_HEAD_DIM)


def _rwscan(r, v, kk, lw, kd, bd, s0):
    bsz, t, w = r.shape
    nsub = RW_SCAN_CHUNKS if (t // CHUNK) % RW_SCAN_CHUNKS == 0 else 1
    tb = nsub * CHUNK
    nc = t // tb
    nr = RW_SCAN_ROWS if bsz % RW_SCAN_ROWS == 0 else 1
    tok_f = pl.BlockSpec((nr, tb, w), lambda b, c: (b, c, 0))
    tok_b = pl.BlockSpec((nr, tb, w), lambda b, c: (b, nc - 1 - c, 0))
    dir_f = pl.BlockSpec((None, nr, tb, w), lambda b, c: (0, b, c, 0))
    dir_b = pl.BlockSpec((None, nr, tb, w), lambda b, c: (1, b, nc - 1 - c, 0))
    st = pl.BlockSpec((nr,) + RW_STATE_SHAPE, lambda b, c: (b, 0, 0, 0, 0))
    y_shape = jax.ShapeDtypeStruct((bsz, t, w), BF16)
    yf, yb, sfin = pl.pallas_call(
        _rwscan_kernel,
        grid=(bsz // nr, nc),
        in_specs=[tok_f, tok_f, tok_f, dir_f, dir_f, dir_f,
                  tok_b, tok_b, tok_b, dir_b, dir_b, dir_b, st],
        out_specs=[tok_f, tok_b, st],
        out_shape=[y_shape, y_shape, jax.ShapeDtypeStruct(s0.shape, F32)],
        scratch_shapes=[pltpu.VMEM((nr,) + RW_STATE_SHAPE, F32)],
        compiler_params=_params(2),
        name="rwscan",
    )(r, v, kk, lw, kd, bd, r, v, kk, lw, kd, bd, s0)
    return yf, yb, sfin


def _rw_output(yf, yb, g, gb, gn, hind):
    y = yf.astype(F32) + yb.astype(F32)
    inv_n = 1.0 / RW_HEAD_DIM
    mean = _head_sums(y, hind) * inv_n
    yc = y - mean
    var = _head_sums(yc * yc, hind) * inv_n
    yn = yc * lax.rsqrt(var + RW_GN_EPS) * gn[0:1, :] + gn[1:2, :]
    return yn * g.astype(F32) + gb.astype(F32)


GLA_STATE_SHAPE = (GLA_HEADS, GLA_DV, GLA_DK)


def _gla_block_scan(q, k, v, la, s_scr, nb, reverse, emit):
    row = lax.broadcasted_iota(jnp.int32, (CHUNK, CHUNK), 0)
    col = lax.broadcasted_iota(jnp.int32, (CHUNK, CHUNK), 1)
    keep = _before(row, col, reverse, True)
    chains = []
    for ci in (reversed(range(nb)) if reverse else range(nb)):
        rows = slice(ci * CHUNK, (ci + 1) * CHUNK)
        cum, tot = _cum_parts(la[rows, :], reverse)
        kc = k[rows, :].astype(F32)
        q_dec = (q[rows, :].astype(F32) * jnp.exp(cum)).astype(BF16)
        k_inv = (kc * jnp.exp(-cum)).astype(BF16)
        k_end = (kc * jnp.exp(tot - cum)).astype(BF16)
        dec = jnp.exp(tot)
        vc = v[rows, :]
        for h in range(GLA_HEADS):
            sk = slice(h * GLA_DK, (h + 1) * GLA_DK)
            sv = slice(h * GLA_DV, (h + 1) * GLA_DV)
            chains.append(dict(h=h, rows=rows, sv=sv, q=q_dec[:, sk], ki=k_inv[:, sk],
                               ke=k_end[:, sk], dec=dec[:, sk], v=vc[:, sv]))
    for ch in chains:
        ch["sc"] = jnp.where(keep, _dot_nt(ch["q"], ch["ki"]), 0.0)
        ch["kv"] = _dot_tn(ch["v"], ch["ke"])
    state = {}
    for ch in chains:
        s = state[ch["h"]] if ch["h"] in state else s_scr[ch["h"]]
        ch["s"] = s
        state[ch["h"]] = s * ch["dec"] + ch["kv"]
    for h, s in state.items():
        s_scr[h] = s
    for ch in chains:
        emit(ch["rows"], ch["sv"], _dot(ch["sc"], ch["v"]) + _dot_nt(ch["q"], ch["s"]))


def _glafwd_kernel(q_cur, q_prev, q_next, ald_ref, conv, up, bias, s0_ref,
                   o_q, o_k, o_v, o_la, o_of, sfin_ref, s_scr, *, nb):
    j = pl.program_id(1)

    @pl.when(j == 0)
    def _():
        s_scr[...] = s0_ref[...]

    x = q_cur[...].astype(F32)
    prev, nxt = _shifted(x, q_prev[...], q_next[...], j == 0, j == pl.num_programs(1) - 1)
    y = _silu(conv[0:1, :] * prev + conv[1:2, :] * x + conv[2:3, :] * nxt)
    q = (y[:, 0:GLA_KW] * (GLA_DK ** -0.5)).astype(BF16)
    k = y[:, GLA_KW:2 * GLA_KW].astype(BF16)
    v = y[:, 2 * GLA_KW:].astype(BF16)
    o_q[...] = q
    o_k[...] = k
    o_v[...] = v
    ald = ald_ref[...]
    log_alpha = lambda di: _log_sigmoid(_dot(ald, up[di]) + bias[di:di + 1, :]) / GLA_TAU
    o_la[...] = log_alpha(1)

    def emit(rows, lanes, o):
        o_of[rows, lanes] = o.astype(o_of.dtype)

    _gla_block_scan(q, k, v, log_alpha(0), s_scr, nb, False, emit)

    @pl.when(j == pl.num_programs(1) - 1)
    def _():
        sfin_ref[...] = s_scr[...]


def _glafwd(qkv, ald, s0, lp):
    bsz, t, fq = qkv.shape
    nb = min(GLA_BLOCK_CHUNKS, t // CHUNK)
    tb = nb * CHUNK
    assert t % tb == 0
    prev_spec, next_spec = _halo_specs(fq, tb, t)
    tok = lambda f: pl.BlockSpec((None, tb, f), lambda b, j: (b, j, 0))
    st = pl.BlockSpec((None,) + GLA_STATE_SHAPE, lambda b, j: (b, 0, 0, 0))
    small = [lp["gla_conv"], lp["gla_alpha_up_pad"], lp["gla_alpha_bias"]]
    act = lambda f, dt: jax.ShapeDtypeStruct((bsz, t, f), dt)
    return pl.pallas_call(
        functools.partial(_glafwd_kernel, nb=nb),
        grid=(bsz, t // tb),
        in_specs=[tok(fq), prev_spec, next_spec, tok(LANES)]
        + [_const_spec(s.shape) for s in small] + [st],
        out_specs=[tok(GLA_KW), tok(GLA_KW), tok(GLA_VW), tok(GLA_KW), tok(GLA_VW), st],
        out_shape=[act(GLA_KW, BF16), act(GLA_KW, BF16), act(GLA_VW, BF16), act(GLA_KW, F32),
                   act(GLA_VW, BF16), jax.ShapeDtypeStruct(s0.shape, F32)],
        scratch_shapes=[pltpu.VMEM(GLA_STATE_SHAPE, F32)],
        compiler_params=_params(2),
        name="glafwd",
    )(qkv, qkv, qkv, ald, *small, s0)


def _glabwd_kernel(q_ref, k_ref, v_ref, la_ref, of_ref, gate_ref, nw, s0_ref,
                   y_ref, sfin_ref, s_scr, o_scr, *, nb, y_tr, a, k):
    j = pl.program_id(1)

    @pl.when(j == 0)
    def _():
        s_scr[...] = s0_ref[...]

    def emit(rows, lanes, o):
        o_scr[rows, lanes] = o

    _gla_block_scan(q_ref, k_ref, v_ref, la_ref, s_scr, nb, True, emit)
    o = of_ref[...].astype(F32) + o_scr[...]
    gate = gate_ref[...].astype(F32)
    ys = []
    for h in range(GLA_HEADS):
        sv = slice(h * GLA_DV, (h + 1) * GLA_DV)
        ys.append(_rms(o[:, sv], nw[...]) * _silu(gate[:, sv]))
    _store_tok(y_ref, jnp.concatenate(ys, axis=1), y_tr, a, k)

    @pl.when(j == pl.num_programs(1) - 1)
    def _():
        sfin_ref[...] = s_scr[...]


def _glabwd(q, k, v, la, of, gate, s0, lp, *, y_tr, a):
    bsz, t, _ = q.shape
    w = GLA_VW
    tb = min(TOKEN_TILE, t)
    nb = tb // CHUNK
    nblk = t // tb
    kk = tb // a
    tok = lambda f: pl.BlockSpec((None, tb, f), lambda b, j: (b, nblk - 1 - j, 0))
    st = pl.BlockSpec((None,) + GLA_STATE_SHAPE, lambda b, j: (b, 0, 0, 0))
    if y_tr:
        y_spec = pl.BlockSpec((None, a, kk, w), lambda b, j: (b, 0, nblk - 1 - j, 0))
    else:
        y_spec = tok(w)
    y, sfin = pl.pallas_call(
        functools.partial(_glabwd_kernel, nb=nb, y_tr=y_tr, a=a, k=kk),
        grid=(bsz, nblk),
        in_specs=[tok(GLA_KW), tok(GLA_KW), tok(w), tok(GLA_KW), tok(w), tok(w),
                  _const_spec((1, GLA_DV)), st],
        out_specs=[y_spec, st],
        out_shape=[_tok_shape(bsz, t, w, y_tr, a, BF16), jax.ShapeDtypeStruct(s0.shape, F32)],
        scratch_shapes=[pltpu.VMEM(GLA_STATE_SHAPE, F32), pltpu.VMEM((tb, w), F32)],
        compiler_params=_params(2),
        name="glabwd",
    )(q, k, v, la, of, gate, lp["gla_norm_w"], s0)
    return _tok_unview(y, y_tr, t, w), sfin


def _merge_kernel(x_ref, yf_ref, yb_ref, g_ref, gb_ref, ygla_ref, mg_ref, mod_ref, gains, gn, hind,
                  w_rwo, w_glao, w_mo, w1, w2, o_ref, *, x_tr, a, k):
    d = x_ref.shape[-1]
    x = _load_tok(x_ref, x_tr, a, k)
    mg = mg_ref[...].astype(F32)
    yrw = _rw_output(yf_ref[...], yb_ref[...], g_ref[...], gb_ref[...], gn, hind[...])
    br = (_sigmoid(mg[:, :d]) * _dot(yrw, w_rwo[...])
          + _sigmoid(mg[:, d:]) * _dot(ygla_ref[...], w_glao[...]))
    m = _dot(br, w_mo[...])
    x1 = x + mod_ref[2:3, :] * _rms(m, gains[1:2, :])
    h2 = _rms(x1, gains[2:3, :]) * (1.0 + mod_ref[4:5, :]) + mod_ref[3:4, :]
    h2b = h2.astype(BF16)
    f = None
    nh = w1.shape[1] // MLP_CHUNKS
    for ci in range(MLP_CHUNKS):
        hid = jnp.maximum(jnp.dot(h2b, w1[:, ci * nh:(ci + 1) * nh],
                                  preferred_element_type=F32), 0.0)
        part = _dot(hid * hid, w2[ci * nh:(ci + 1) * nh, :])
        f = part if f is None else f + part
    x2 = x1 + mod_ref[5:6, :] * _rms(f, gains[3:4, :])
    _store_tok(o_ref, x2, x_tr, a, k)


def _merge(x, rw_parts, ygla, mg, mods, mod_row, gains, gn, hind, ws, *, x_tr, a, tm):
    bsz, t, d = x.shape
    k = tm // a
    kern = functools.partial(_merge_kernel, x_tr=x_tr, a=a, k=k)
    rw_tok = _tok_spec(RW_WIDTH, tm, False, a)
    out = pl.pallas_call(
        kern,
        grid=(bsz, t // tm),
        in_specs=[_tok_spec(d, tm, x_tr, a), rw_tok, rw_tok, rw_tok, rw_tok,
                  _tok_spec(GLA_VW, tm, False, a), _tok_spec(2 * d, tm, False, a),
                  pl.BlockSpec((None, 6, d), lambda b, j: (mod_row(b), 0, 0)),
                  _const_spec(gains.shape), _const_spec(gn.shape), _const_spec(hind.shape)]
        + [_wspec(w) for w in ws],
        out_specs=_tok_spec(d, tm, x_tr, a),
        out_shape=_tok_shape(bsz, t, d, x_tr, a, F32),
        compiler_params=_params(2),
        name="merge",
    )(_tok_view(x, x_tr, a), *rw_parts, ygla, mg, mods, gains, gn, hind, *[_warg(w) for w in ws])
    return _tok_unview(out, x_tr, t, d)


def _stacked_weights(p):
    w_in = p["w_in"]
    g0 = RW_COLS
    g1 = g0 + GLA_QKV_W
    g2 = g1 + 2 * GLA_GATE_RANK
    g3 = g2 + GLA_VW
    bf = lambda w: w.astype(BF16)
    return {
        "w_in_parts": [
            bf(jnp.concatenate([w_in[:, :, :g0], jnp.pad(
                w_in[:, :, g1:g2], ((0, 0), (0, 0), (0, LANES - 2 * GLA_GATE_RANK)))], axis=2)),
            bf(w_in[:, :, g0:g1]),
            bf(w_in[:, :, g2:g3]),
            bf(w_in[:, :, g3:]),
        ],
        "merge_ws": [bf(p[name]) for name in ("rw_out", "gla_out", "merge_out", "mlp_w1", "mlp_w2")],
    }


def _layer_params(l, p, stacked):
    bf = lambda w: w.astype(BF16)
    row = lambda w: w.reshape(1, -1)
    lp = {
        "w_in_parts": [_Stacked(w, l) for w in stacked["w_in_parts"]],
        "rw_mu": row(p["rw_mu"][l]),
        "rw_w0": p["rw_w0"][l],
        "rw_a0": p["rw_a0"][l],
        "rw_g_up": bf(p["rw_g_up"][l]),
        "rw_k_k": row(p["rw_k_k"][l]),
        "rw_k_a": row(p["rw_k_a"][l]),
        "rw_r_k": row(p["rw_r_k"][l]),
        "rw_gn": jnp.stack([p["rw_gn_w"][l], p["rw_gn_b"][l]]),
        "gla_conv": p["gla_conv"][l],
        "gla_alpha_bias": p["gla_alpha_bias"][l],
        "gla_norm_w": row(p["gla_norm_w"][l]),
        "merge_ws": [_Stacked(w, l) for w in stacked["merge_ws"]],
        "gains": jnp.stack([p["norm_mix_pre"][l], p["norm_mix_post"][l],
                            p["norm_ffn_pre"][l], p["norm_ffn_post"][l]]),
    }
    pad_dir = lambda w, r: jnp.stack([jnp.pad(w[di], ((di * r, LANES - (di + 1) * r), (0, 0)))
                                      for di in range(2)])
    lp["rw_w_up_pad"] = bf(pad_dir(p["rw_w_up"][l], p["rw_w_up"].shape[2]))
    lp["rw_a_up_pad"] = bf(pad_dir(p["rw_a_up"][l], p["rw_a_up"].shape[2]))
    lp["gla_alpha_up_pad"] = bf(pad_dir(p["gla_alpha_up"][l], GLA_GATE_RANK))
    if l > 0:
        lp["rw_vres_down"] = bf(p["rw_vres_down"][l - 1])
        lp["rw_vres_up"] = bf(p["rw_vres_up"][l - 1])
        lp["rw_vres_bias"] = row(p["rw_vres_bias"][l - 1])
    return lp


def _head_indicator():
    h = jnp.arange(RW_WIDTH) // RW_HEAD_DIM
    return (h[:, None] == h[None, :]).astype(BF16)


def _mixer(x, mods, mod_row, vfirst, s_rw, s_gla, lp, hind, *, p_col, need_out):
    bsz, t, d = x.shape
    tm_merge = min(MERGE_TILE, t)
    if p_col is None:
        x_tr, g_tr, a_p, a_q = False, False, SUBLANES, SUBLANES
    else:
        x_tr, g_tr = p_col, True
        rows = t // GRID_W
        a_p, a_q = (rows, GRID_W) if p_col else (GRID_W, rows)
    (gq, gald, gg), mg, (r, v, kk, g, gb, lw, kd, bd) = _inproj(
        x, vfirst, mods, mod_row, lp, hind, x_tr=x_tr, g_tr=g_tr, a=a_p, tm=min(PROJ_TILE, t))
    yf, yb, s_rw_out = _rwscan(r, v, kk, lw, kd, bd, s_rw)
    q, k, gv, la_b, of, s_gla_f = _glafwd(gq, gald, s_gla[0], lp)
    ygla, s_gla_b = _glabwd(q, k, gv, la_b, of, gg, s_gla[1], lp, y_tr=g_tr, a=a_q)
    s_gla_out = (s_gla_f, s_gla_b)
    if not need_out:
        return None, v, s_rw_out, s_gla_out
    x_new = _merge(x, (yf, yb, g, gb), ygla, mg, mods, mod_row, lp["gains"], lp["rw_gn"], hind,
                   lp["merge_ws"], x_tr=x_tr, a=a_p, tm=tm_merge)
    return x_new, v, s_rw_out, s_gla_out


def kernel(x, c, ctx, c_ctx, w_in, rw_mu, rw_w0, rw_w_up, rw_a0, rw_a_up, rw_g_up, rw_k_k, rw_k_a,
           rw_r_k, rw_gn_w, rw_gn_b, rw_vres_down, rw_vres_up, rw_vres_bias, rw_out, gla_conv,
           gla_alpha_up, gla_alpha_bias, gla_norm_w, gla_out, merge_out, mlp_w1, mlp_w2, ada_w,
           ada_b, norm_mix_pre, norm_mix_post, norm_ffn_pre, norm_ffn_post):
    p = dict(w_in=w_in, rw_mu=rw_mu, rw_w0=rw_w0, rw_w_up=rw_w_up, rw_a0=rw_a0, rw_a_up=rw_a_up,
             rw_g_up=rw_g_up, rw_k_k=rw_k_k, rw_k_a=rw_k_a, rw_r_k=rw_r_k.reshape(rw_r_k.shape[0], -1),
             rw_gn_w=rw_gn_w, rw_gn_b=rw_gn_b, rw_vres_down=rw_vres_down, rw_vres_up=rw_vres_up,
             rw_vres_bias=rw_vres_bias, rw_out=rw_out, gla_conv=gla_conv, gla_alpha_up=gla_alpha_up,
             gla_alpha_bias=gla_alpha_bias, gla_norm_w=gla_norm_w, gla_out=gla_out,
             merge_out=merge_out, mlp_w1=mlp_w1, mlp_w2=mlp_w2, norm_mix_pre=norm_mix_pre,
             norm_mix_post=norm_mix_post, norm_ffn_pre=norm_ffn_pre, norm_ffn_post=norm_ffn_post)
    bsz, t, d = x.shape
    t_ctx = ctx.shape[1]
    depth = w_in.shape[0]
    assert bsz < SUBLANES and d % LANES == 0
    assert t % (GRID_W * SUBLANES) == 0 and t_ctx % CHUNK == 0
    assert t % min(TOKEN_TILE, t) == 0 and t_ctx % min(TOKEN_TILE, t_ctx) == 0

    cc = jnp.concatenate([c, c_ctx[None, :], jnp.zeros((SUBLANES - 1 - bsz, d), F32)], axis=0)
    mods = _ada_mods(cc, ada_w, ada_b).reshape(depth, SUBLANES, 6, d)
    hind = _head_indicator()
    stacked = _stacked_weights(p)
    lat_row = lambda b: b
    ctx_row = lambda b: bsz

    x_lat, x_ctx = x, ctx
    vf_lat = vf_ctx = None
    for l in range(depth):
        last = l == depth - 1
        lp = _layer_params(l, p, stacked)
        z_rw = jnp.zeros((bsz,) + RW_STATE_SHAPE, F32)
        z_gla = (jnp.zeros((bsz,) + GLA_STATE_SHAPE, F32),) * 2
        x_ctx_new, v_ctx, s_rw, s_gla = _mixer(
            x_ctx, mods[l], ctx_row, vf_ctx, z_rw, z_gla, lp, hind, p_col=None, need_out=not last)
        x_lat, v_lat, _, _ = _mixer(
            x_lat, mods[l], lat_row, vf_lat, s_rw, s_gla, lp, hind,
            p_col=(l % 2 == 1), need_out=True)
        if l == 0:
            vf_lat, vf_ctx = v_lat, v_ctx
        if not last:
            x_ctx = x_ctx_new
    return x_lat
```
